```python
import math
import jax, jax.numpy as jnp
from jax import lax
import numpy as np

D_MODEL = 1024
BATCH = 4
SEQ = 4096
DEPTH = 4

N_MIXERS = 4
RMS_EPS = 1e-6
NEG = -1e30
D_FF = -(-(8 * D_MODEL) // (3 * 256)) * 256

HEAD_DIM = 64
ATT_HEADS = D_MODEL // HEAD_DIM
ROPE_DIM = HEAD_DIM // 4
ROPE_THETA = 500000.0

S5_GROUP = 16
S5_GROUPS = D_MODEL // S5_GROUP
S5_STATE = 64
S5_DT_MIN = 1e-3
S5_DT_MAX = 1e-1

DIL_PATTERNS = ((128, 1), (512, 4), (2048, 16))
DIL_BLOCK = 128

HGRN_HEAD_DIM = 128
HGRN_HEADS = D_MODEL // HGRN_HEAD_DIM
HGRN_CHUNK = 64

MOBA_BLOCK = 256
MOBA_TOPK = 3
MOBA_Q_CHUNK = 32

kernel_name = "hybrid_s5_dilated_hgrn2_moba_trunk"


def n_layers_of(mixer):
    return len(range(mixer, DEPTH, N_MIXERS))


def rmsnorm(x, g):
    xf = x.astype(jnp.float32)
    y = xf * lax.rsqrt(jnp.mean(xf * xf, axis=-1, keepdims=True) + RMS_EPS)
    return (y * g.astype(jnp.float32)).astype(x.dtype)


def rope_tables(positions):
    inv = ROPE_THETA ** (-jnp.arange(0, ROPE_DIM, 2, dtype=jnp.float32) / ROPE_DIM)
    ang = positions.astype(jnp.float32)[..., None] * inv
    return jnp.cos(ang)[:, :, None, :], jnp.sin(ang)[:, :, None, :]


def apply_partial_rope(x, cos, sin):
    half = ROPE_DIM // 2
    x1, x2, rest = x[..., :half], x[..., half:ROPE_DIM], x[..., ROPE_DIM:]
    return jnp.concatenate([x1 * cos - x2 * sin, x2 * cos + x1 * sin, rest], axis=-1)


def swiglu(xn, w_gate_up, w_down):
    gate, up = jnp.split(xn @ w_gate_up, 2, axis=-1)
    return (jax.nn.silu(gate) * up) @ w_down


def _complex_affine_combine(left, right):
    a1r, a1i, b1r, b1i = left
    a2r, a2i, b2r, b2i = right
    return (a2r * a1r - a2i * a1i, a2r * a1i + a2i * a1r,
            a2r * b1r - a2i * b1i + b2r, a2r * b1i + a2i * b1r + b2i)


def s5_mixer(u, a_re, a_im, log_dt, b_re, b_im, c_re, c_im, d_skip, w_glu, b_glu):
    bsz, seq, _ = u.shape
    f32 = jnp.float32
    uf = u.astype(f32)
    ug = uf.reshape(bsz, seq, S5_GROUPS, S5_GROUP)
    lr, li = a_re.astype(f32), a_im.astype(f32)
    dt = jnp.exp(log_dt.astype(f32))[:, None]
    mag = jnp.exp(lr * dt)
    ab_re, ab_im = mag * jnp.cos(li * dt), mag * jnp.sin(li * dt)
    den = lr * lr + li * li
    m_re = ab_re - 1.0
    f_re = (m_re * lr + ab_im * li) / den
    f_im = (ab_im * lr - m_re * li) / den
    bu_re = jnp.einsum('bsgc,gpc->bsgp', ug, b_re.astype(f32))
    bu_im = jnp.einsum('bsgc,gpc->bsgp', ug, b_im.astype(f32))
    e_re = f_re * bu_re - f_im * bu_im
    e_im = f_re * bu_im + f_im * bu_re
    a_t_re = jnp.broadcast_to(ab_re, e_re.shape)
    a_t_im = jnp.broadcast_to(ab_im, e_im.shape)
    _, _, h_re, h_im = lax.associative_scan(_complex_affine_combine, (a_t_re, a_t_im, e_re, e_im), axis=1)
    y = (jnp.einsum('bsgp,gcp->bsgc', h_re, c_re.astype(f32))
         - jnp.einsum('bsgp,gcp->bsgc', h_im, c_im.astype(f32)))
    y = y.reshape(bsz, seq, D_MODEL) + d_skip.astype(f32) * uf
    z = jax.nn.gelu(y)
    val, gate = jnp.split(z @ w_glu.astype(f32) + b_glu.astype(f32), 2, axis=-1)
    return (val * jax.nn.sigmoid(gate)).astype(u.dtype)


def dilated_branch(q, k, v, window, dilation):
    bsz, seq, heads, hd = q.shape
    n_back = window // dilation
    blk = DIL_BLOCK
    padded = -(-seq // (dilation * blk)) * (dilation * blk)
    sub_len = padded // dilation
    n_blk = sub_len // blk

    def by_stride(t):
        t = jnp.pad(t, ((0, 0), (0, padded - seq), (0, 0), (0, 0)))
        t = t.reshape(bsz, sub_len, dilation, heads, hd).transpose(0, 2, 1, 3, 4)
        return t.reshape(bsz, dilation, n_blk, blk, heads, hd)

    def with_prev(t):
        prev = jnp.pad(t[:, :, :-1], ((0, 0), (0, 0), (1, 0), (0, 0), (0, 0), (0, 0)))
        return jnp.concatenate([prev, t], axis=3)

    qb = by_stride(q)
    kk, vv = with_prev(by_stride(k)), with_prev(by_stride(v))
    qi = jnp.arange(blk)[:, None]
    kj = jnp.arange(2 * blk)[None, :]
    rel = qi + blk - kj
    band = (rel >= 0) & (rel <= n_back)
    mask = band[None] & ((jnp.arange(n_blk) > 0)[:, None, None] | (kj >= blk)[None])
    s = jnp.einsum('brnqhd,brnkhd->brnhqk', qb, kk) * (hd ** -0.5)
    s = jnp.where(mask[None, None, :, None], s, NEG)
    lse = jax.nn.logsumexp(s, axis=-1)
    p = jnp.exp(s - lse[..., None])
    o = jnp.einsum('brnhqk,brnkhd->brnqhd', p, vv)

    def unstride(t):
        tail = t.shape[4:]
        t = t.reshape((bsz, dilation, sub_len) + tail)
        t = jnp.moveaxis(t, 1, 2).reshape((bsz, padded) + tail)
        return t[:, :seq]

    return unstride(o), unstride(jnp.moveaxis(lse, 3, 4))


def dilated_mixer(xn, w_qkv, w_o, cos, sin):
    bsz, seq, _ = xn.shape
    qkv = (xn @ w_qkv).astype(jnp.float32).reshape(bsz, seq, 3, ATT_HEADS, HEAD_DIM)
    q = apply_partial_rope(qkv[:, :, 0], cos, sin)
    k = apply_partial_rope(qkv[:, :, 1], cos, sin)
    v = qkv[:, :, 2]
    outs, lses = [], []
    for window, dilation in DIL_PATTERNS:
        o, l = dilated_branch(q, k, v, window, dilation)
        outs.append(o)
        lses.append(l)
    w = jax.nn.softmax(jnp.stack(lses, axis=0), axis=0)
    o = jnp.einsum('gbsh,gbshd->bshd', w, jnp.stack(outs, axis=0))
    return o.reshape(bsz, seq, D_MODEL).astype(xn.dtype) @ w_o


def hgrn2_mixer(xn, w_in, lower_bound, norm_g, w_o):
    bsz, seq, _ = xn.shape
    f32 = jnp.float32
    q, f, i, g = jnp.split((xn @ w_in).astype(f32), 4, axis=-1)
    q = jax.nn.silu(q)
    lb = lower_bound.astype(f32)
    fg = lb + (1.0 - lb) * jax.nn.sigmoid(f)
    k = 1.0 - fg
    log_f = jnp.log(fg)
    n_chunks = seq // HGRN_CHUNK

    def to_chunks(t):
        t = t.reshape(bsz, n_chunks, HGRN_CHUNK, HGRN_HEADS, HGRN_HEAD_DIM)
        return t.transpose(1, 0, 3, 2, 4)

    causal = jnp.tril(jnp.ones((HGRN_CHUNK, HGRN_CHUNK), dtype=bool))

    def chunk_step(state, inp):
        qc, kc, ic, lc = inp
        b = jnp.cumsum(lc, axis=2)
        o_inter = jnp.einsum('bhtk,bhkv->bhtv', qc * jnp.exp(b), state)
        diff = jnp.where(causal[:, :, None], b[:, :, :, None, :] - b[:, :, None, :, :], NEG)
        attn = jnp.einsum('bhtk,bhsk,bhtsk->bhts', qc, kc, jnp.exp(diff))
        o_intra = jnp.einsum('bhts,bhsv->bhtv', attn, ic)
        b_last = b[:, :, -1:, :]
        state = (jnp.exp(b_last[:, :, 0, :, None]) * state
                 + jnp.einsum('bhsk,bhsv->bhkv', kc * jnp.exp(b_last - b), ic))
        return state, o_inter + o_intra

    state0 = jnp.zeros((bsz, HGRN_HEADS, HGRN_HEAD_DIM, HGRN_HEAD_DIM), f32)
    _, o = lax.scan(chunk_step, state0, (to_chunks(q), to_chunks(k), to_chunks(i), to_chunks(log_f)))
    o = o.transpose(1, 0, 3, 2, 4).reshape(bsz, seq, HGRN_HEADS, HGRN_HEAD_DIM)
    o = o * lax.rsqrt(jnp.mean(o * o, axis=-1, keepdims=True) + RMS_EPS) * norm_g.astype(f32)
    o = o.reshape(bsz, seq, D_MODEL) * jax.nn.silu(g)
    return o.astype(xn.dtype) @ w_o


def moba_mixer(xn, w_qkv, w_o, cos, sin):
    bsz, seq, _ = xn.shape
    qkv = (xn @ w_qkv).astype(jnp.float32).reshape(bsz, seq, 3, ATT_HEADS, HEAD_DIM)
    padded = -(-seq // MOBA_BLOCK) * MOBA_BLOCK
    n_blk = padded // MOBA_BLOCK

    def heads_first(t):
        return jnp.pad(t, ((0, 0), (0, padded - seq), (0, 0), (0, 0))).transpose(0, 2, 1, 3)

    q = heads_first(apply_partial_rope(qkv[:, :, 0], cos, sin))
    k = heads_first(apply_partial_rope(qkv[:, :, 1], cos, sin))
    v = heads_first(qkv[:, :, 2])
    scale = HEAD_DIM ** -0.5
    kb = k.reshape(bsz, ATT_HEADS, n_blk, MOBA_BLOCK, HEAD_DIM)
    vb = v.reshape(bsz, ATT_HEADS, n_blk, MOBA_BLOCK, HEAD_DIM)
    n_sel = min(MOBA_TOPK, n_blk - 1)
    q_blk = jnp.arange(padded) // MOBA_BLOCK
    if n_sel > 0:
        gate = jnp.einsum('bhsd,bhnd->bhsn', q, kb.mean(axis=3))
        past = jnp.arange(n_blk)[None, :] < q_blk[:, None]
        _, sel = lax.top_k(jnp.where(past, gate, NEG), n_sel)
    else:
        sel = jnp.zeros((bsz, ATT_HEADS, padded, 0), jnp.int32)
    sel_ok = sel < q_blk[:, None]
    n_chunks = padded // MOBA_Q_CHUNK

    def chunkify(t):
        return jnp.moveaxis(t.reshape(t.shape[:2] + (n_chunks, MOBA_Q_CHUNK) + t.shape[3:]), 2, 0)

    bi = jnp.arange(bsz)[:, None, None, None]
    hi = jnp.arange(ATT_HEADS)[None, :, None, None]
    q_local = jnp.arange(MOBA_Q_CHUNK)
    k_local = jnp.arange(MOBA_BLOCK)
    n_past_keys = n_sel * MOBA_BLOCK

    def attend_chunk(args):
        c, qc, sc, okc = args
        kg, vg = kb[bi, hi, sc], vb[bi, hi, sc]
        s_past = jnp.einsum('bhqd,bhqjkd->bhqjk', qc, kg) * scale
        s_past = jnp.where(okc[..., None], s_past, NEG).reshape(bsz, ATT_HEADS, MOBA_Q_CHUNK, n_past_keys)
        start = (c * MOBA_Q_CHUNK // MOBA_BLOCK) * MOBA_BLOCK
        ko = lax.dynamic_slice_in_dim(k, start, MOBA_BLOCK, axis=2)
        vo = lax.dynamic_slice_in_dim(v, start, MOBA_BLOCK, axis=2)
        s_own = jnp.einsum('bhqd,bhkd->bhqk', qc, ko) * scale
        s_own = jnp.where((start + k_local)[None, :] <= (c * MOBA_Q_CHUNK + q_local)[:, None], s_own, NEG)
        p = jax.nn.softmax(jnp.concatenate([s_past, s_own], axis=-1), axis=-1)
        p_past = p[..., :n_past_keys].reshape(bsz, ATT_HEADS, MOBA_Q_CHUNK, n_sel, MOBA_BLOCK)
        return (jnp.einsum('bhqjk,bhqjkd->bhqd', p_past, vg)
                + jnp.einsum('bhqk,bhkd->bhqd', p[..., n_past_keys:], vo))

    o = lax.map(attend_chunk, (jnp.arange(n_chunks), chunkify(q), chunkify(sel), chunkify(sel_ok)))
    o = jnp.moveaxis(o, 0, 2).reshape(bsz, ATT_HEADS, padded, HEAD_DIM)[:, :, :seq]
    o = o.transpose(0, 2, 1, 3).reshape(bsz, seq, D_MODEL)
    return o.astype(xn.dtype) @ w_o


def setup_inputs(seed: int = 0) -> dict:
    key = jax.random.key(seed)
    ks = iter(jax.random.split(key, 40))

    def nrm(shape, scale):
        return jax.random.normal(next(ks), shape, jnp.float32) * scale

    na, nb, nc, nd = (n_layers_of(m) for m in range(N_MIXERS))
    D, G, P, C = D_MODEL, S5_GROUPS, S5_STATE, S5_GROUP
    x = nrm((BATCH, SEQ, D), 1.0)
    positions = (jnp.arange(SEQ, dtype=jnp.int32)[None, :]
                 + jax.random.randint(next(ks), (BATCH, 1), 0, 1024, dtype=jnp.int32))
    norm_mix = 1.0 + nrm((DEPTH, D), 0.02)
    norm_ffn = 1.0 + nrm((DEPTH, D), 0.02)
    norm_final = 1.0 + nrm((D,), 0.02)
    s5_a_re = -0.5 + nrm((na, G, P), 0.01)
    s5_a_im = jnp.pi * jnp.arange(P, dtype=jnp.float32) + nrm((na, G, P), 0.01)
    s5_log_dt = jax.random.uniform(next(ks), (na, G), jnp.float32, math.log(S5_DT_MIN), math.log(S5_DT_MAX))
    s5_b_re = nrm((na, G, P, C), (2 * C) ** -0.5)
    s5_b_im = nrm((na, G, P, C), (2 * C) ** -0.5)
    s5_c_re = nrm((na, G, C, P), (2 * P) ** -0.5)
    s5_c_im = nrm((na, G, C, P), (2 * P) ** -0.5)
    s5_d = nrm((na, D), 1.0)
    s5_w_glu = nrm((na, D, 2 * D), D ** -0.5)
    s5_b_glu = nrm((na, 2 * D), 0.01)
    dil_w_qkv = nrm((nb, D, 3 * D), D ** -0.5)
    dil_w_o = nrm((nb, D, D), D ** -0.5)
    hgrn_w_in = nrm((nc, D, 4 * D), D ** -0.5)
    hgrn_lower_bound = nrm((DEPTH, D), 0.1)
    hgrn_norm = 1.0 + nrm((nc, HGRN_HEAD_DIM), 0.02)
    hgrn_w_o = nrm((nc, D, D), D ** -0.5)
    moba_w_qkv = nrm((nd, D, 3 * D), D ** -0.5)
    moba_w_o = nrm((nd, D, D), D ** -0.5)
    ffn_w_gate_up = nrm((DEPTH, D, 2 * D_FF), D ** -0.5)
    ffn_w_down = nrm((DEPTH, D_FF, D), D_FF ** -0.5)
    return {"x": x, "positions": positions, "norm_mix": norm_mix, "norm_ffn": norm_ffn,
            "norm_final": norm_final, "s5_a_re": s5_a_re, "s5_a_im": s5_a_im, "s5_log_dt": s5_log_dt,
            "s5_b_re": s5_b_re, "s5_b_im": s5_b_im, "s5_c_re": s5_c_re, "s5_c_im": s5_c_im,
            "s5_d": s5_d, "s5_w_glu": s5_w_glu, "s5_b_glu": s5_b_glu, "dil_w_qkv": dil_w_qkv,
            "dil_w_o": dil_w_o, "hgrn_w_in": hgrn_w_in, "hgrn_lower_bound": hgrn_lower_bound,
            "hgrn_norm": hgrn_norm, "hgrn_w_o": hgrn_w_o, "moba_w_qkv": moba_w_qkv, "moba_w_o": moba_w_o,
            "ffn_w_gate_up": ffn_w_gate_up, "ffn_w_down": ffn_w_down}


def reference(x, positions, norm_mix, norm_ffn, norm_final, s5_a_re, s5_a_im, s5_log_dt, s5_b_re, s5_b_im,
              s5_c_re, s5_c_im, s5_d, s5_w_glu, s5_b_glu, dil_w_qkv, dil_w_o, hgrn_w_in, hgrn_lower_bound,
              hgrn_norm, hgrn_w_o, moba_w_qkv, moba_w_o, ffn_w_gate_up, ffn_w_down):
    cos, sin = rope_tables(positions)
    lb_w = jax.nn.softmax(hgrn_lower_bound.astype(jnp.float32), axis=0)
    lower_bounds = jnp.cumsum(lb_w, axis=0) - lb_w[0]
    h = x
    for layer in range(DEPTH):
        mixer = layer % N_MIXERS
        j = layer // N_MIXERS
        xn = rmsnorm(h, norm_mix[layer])
        if mixer == 0:
            y = s5_mixer(xn, s5_a_re[j], s5_a_im[j], s5_log_dt[j], s5_b_re[j], s5_b_im[j],
                         s5_c_re[j], s5_c_im[j], s5_d[j], s5_w_glu[j], s5_b_glu[j])
        elif mixer == 1:
            y = dilated_mixer(xn, dil_w_qkv[j], dil_w_o[j], cos, sin)
        elif mixer == 2:
            y = hgrn2_mixer(xn, hgrn_w_in[j], lower_bounds[layer], hgrn_norm[j], hgrn_w_o[j])
        else:
            y = moba_mixer(xn, moba_w_qkv[j], moba_w_o[j], cos, sin)
        h = h + y.astype(h.dtype)
        h = h + swiglu(rmsnorm(h, norm_ffn[layer]), ffn_w_gate_up[layer], ffn_w_down[layer]).astype(h.dtype)
    return rmsnorm(h, norm_final)
```

```python
import functools
import math

import jax
import jax.numpy as jnp
from jax import lax
from jax.experimental import pallas as pl
from jax.experimental.pallas import tpu as pltpu

F32 = jnp.float32
BF16 = jnp.bfloat16

D_MODEL = 1024
D_FF = 2816
RMS_EPS = 1e-6
NEG = -1e30
BELOW_NEG = -3e38

HEAD_DIM = 64
ATT_HEADS = D_MODEL // HEAD_DIM
ROPE_DIM = HEAD_DIM // 4
ROPE_THETA = 500000.0
LANES = 128
SUBLANES = 8
HEADS_PER_TILE = LANES // HEAD_DIM

S5_GROUP = 16
S5_GROUPS = D_MODEL // S5_GROUP
S5_STATE = 64
S5_CH = S5_GROUPS * S5_STATE
S5_SLABS = D_MODEL // LANES
S5_SLAB_CH = S5_CH // S5_SLABS

DIL_PATTERNS = ((128, 1), (512, 4), (2048, 16))
DIL_BLOCK = 128

HGRN_HEAD_DIM = 128
HGRN_HEADS = D_MODEL // HGRN_HEAD_DIM
HGRN_CHUNK = 64
HGRN_SUB = 16

MOBA_BLOCK = 256
MOBA_TOPK = 3

VMEM_LIMIT = 56 * 1024 * 1024

NT_DIMS = (((1,), (1,)), ((), ()))
TN_DIMS = (((0,), (0,)), ((), ()))


def _cparams(*sem):
    return pltpu.CompilerParams(dimension_semantics=sem, vmem_limit_bytes=VMEM_LIMIT)


def _rms(x, g):
    return x * lax.rsqrt(jnp.mean(x * x, axis=-1, keepdims=True) + RMS_EPS) * g


def _silu(x):
    return x * jax.nn.sigmoid(x)


def _proj_kernel(*refs, n_out, tn, rope_cols):
    if rope_cols:
        x_ref, g_ref, w_ref, c_ref, s1_ref, s2_ref, o_ref = refs
    else:
        x_ref, g_ref, w_ref, o_ref = refs
    x = x_ref[...]
    xn = _rms(x, g_ref[...]).astype(BF16)
    for c in range(n_out // tn):
        y = jnp.dot(xn, w_ref[:, c * tn:(c + 1) * tn], preferred_element_type=F32)
        if c * tn < rope_cols:
            cos, s1, s2 = c_ref[...], s1_ref[...], s2_ref[...]
            parts = []
            for j in range(tn // LANES):
                yj = y[:, j * LANES:(j + 1) * LANES]
                half = ROPE_DIM // 2
                parts.append(yj * cos + pltpu.roll(yj, LANES - half, 1) * s1 + pltpu.roll(yj, half, 1) * s2)
            y = jnp.concatenate(parts, axis=1) if len(parts) > 1 else parts[0]
        o_ref[:, c * tn:(c + 1) * tn] = y


def _proj(h, g, w, rope=None, rope_cols=0, tm=256, tn=256):
    t, d = h.shape
    n_out = w.shape[1]
    in_specs = [pl.BlockSpec((tm, d), lambda i: (i, 0)),
                pl.BlockSpec((1, d), lambda i: (0, 0)),
                pl.BlockSpec((d, n_out), lambda i: (0, 0))]
    args = [h, g.reshape(1, d), w]
    if rope_cols:
        in_specs += [pl.BlockSpec((tm, LANES), lambda i: (i, 0))] * 3
        args += list(rope)
    return pl.pallas_call(
        functools.partial(_proj_kernel, n_out=n_out, tn=tn, rope_cols=rope_cols),
        grid=(t // tm,),
        in_specs=in_specs,
        out_specs=pl.BlockSpec((tm, n_out), lambda i: (i, 0)),
        out_shape=jax.ShapeDtypeStruct((t, n_out), F32),
        compiler_params=_cparams("parallel"),
        name="proj_rope" if rope_cols else "proj",
    )(*args)


def _ffn_kernel(*refs, has_mix, has_final, fc):
    refs = list(refs)
    h_ref, g_ref, wgu_ref, wd_ref = refs[:4]
    pos = 4
    if has_mix:
        a_ref, wo_ref = refs[pos:pos + 2]
        pos += 2
    if has_final:
        fg_ref = refs[pos]
        pos += 1
    o_ref, acc_ref = refs[pos], refs[pos + 1]

    h = h_ref[...]
    if has_mix:
        h = h + jnp.dot(a_ref[...], wo_ref[...], preferred_element_type=F32)
    xn = _rms(h, g_ref[...]).astype(BF16)
    for c in range(D_FF // fc):
        gate = jnp.dot(xn, wgu_ref[:, c * fc:(c + 1) * fc], preferred_element_type=F32)
        up = jnp.dot(xn, wgu_ref[:, D_FF + c * fc:D_FF + (c + 1) * fc], preferred_element_type=F32)
        act = (_silu(gate) * up).astype(BF16)
        contrib = jnp.dot(act, wd_ref[c * fc:(c + 1) * fc, :], preferred_element_type=F32)
        if c == 0:
            acc_ref[...] = contrib
        else:
            acc_ref[...] += contrib
    out = h + acc_ref[...]
    if has_final:
        out = _rms(out, fg_ref[...])
    o_ref[...] = out


def _ffn(h, g, wgu, wd, mix=None, final_g=None, tm=256, fc=256):
    t, d = h.shape
    const = lambda i: (0, 0)
    in_specs = [pl.BlockSpec((tm, d), lambda i: (i, 0)),
                pl.BlockSpec((1, d), const),
                pl.BlockSpec((d, 2 * D_FF), const),
                pl.BlockSpec((D_FF, d), const)]
    args = [h, g.reshape(1, d), wgu, wd]
    if mix is not None:
        a, wo = mix
        in_specs += [pl.BlockSpec((tm, d), lambda i: (i, 0)), pl.BlockSpec((d, d), const)]
        args += [a, wo]
    if final_g is not None:
        in_specs += [pl.BlockSpec((1, d), const)]
        args += [final_g.reshape(1, d)]
    return pl.pallas_call(
        functools.partial(_ffn_kernel, has_mix=mix is not None, has_final=final_g is not None, fc=fc),
        grid=(t // tm,),
        in_specs=in_specs,
        out_specs=pl.BlockSpec((tm, d), lambda i: (i, 0)),
        out_shape=jax.ShapeDtypeStruct((t, d), F32),
        scratch_shapes=[pltpu.VMEM((tm, d), F32)],
        compiler_params=_cparams("parallel"),
        name="ffn",
    )(*args)


def _s5_kernel(x_ref, g_ref, wbr_ref, wbi_ref, akr_ref, aki_ref, pr_ref, pi_ref, wcr_ref, wci_ref,
               d_ref, wglu_ref, bglu_ref, o_ref, er_ref, ei_ref, cr_ref, ci_ref, *, tm, tiles_per_seq):
    i = pl.program_id(0)

    @pl.when(i % tiles_per_seq == 0)
    def _():
        cr_ref[...] = jnp.zeros_like(cr_ref)
        ci_ref[...] = jnp.zeros_like(ci_ref)

    x = x_ref[...]
    u = _rms(x, g_ref[...])
    ub = u.astype(BF16)
    for s in range(S5_SLABS):
        us = ub[:, s * LANES:(s + 1) * LANES]
        er_ref[:, s * S5_SLAB_CH:(s + 1) * S5_SLAB_CH] = jnp.dot(us, wbr_ref[s], preferred_element_type=F32)
        ei_ref[:, s * S5_SLAB_CH:(s + 1) * S5_SLAB_CH] = jnp.dot(us, wbi_ref[s], preferred_element_type=F32)

    def group(j, carry):
        r0 = pl.multiple_of(j * SUBLANES, SUBLANES)
        xr = er_ref[pl.ds(r0, SUBLANES), :]
        xi = ei_ref[pl.ds(r0, SUBLANES), :]
        for lvl in range(3):
            k = 1 << lvl
            sr = pltpu.roll(xr, k, 0)
            si = pltpu.roll(xi, k, 0)
            ar, ai = akr_ref[lvl], aki_ref[lvl]
            xr, xi = xr + ar * sr - ai * si, xi + ar * si + ai * sr
        cr, ci = cr_ref[...], ci_ref[...]
        pr, pi_ = pr_ref[...], pi_ref[...]
        hr = xr + pr * cr - pi_ * ci
        hi = xi + pr * ci + pi_ * cr
        er_ref[pl.ds(r0, SUBLANES), :] = hr
        ei_ref[pl.ds(r0, SUBLANES), :] = hi
        cr_ref[...] = jnp.broadcast_to(hr[SUBLANES - 1:SUBLANES, :], (SUBLANES, S5_CH))
        ci_ref[...] = jnp.broadcast_to(hi[SUBLANES - 1:SUBLANES, :], (SUBLANES, S5_CH))
        return carry

    lax.fori_loop(0, tm // SUBLANES, group, 0)

    ys = []
    for s in range(S5_SLABS):
        hr = er_ref[:, s * S5_SLAB_CH:(s + 1) * S5_SLAB_CH].astype(BF16)
        hi = ei_ref[:, s * S5_SLAB_CH:(s + 1) * S5_SLAB_CH].astype(BF16)
        ys.append(jnp.dot(hr, wcr_ref[s], preferred_element_type=F32)
                  - jnp.dot(hi, wci_ref[s], preferred_element_type=F32))
    y = jnp.concatenate(ys, axis=1) + d_ref[...] * u
    z = jax.nn.gelu(y).astype(BF16)
    zz = jnp.dot(z, wglu_ref[...], preferred_element_type=F32) + bglu_ref[...]
    o_ref[...] = x + zz[:, :D_MODEL] * jax.nn.sigmoid(zz[:, D_MODEL:])


def _block_diag_slabs(w):
    g, r, c = w.shape
    per = g // S5_SLABS
    w = w.reshape(S5_SLABS, per, r, c)
    eye = jnp.eye(per, dtype=w.dtype)
    return jnp.einsum('sgrc,gh->sgrhc', w, eye).reshape(S5_SLABS, per * r, per * c)


def _s5_tables(a_re, a_im, log_dt):
    lr, li = a_re.astype(F32), a_im.astype(F32)
    dt = jnp.exp(log_dt.astype(F32))[:, None]
    mag = jnp.exp(lr * dt)
    ab_re, ab_im = mag * jnp.cos(li * dt), mag * jnp.sin(li * dt)
    den = lr * lr + li * li
    m_re = ab_re - 1.0
    f_re = (m_re * lr + ab_im * li) / den
    f_im = (ab_im * lr - m_re * li) / den

    def power(k):
        return ((jnp.exp(lr * dt * k) * jnp.cos(li * dt * k)).reshape(-1),
                (jnp.exp(lr * dt * k) * jnp.sin(li * dt * k)).reshape(-1))

    rows = jnp.arange(SUBLANES)[:, None]
    akr, aki = [], []
    for k in (1, 2, 4):
        pr, pi_ = power(float(k))
        akr.append(jnp.where(rows >= k, pr[None, :], 0.0))
        aki.append(jnp.where(rows >= k, pi_[None, :], 0.0))
    pw = [power(float(k + 1)) for k in range(SUBLANES)]
    p_re = jnp.stack([p[0] for p in pw])
    p_im = jnp.stack([p[1] for p in pw])
    return f_re, f_im, jnp.stack(akr), jnp.stack(aki), p_re, p_im


def _s5(h, g, a_re, a_im, log_dt, b_re, b_im, c_re, c_im, d_skip, w_glu, b_glu, seq, tm=256):
    t, d = h.shape
    f_re, f_im, akr, aki, p_re, p_im = _s5_tables(a_re, a_im, log_dt)
    bw_re = f_re[:, :, None] * b_re - f_im[:, :, None] * b_im
    bw_im = f_re[:, :, None] * b_im + f_im[:, :, None] * b_re
    wbr = _block_diag_slabs(jnp.swapaxes(bw_re, 1, 2)).astype(BF16)
    wbi = _block_diag_slabs(jnp.swapaxes(bw_im, 1, 2)).astype(BF16)
    wcr = _block_diag_slabs(jnp.swapaxes(c_re.astype(F32), 1, 2)).astype(BF16)
    wci = _block_diag_slabs(jnp.swapaxes(c_im.astype(F32), 1, 2)).astype(BF16)
    c2 = lambda i: (0, 0)
    c3 = lambda i: (0, 0, 0)
    return pl.pallas_call(
        functools.partial(_s5_kernel, tm=tm, tiles_per_seq=seq // tm),
        grid=(t // tm,),
        in_specs=[pl.BlockSpec((tm, d), lambda i: (i, 0)),
                  pl.BlockSpec((1, d), c2),
                  pl.BlockSpec(wbr.shape, c3), pl.BlockSpec(wbi.shape, c3),
                  pl.BlockSpec(akr.shape, c3), pl.BlockSpec(aki.shape, c3),
                  pl.BlockSpec(p_re.shape, c2), pl.BlockSpec(p_im.shape, c2),
                  pl.BlockSpec(wcr.shape, c3), pl.BlockSpec(wci.shape, c3),
                  pl.BlockSpec((1, d), c2),
                  pl.BlockSpec((d, 2 * d), c2),
                  pl.BlockSpec((1, 2 * d), c2)],
        out_specs=pl.BlockSpec((tm, d), lambda i: (i, 0)),
        out_shape=jax.ShapeDtypeStruct((t, d), F32),
        scratch_shapes=[pltpu.VMEM((tm, S5_CH), F32), pltpu.VMEM((tm, S5_CH), F32),
                        pltpu.VMEM((SUBLANES, S5_CH), F32), pltpu.VMEM((SUBLANES, S5_CH), F32)],
        compiler_params=_cparams("arbitrary"),
        name="s5",
    )(h, g.reshape(1, d), wbr, wbi, akr, aki, p_re, p_im, wcr, wci,
      d_skip.astype(F32).reshape(1, d), w_glu.astype(BF16), b_glu.astype(F32).reshape(1, 2 * d))


def _dil_kernel(q_ref, k_ref, v_ref, o_ref, m_ref, l_ref, acc_ref, *, seq):
    blk = DIL_BLOCK
    row = lax.broadcasted_iota(jnp.int32, (blk, blk), 0)
    col = lax.broadcasted_iota(jnp.int32, (blk, blk), 1)
    mask_cur = col <= row
    mask_prev = col >= row

    for pat, (window, dil) in enumerate(DIL_PATTERNS):
        assert window // dil == blk and seq % (dil * blk) == 0
        n_blk = seq // (dil * blk)

        def step(it, carry, pat=pat, dil=dil, n_blk=n_blk):
            res = it // n_blk
            bi = it % n_blk
            start = res + dil * blk * bi
            pstart = res + dil * blk * jnp.maximum(bi - 1, 0)
            rows = pl.ds(start, blk, stride=dil)
            prows = pl.ds(pstart, blk, stride=dil)
            q = q_ref[rows, :].astype(BF16)
            kc = k_ref[rows, :].astype(BF16)
            vc = v_ref[rows, :].astype(BF16)
            kp = k_ref[prows, :].astype(BF16)
            vp = v_ref[prows, :].astype(BF16)
            has_prev = bi > 0
            ms, ls, os_ = [], [], []
            for hh in range(HEADS_PER_TILE):
                sl = slice(hh * HEAD_DIM, (hh + 1) * HEAD_DIM)
                sc = lax.dot_general(q[:, sl], kc[:, sl], NT_DIMS, preferred_element_type=F32)
                sp = lax.dot_general(q[:, sl], kp[:, sl], NT_DIMS, preferred_element_type=F32)
                sc = jnp.where(mask_cur, sc, NEG)
                sp = jnp.where(jnp.logical_and(mask_prev, has_prev), sp, NEG)
                mb = jnp.maximum(jnp.max(sc, axis=1, keepdims=True), jnp.max(sp, axis=1, keepdims=True))
                pc = jnp.exp(sc - mb)
                pp = jnp.exp(sp - mb)
                lb = jnp.sum(pc, axis=1, keepdims=True) + jnp.sum(pp, axis=1, keepdims=True)
                ob = (jnp.dot(pc.astype(BF16), vc[:, sl], preferred_element_type=F32)
                      + jnp.dot(pp.astype(BF16), vp[:, sl], preferred_element_type=F32))
                ms.append(jnp.broadcast_to(mb, (blk, HEAD_DIM)))
                ls.append(jnp.broadcast_to(lb, (blk, HEAD_DIM)))
                os_.append(ob)
            m_b = jnp.concatenate(ms, axis=1)
            l_b = jnp.concatenate(ls, axis=1)
            o_b = jnp.concatenate(os_, axis=1)
            if pat == 0:
                m_ref[rows, :] = m_b
                l_ref[rows, :] = l_b
                acc_ref[rows, :] = o_b
            else:
                m_o = m_ref[rows, :]
                m_n = jnp.maximum(m_o, m_b)
                a_o = jnp.exp(m_o - m_n)
                a_b = jnp.exp(m_b - m_n)
                m_ref[rows, :] = m_n
                l_ref[rows, :] = l_ref[rows, :] * a_o + l_b * a_b
                acc_ref[rows, :] = acc_ref[rows, :] * a_o + o_b * a_b
            return carry

        lax.fori_loop(0, seq // blk, step, 0)

    def finish(c, carry):
        r0 = pl.multiple_of(c * 256, 256)
        o_ref[pl.ds(r0, 256), :] = (acc_ref[pl.ds(r0, 256), :] / l_ref[pl.ds(r0, 256), :]).astype(o_ref.dtype)
        return carry

    lax.fori_loop(0, seq // 256, finish, 0)


def _dilated(qkv, bsz, seq):
    n_tiles = D_MODEL // LANES
    qkv3 = qkv.reshape(bsz, seq, 3 * D_MODEL)
    spec = lambda off: pl.BlockSpec((None, seq, LANES), lambda b, hp, off=off: (b, 0, off + hp))
    out = pl.pallas_call(
        functools.partial(_dil_kernel, seq=seq),
        grid=(bsz, n_tiles),
        in_specs=[spec(0), spec(n_tiles), spec(2 * n_tiles)],
        out_specs=pl.BlockSpec((None, seq, LANES), lambda b, hp: (b, 0, hp)),
        out_shape=jax.ShapeDtypeStruct((bsz, seq, D_MODEL), BF16),
        scratch_shapes=[pltpu.VMEM((seq, LANES), F32)] * 3,
        compiler_params=_cparams("parallel", "parallel"),
        name="dilated_attn",
    )(qkv3, qkv3, qkv3)
    return out.reshape(bsz * seq, D_MODEL)


def _hgrn_chunk(q, f, iv, lb, st_t, tril):
    sub = HGRN_SUB
    qs = _silu(q)
    fg = lb + (1.0 - lb) * jax.nn.sigmoid(f)
    kk = 1.0 - fg
    bc = jnp.dot(tril, jnp.log(fg), preferred_element_type=F32, precision=lax.Precision.HIGHEST)
    bl = bc[HGRN_CHUNK - 1:HGRN_CHUNK]
    qe = (qs * jnp.exp(bc)).astype(BF16)
    o_inter = lax.dot_general(qe, st_t.astype(BF16), NT_DIMS, preferred_element_type=F32)
    ivb = iv.astype(BF16)
    trow = lax.broadcasted_iota(jnp.int32, (sub, 1), 0)
    outs = []
    for blk in range(HGRN_CHUNK // sub):
        lo = blk * sub
        q_i, b_i = qs[lo:lo + sub], bc[lo:lo + sub]
        o_i = o_inter[lo:lo + sub]
        if blk > 0:
            r_i = bc[lo - 1:lo]
            qt = (q_i * jnp.exp(b_i - r_i)).astype(BF16)
            kt = (kk[:lo] * jnp.exp(r_i - bc[:lo])).astype(BF16)
            att = lax.dot_general(qt, kt, NT_DIMS, preferred_element_type=F32)
            o_i = o_i + jnp.dot(att.astype(BF16), ivb[:lo], preferred_element_type=F32)
        k_i, iv_i = kk[lo:lo + sub], iv[lo:lo + sub]
        for s in range(sub):
            w = jnp.exp(jnp.minimum(b_i - b_i[s:s + 1], 0.0))
            a_col = jnp.sum(q_i * k_i[s:s + 1] * w, axis=1, keepdims=True)
            a_col = jnp.where(trow >= s, a_col, 0.0)
            o_i = o_i + a_col * iv_i[s:s + 1]
        outs.append(o_i)
    o = jnp.concatenate(outs, axis=0)
    kd = (kk * jnp.exp(bl - bc)).astype(BF16)
    st_t = st_t * jnp.exp(bl) + lax.dot_general(ivb, kd, TN_DIMS, preferred_element_type=F32)
    return o, st_t


def _hgrn_kernel(q_ref, f_ref, i_ref, g_ref, lb_ref, ng_ref, o_ref, st_ref, *, tc):
    @pl.when(pl.program_id(2) == 0)
    def _():
        st_ref[...] = jnp.zeros_like(st_ref)

    r = lax.broadcasted_iota(jnp.int32, (HGRN_CHUNK, HGRN_CHUNK), 0)
    c = lax.broadcasted_iota(jnp.int32, (HGRN_CHUNK, HGRN_CHUNK), 1)
    tril = (c <= r).astype(F32)
    lb = lb_ref[...]
    st_t = st_ref[...]
    for ch in range(tc // HGRN_CHUNK):
        rows = slice(ch * HGRN_CHUNK, (ch + 1) * HGRN_CHUNK)
        o, st_t = _hgrn_chunk(q_ref[rows, :], f_ref[rows, :], i_ref[rows, :], lb, st_t, tril)
        o = o * lax.rsqrt(jnp.mean(o * o, axis=-1, keepdims=True) + RMS_EPS) * ng_ref[...]
        o_ref[rows, :] = (o * _silu(g_ref[rows, :])).astype(o_ref.dtype)
    st_ref[...] = st_t


def _hgrn(proj, lb, norm_g, bsz, seq, tc=256):
    t = bsz * seq
    tiles = seq // tc
    hd = HGRN_HEAD_DIM
    spec = lambda off: pl.BlockSpec((tc, hd), lambda b, h, j, off=off: (b * tiles + j, off + h))
    return pl.pallas_call(
        functools.partial(_hgrn_kernel, tc=tc),
        grid=(bsz, HGRN_HEADS, tiles),
        in_specs=[spec(0), spec(HGRN_HEADS), spec(2 * HGRN_HEADS), spec(3 * HGRN_HEADS),
                  pl.BlockSpec((1, hd), lambda b, h, j: (0, h)),
                  pl.BlockSpec((1, hd), lambda b, h, j: (0, 0))],
        out_specs=spec(0),
        out_shape=jax.ShapeDtypeStruct((t, D_MODEL), BF16),
        scratch_shapes=[pltpu.VMEM((hd, hd), F32)],
        compiler_params=_cparams("parallel", "parallel", "arbitrary"),
        name="hgrn2",
    )(proj, proj, proj, proj, lb.reshape(1, D_MODEL), norm_g.astype(F32).reshape(1, hd))


def _moba_kernel(q_ref, k_ref, v_ref, o_ref, km_ref, m_ref, l_ref, acc_ref, *, n_blk, n_sel):
    blk = MOBA_BLOCK
    qi = pl.program_id(2)
    lane = lax.broadcasted_iota(jnp.int32, (1, LANES), 1)

    @pl.when(qi == 0)
    def _():
        for n in range(n_blk):
            km = jnp.sum(k_ref[n * blk:(n + 1) * blk, :], axis=0, keepdims=True) * (1.0 / blk)
            for hh in range(HEADS_PER_TILE):
                in_head = jnp.logical_and(lane >= hh * HEAD_DIM, lane < (hh + 1) * HEAD_DIM)
                km_ref[hh * n_blk + n:hh * n_blk + n + 1, :] = jnp.where(in_head, km, 0.0)

    q = q_ref[...]
    n_cols = HEADS_PER_TILE * n_blk
    gate = lax.dot_general(q, km_ref[...], NT_DIMS, preferred_element_type=F32,
                           precision=lax.Precision.HIGHEST)
    colid = lax.broadcasted_iota(jnp.int32, (blk, n_cols), 1)
    nid = colid % n_blk
    past = nid < qi
    sel = jnp.zeros((blk, n_cols), F32)
    for hh in range(HEADS_PER_TILE):
        in_head = (colid // n_blk) == hh
        g = jnp.where(in_head, jnp.where(past, gate, NEG), BELOW_NEG)
        for _ in range(n_sel):
            mx = jnp.max(g, axis=1, keepdims=True)
            idx = jnp.min(jnp.where(g == mx, colid, n_cols), axis=1, keepdims=True)
            pick = colid == idx
            sel = jnp.where(jnp.logical_and(pick, past), 1.0, sel)
            g = jnp.where(pick, BELOW_NEG, g)

    qb = q.astype(BF16)
    row = lax.broadcasted_iota(jnp.int32, (blk, blk), 0)
    col = lax.broadcasted_iota(jnp.int32, (blk, blk), 1)
    causal = col <= row

    r0 = pl.multiple_of(qi * blk, blk)
    ko = k_ref[pl.ds(r0, blk), :].astype(BF16)
    vo = v_ref[pl.ds(r0, blk), :].astype(BF16)
    for hh in range(HEADS_PER_TILE):
        sl = slice(hh * HEAD_DIM, (hh + 1) * HEAD_DIM)
        s = lax.dot_general(qb[:, sl], ko[:, sl], NT_DIMS, preferred_element_type=F32)
        s = jnp.where(causal, s, NEG)
        mx = jnp.max(s, axis=1, keepdims=True)
        p = jnp.exp(s - mx)
        m_ref[:, sl] = jnp.broadcast_to(mx, (blk, HEAD_DIM))
        l_ref[:, sl] = jnp.broadcast_to(jnp.sum(p, axis=1, keepdims=True), (blk, HEAD_DIM))
        acc_ref[:, sl] = jnp.dot(p.astype(BF16), vo[:, sl], preferred_element_type=F32)

    def past_block(n, carry):
        k0 = pl.multiple_of(n * blk, blk)
        kn = k_ref[pl.ds(k0, blk), :].astype(BF16)
        vn = v_ref[pl.ds(k0, blk), :].astype(BF16)
        for hh in range(HEADS_PER_TILE):
            sl = slice(hh * HEAD_DIM, (hh + 1) * HEAD_DIM)
            chosen = jnp.sum(jnp.where(colid == hh * n_blk + n, sel, 0.0), axis=1, keepdims=True) > 0.0
            s = lax.dot_general(qb[:, sl], kn[:, sl], NT_DIMS, preferred_element_type=F32)
            s = jnp.where(chosen, s, NEG)
            m_o = m_ref[:, sl][:, :1]
            l_o = l_ref[:, sl][:, :1]
            m_n = jnp.maximum(m_o, jnp.max(s, axis=1, keepdims=True))
            alpha = jnp.exp(m_o - m_n)
            p = jnp.exp(s - m_n)
            m_ref[:, sl] = jnp.broadcast_to(m_n, (blk, HEAD_DIM))
            l_ref[:, sl] = jnp.broadcast_to(alpha * l_o + jnp.sum(p, axis=1, keepdims=True), (blk, HEAD_DIM))
            acc_ref[:, sl] = alpha * acc_ref[:, sl] + jnp.dot(p.astype(BF16), vn[:, sl],
                                                             preferred_element_type=F32)
        return carry

    lax.fori_loop(0, qi, past_block, 0)
    o_ref[...] = (acc_ref[...] / l_ref[...]).astype(o_ref.dtype)


def _moba(qkv, bsz, seq):
    blk = MOBA_BLOCK
    assert seq % blk == 0
    n_blk = seq // blk
    n_sel = min(MOBA_TOPK, n_blk - 1)
    n_tiles = D_MODEL // LANES
    qkv3 = qkv.reshape(bsz, seq, 3 * D_MODEL)
    kv_spec = lambda off: pl.BlockSpec((None, seq, LANES), lambda b, hp, i, off=off: (b, 0, off + hp))
    out = pl.pallas_call(
        functools.partial(_moba_kernel, n_blk=n_blk, n_sel=n_sel),
        grid=(bsz, n_tiles, n_blk),
        in_specs=[pl.BlockSpec((None, blk, LANES), lambda b, hp, i: (b, i, hp)),
                  kv_spec(n_tiles), kv_spec(2 * n_tiles)],
        out_specs=pl.BlockSpec((None, blk, LANES), lambda b, hp, i: (b, i, hp)),
        out_shape=jax.ShapeDtypeStruct((bsz, seq, D_MODEL), BF16),
        scratch_shapes=[pltpu.VMEM((HEADS_PER_TILE * n_blk, LANES), F32),
                        pltpu.VMEM((blk, LANES), F32), pltpu.VMEM((blk, LANES), F32),
                        pltpu.VMEM((blk, LANES), F32)],
        compiler_params=_cparams("parallel", "parallel", "arbitrary"),
        name="moba_attn",
    )(qkv3, qkv3, qkv3)
    return out.reshape(bsz * seq, D_MODEL)


def _rope_tables(positions):
    half = ROPE_DIM // 2
    inv = ROPE_THETA ** (-jnp.arange(0, ROPE_DIM, 2, dtype=F32) / ROPE_DIM)
    ang = positions.astype(F32).reshape(-1)[:, None] * inv
    cos, sin = jnp.cos(ang), jnp.sin(ang)
    t = cos.shape[0]
    rest = HEAD_DIM - ROPE_DIM
    c64 = jnp.concatenate([cos, cos, jnp.ones((t, rest), F32)], axis=1)
    s1 = jnp.concatenate([-sin, jnp.zeros((t, HEAD_DIM - half), F32)], axis=1)
    s2 = jnp.concatenate([jnp.zeros((t, half), F32), sin, jnp.zeros((t, rest), F32)], axis=1)
    tile = lambda a: jnp.tile(a, (1, HEADS_PER_TILE))
    return tile(c64), tile(s1), tile(s2)


def _qkv_weight(w):
    scale = jnp.concatenate([jnp.full((D_MODEL,), HEAD_DIM ** -0.5, F32), jnp.ones((2 * D_MODEL,), F32)])
    return (w.astype(F32) * scale[None, :]).astype(BF16)


def kernel(x, positions, norm_mix, norm_ffn, norm_final, s5_a_re, s5_a_im, s5_log_dt, s5_b_re, s5_b_im, s5_c_re, s5_c_im, s5_d, s5_w_glu, s5_b_glu, dil_w_qkv, dil_w_o, hgrn_w_in, hgrn_lower_bound, hgrn_norm, hgrn_w_o, moba_w_qkv, moba_w_o, ffn_w_gate_up, ffn_w_down):
    bsz, seq, d = x.shape
    depth = norm_mix.shape[0]
    n_mixers = 4
    t = bsz * seq
    rope = _rope_tables(positions)
    lb_w = jax.nn.softmax(hgrn_lower_bound.astype(F32), axis=0)
    lower_bounds = jnp.cumsum(lb_w, axis=0) - lb_w[0]
    norm_mix = norm_mix.astype(F32)
    norm_ffn = norm_ffn.astype(F32)
    wgu = ffn_w_gate_up.astype(BF16)
    wdn = ffn_w_down.astype(BF16)

    h = x.reshape(t, d).astype(F32)
    for layer in range(depth):
        mixer, j = layer % n_mixers, layer // n_mixers
        final_g = norm_final.astype(F32) if layer == depth - 1 else None
        mix = None
        if mixer == 0:
            h = _s5(h, norm_mix[layer], s5_a_re[j], s5_a_im[j], s5_log_dt[j], s5_b_re[j], s5_b_im[j],
                    s5_c_re[j], s5_c_im[j], s5_d[j], s5_w_glu[j], s5_b_glu[j], seq)
        elif mixer == 1:
            qkv = _proj(h, norm_mix[layer], _qkv_weight(dil_w_qkv[j]), rope, 2 * D_MODEL)
            mix = (_dilated(qkv, bsz, seq), dil_w_o[j].astype(BF16))
        elif mixer == 2:
            proj = _proj(h, norm_mix[layer], hgrn_w_in[j].astype(BF16))
            mix = (_hgrn(proj, lower_bounds[layer], hgrn_norm[j], bsz, seq), hgrn_w_o[j].astype(BF16))
        else:
            qkv = _proj(h, norm_mix[layer], _qkv_weight(moba_w_qkv[j]), rope, 2 * D_MODEL)
            mix = (_moba(qkv, bsz, seq), moba_w_o[j].astype(BF16))
        h = _ffn(h, norm_ffn[layer], wgu[layer], wdn[layer], mix=mix, final_g=final_g)
    return h.reshape(bsz, seq, d).astype(x.dtype)
```

```python
import functools
import math

import jax
import jax.numpy as jnp
from jax import lax
from jax.experimental import pallas as pl
from jax.experimental.pallas import tpu as pltpu

F32 = jnp.float32
BF16 = jnp.bfloat16

D_MODEL = 1024
D_FF = 2816
RMS_EPS = 1e-6
NEG = -1e30
BELOW_NEG = -3e38

HEAD_DIM = 64
ATT_HEADS = D_MODEL // HEAD_DIM
ROPE_DIM = HEAD_DIM // 4
ROPE_THETA = 500000.0
LANES = 128
SUBLANES = 8
HEADS_PER_TILE = LANES // HEAD_DIM

S5_GROUP = 16
S5_GROUPS = D_MODEL // S5_GROUP
S5_STATE = 64
S5_CH = S5_GROUPS * S5_STATE
S5_SLABS = D_MODEL // LANES
S5_SLAB_CH = S5_CH // S5_SLABS

DIL_PATTERNS = ((128, 1), (512, 4), (2048, 16))
DIL_BLOCK = 128

HGRN_HEAD_DIM = 128
HGRN_HEADS = D_MODEL // HGRN_HEAD_DIM
HGRN_CHUNK = 64
HGRN_SUB = 16

MOBA_BLOCK = 256
MOBA_TOPK = 3

VMEM_LIMIT = 56 * 1024 * 1024

NT_DIMS = (((1,), (1,)), ((), ()))
TN_DIMS = (((0,), (0,)), ((), ()))


def _cparams(*sem):
    return pltpu.CompilerParams(dimension_semantics=sem, vmem_limit_bytes=VMEM_LIMIT)


def _rms(x, g):
    return x * lax.rsqrt(jnp.mean(x * x, axis=-1, keepdims=True) + RMS_EPS) * g


def _silu(x):
    return x * jax.nn.sigmoid(x)


def _proj_kernel(*refs, n_out, tn, rope_cols):
    if rope_cols:
        x_ref, g_ref, w_ref, c_ref, s1_ref, s2_ref, o_ref = refs
    else:
        x_ref, g_ref, w_ref, o_ref = refs
    x = x_ref[...]
    xn = _rms(x, g_ref[...]).astype(BF16)
    for c in range(n_out // tn):
        y = jnp.dot(xn, w_ref[:, c * tn:(c + 1) * tn], preferred_element_type=F32)
        if c * tn < rope_cols:
            cos, s1, s2 = c_ref[...], s1_ref[...], s2_ref[...]
            parts = []
            for j in range(tn // LANES):
                yj = y[:, j * LANES:(j + 1) * LANES]
                half = ROPE_DIM // 2
                parts.append(yj * cos + pltpu.roll(yj, LANES - half, 1) * s1 + pltpu.roll(yj, half, 1) * s2)
            y = jnp.concatenate(parts, axis=1) if len(parts) > 1 else parts[0]
        o_ref[:, c * tn:(c + 1) * tn] = y


def _proj(h, g, w, rope=None, rope_cols=0, tm=256, tn=256):
    t, d = h.shape
    n_out = w.shape[1]
    in_specs = [pl.BlockSpec((tm, d), lambda i: (i, 0)),
                pl.BlockSpec((1, d), lambda i: (0, 0)),
                pl.BlockSpec((d, n_out), lambda i: (0, 0))]
    args = [h, g.reshape(1, d), w]
    if rope_cols:
        in_specs += [pl.BlockSpec((tm, LANES), lambda i: (i, 0))] * 3
        args += list(rope)
    return pl.pallas_call(
        functools.partial(_proj_kernel, n_out=n_out, tn=tn, rope_cols=rope_cols),
        grid=(t // tm,),
        in_specs=in_specs,
        out_specs=pl.BlockSpec((tm, n_out), lambda i: (i, 0)),
        out_shape=jax.ShapeDtypeStruct((t, n_out), F32),
        compiler_params=_cparams("parallel"),
        name="proj_rope" if rope_cols else "proj",
    )(*args)


def _ffn_kernel(*refs, has_mix, has_final, fc):
    refs = list(refs)
    h_ref, g_ref, wgu_ref, wd_ref = refs[:4]
    pos = 4
    if has_mix:
        a_ref, wo_ref = refs[pos:pos + 2]
        pos += 2
    if has_final:
        fg_ref = refs[pos]
        pos += 1
    o_ref, acc_ref = refs[pos], refs[pos + 1]

    h = h_ref[...]
    if has_mix:
        h = h + jnp.dot(a_ref[...], wo_ref[...], preferred_element_type=F32)
    xn = _rms(h, g_ref[...]).astype(BF16)
    for c in range(D_FF // fc):
        gate = jnp.dot(xn, wgu_ref[:, c * fc:(c + 1) * fc], preferred_element_type=F32)
        up = jnp.dot(xn, wgu_ref[:, D_FF + c * fc:D_FF + (c + 1) * fc], preferred_element_type=F32)
        act = (_silu(gate) * up).astype(BF16)
        contrib = jnp.dot(act, wd_ref[c * fc:(c + 1) * fc, :], preferred_element_type=F32)
        if c == 0:
            acc_ref[...] = contrib
        else:
            acc_ref[...] += contrib
    out = h + acc_ref[...]
    if has_final:
        out = _rms(out, fg_ref[...])
    o_ref[...] = out


def _ffn(h, g, wgu, wd, mix=None, final_g=None, tm=256, fc=256):
    t, d = h.shape
    const = lambda i: (0, 0)
    in_specs = [pl.BlockSpec((tm, d), lambda i: (i, 0)),
                pl.BlockSpec((1, d), const),
                pl.BlockSpec((d, 2 * D_FF), const),
                pl.BlockSpec((D_FF, d), const)]
    args = [h, g.reshape(1, d), wgu, wd]
    if mix is not None:
        a, wo = mix
        in_specs += [pl.BlockSpec((tm, d), lambda i: (i, 0)), pl.BlockSpec((d, d), const)]
        args += [a, wo]
    if final_g is not None:
        in_specs += [pl.BlockSpec((1, d), const)]
        args += [final_g.reshape(1, d)]
    return pl.pallas_call(
        functools.partial(_ffn_kernel, has_mix=mix is not None, has_final=final_g is not None, fc=fc),
        grid=(t // tm,),
        in_specs=in_specs,
        out_specs=pl.BlockSpec((tm, d), lambda i: (i, 0)),
        out_shape=jax.ShapeDtypeStruct((t, d), F32),
        scratch_shapes=[pltpu.VMEM((tm, d), F32)],
        compiler_params=_cparams("parallel"),
        name="ffn",
    )(*args)


def _s5_kernel(x_ref, g_ref, wbr_ref, wbi_ref, akr_ref, aki_ref, pr_ref, pi_ref, wcr_ref, wci_ref,
               d_ref, wglu_ref, bglu_ref, o_ref, er_ref, ei_ref, cr_ref, ci_ref, *, tm, tiles_per_seq):
    i = pl.program_id(0)

    @pl.when(i % tiles_per_seq == 0)
    def _():
        cr_ref[...] = jnp.zeros_like(cr_ref)
        ci_ref[...] = jnp.zeros_like(ci_ref)

    x = x_ref[...]
    u = _rms(x, g_ref[...])
    ub = u.astype(BF16)
    for s in range(S5_SLABS):
        us = ub[:, s * LANES:(s + 1) * LANES]
        er_ref[:, s * S5_SLAB_CH:(s + 1) * S5_SLAB_CH] = jnp.dot(us, wbr_ref[s], preferred_element_type=F32)
        ei_ref[:, s * S5_SLAB_CH:(s + 1) * S5_SLAB_CH] = jnp.dot(us, wbi_ref[s], preferred_element_type=F32)

    def group(j, carry):
        r0 = pl.multiple_of(j * SUBLANES, SUBLANES)
        xr = er_ref[pl.ds(r0, SUBLANES), :]
        xi = ei_ref[pl.ds(r0, SUBLANES), :]
        for lvl in range(3):
            k = 1 << lvl
            sr = pltpu.roll(xr, k, 0)
            si = pltpu.roll(xi, k, 0)
            ar, ai = akr_ref[lvl], aki_ref[lvl]
            xr, xi = xr + ar * sr - ai * si, xi + ar * si + ai * sr
        cr, ci = cr_ref[...], ci_ref[...]
        pr, pi_ = pr_ref[...], pi_ref[...]
        hr = xr + pr * cr - pi_ * ci
        hi = xi + pr * ci + pi_ * cr
        er_ref[pl.ds(r0, SUBLANES), :] = hr
        ei_ref[pl.ds(r0, SUBLANES), :] = hi
        cr_ref[...] = jnp.broadcast_to(hr[SUBLANES - 1:SUBLANES, :], (SUBLANES, S5_CH))
        ci_ref[...] = jnp.broadcast_to(hi[SUBLANES - 1:SUBLANES, :], (SUBLANES, S5_CH))
        return carry

    lax.fori_loop(0, tm // SUBLANES, group, 0)

    ys = []
    for s in range(S5_SLABS):
        hr = er_ref[:, s * S5_SLAB_CH:(s + 1) * S5_SLAB_CH].astype(BF16)
        hi = ei_ref[:, s * S5_SLAB_CH:(s + 1) * S5_SLAB_CH].astype(BF16)
        ys.append(jnp.dot(hr, wcr_ref[s], preferred_element_type=F32)
                  - jnp.dot(hi, wci_ref[s], preferred_element_type=F32))
    y = jnp.concatenate(ys, axis=1) + d_ref[...] * u
    z = jax.nn.gelu(y).astype(BF16)
    zz = jnp.dot(z, wglu_ref[...], preferred_element_type=F32) + bglu_ref[...]
    o_ref[...] = x + zz[:, :D_MODEL] * jax.nn.sigmoid(zz[:, D_MODEL:])


def _block_diag_slabs(w):
    g, r, c = w.shape
    per = g // S5_SLABS
    w = w.reshape(S5_SLABS, per, r, c)
    eye = jnp.eye(per, dtype=w.dtype)
    return jnp.einsum('sgrc,gh->sgrhc', w, eye).reshape(S5_SLABS, per * r, per * c)


def _s5_tables(a_re, a_im, log_dt):
    lr, li = a_re.astype(F32), a_im.astype(F32)
    dt = jnp.exp(log_dt.astype(F32))[:, None]
    mag = jnp.exp(lr * dt)
    ab_re, ab_im = mag * jnp.cos(li * dt), mag * jnp.sin(li * dt)
    den = lr * lr + li * li
    m_re = ab_re - 1.0
    f_re = (m_re * lr + ab_im * li) / den
    f_im = (ab_im * lr - m_re * li) / den

    def power(k):
        return ((jnp.exp(lr * dt * k) * jnp.cos(li * dt * k)).reshape(-1),
                (jnp.exp(lr * dt * k) * jnp.sin(li * dt * k)).reshape(-1))

    rows = jnp.arange(SUBLANES)[:, None]
    akr, aki = [], []
    for k in (1, 2, 4):
        pr, pi_ = power(float(k))
        akr.append(jnp.where(rows >= k, pr[None, :], 0.0))
        aki.append(jnp.where(rows >= k, pi_[None, :], 0.0))
    pw = [power(float(k + 1)) for k in range(SUBLANES)]
    p_re = jnp.stack([p[0] for p in pw])
    p_im = jnp.stack([p[1] for p in pw])
    return f_re, f_im, jnp.stack(akr), jnp.stack(aki), p_re, p_im


def _s5(h, g, a_re, a_im, log_dt, b_re, b_im, c_re, c_im, d_skip, w_glu, b_glu, seq, tm=256):
    t, d = h.shape
    f_re, f_im, akr, aki, p_re, p_im = _s5_tables(a_re, a_im, log_dt)
    bw_re = f_re[:, :, None] * b_re - f_im[:, :, None] * b_im
    bw_im = f_re[:, :, None] * b_im + f_im[:, :, None] * b_re
    wbr = _block_diag_slabs(jnp.swapaxes(bw_re, 1, 2)).astype(BF16)
    wbi = _block_diag_slabs(jnp.swapaxes(bw_im, 1, 2)).astype(BF16)
    wcr = _block_diag_slabs(jnp.swapaxes(c_re.astype(F32), 1, 2)).astype(BF16)
    wci = _block_diag_slabs(jnp.swapaxes(c_im.astype(F32), 1, 2)).astype(BF16)
    c2 = lambda i: (0, 0)
    c3 = lambda i: (0, 0, 0)
    return pl.pallas_call(
        functools.partial(_s5_kernel, tm=tm, tiles_per_seq=seq // tm),
        grid=(t // tm,),
        in_specs=[pl.BlockSpec((tm, d), lambda i: (i, 0)),
                  pl.BlockSpec((1, d), c2),
                  pl.BlockSpec(wbr.shape, c3), pl.BlockSpec(wbi.shape, c3),
                  pl.BlockSpec(akr.shape, c3), pl.BlockSpec(aki.shape, c3),
                  pl.BlockSpec(p_re.shape, c2), pl.BlockSpec(p_im.shape, c2),
                  pl.BlockSpec(wcr.shape, c3), pl.BlockSpec(wci.shape, c3),
                  pl.BlockSpec((1, d), c2),
                  pl.BlockSpec((d, 2 * d), c2),
                  pl.BlockSpec((1, 2 * d), c2)],
        out_specs=pl.BlockSpec((tm, d), lambda i: (i, 0)),
        out_shape=jax.ShapeDtypeStruct((t, d), F32),
        scratch_shapes=[pltpu.VMEM((tm, S5_CH), F32), pltpu.VMEM((tm, S5_CH), F32),
                        pltpu.VMEM((SUBLANES, S5_CH), F32), pltpu.VMEM((SUBLANES, S5_CH), F32)],
        compiler_params=_cparams("arbitrary"),
        name="s5",
    )(h, g.reshape(1, d), wbr, wbi, akr, aki, p_re, p_im, wcr, wci,
      d_skip.astype(F32).reshape(1, d), w_glu.astype(BF16), b_glu.astype(F32).reshape(1, 2 * d))


DIL_CLASSES = 16
DIL_STEPS = 2


def _dil_kernel(*refs, seq):
    ncls = DIL_CLASSES
    u_len = seq // ncls
    blk = DIL_BLOCK
    q_in, k_in, v_in, o_ref = refs[:4]
    qp_ref, kp_ref, vp_ref, acc_ref, lsw_ref, m_ref, msw_ref = refs[4:11]
    s_refs = refs[11:]
    for r in range(ncls):
        rows = slice(r * u_len, (r + 1) * u_len)
        qp_ref[rows, :] = q_in[:, r, :]
        kp_ref[rows, :] = k_in[:, r, :]
        vp_ref[rows, :] = v_in[:, r, :]

    lane = lax.broadcasted_iota(jnp.int32, (1, LANES), 1)
    low = lane < HEAD_DIM
    n_iter = seq // blk // DIL_STEPS

    for pat, (window, dil) in enumerate(DIL_PATTERNS):
        assert window // dil == blk and ncls % dil == 0 and seq % (dil * blk) == 0
        n_run = ncls // dil
        run = SUBLANES * dil
        n_blk = u_len // run
        jq = lax.broadcasted_iota(jnp.int32, (blk, blk), 0)
        jk = lax.broadcasted_iota(jnp.int32, (blk, blk), 1)
        wq = (jq % run) * n_run + jq // run
        wk = (jk % run) * n_run + jk // run
        mask_cur = wk <= wq
        mask_prev = wk >= wq

        def run_rows(step, dil=dil, n_run=n_run, run=run, n_blk=n_blk):
            res = step // n_blk
            bi = step % n_blk
            base = [(res + dil * c) * u_len for c in range(n_run)]
            cur = [pl.multiple_of(b + run * bi, SUBLANES) for b in base]
            prev = [pl.multiple_of(b + run * jnp.maximum(bi - 1, 0), SUBLANES) for b in base]
            return cur, prev, bi > 0

        def gather(ref, offs, run=run):
            parts = [ref[pl.ds(o, run), :] for o in offs]
            return jnp.concatenate(parts, axis=0) if len(parts) > 1 else parts[0]

        def scores(step, hh):
            cur, prev, _ = run_rows(step)
            q = gather(qp_ref, cur)
            q = jnp.where(low if hh == 0 else jnp.logical_not(low), q, 0.0).astype(BF16)
            kcat = jnp.concatenate([gather(kp_ref, prev), gather(kp_ref, cur)], axis=0).astype(BF16)
            return lax.dot_general(q, kcat, NT_DIMS, preferred_element_type=F32)

        def phase(it, carry, pat=pat):
            nxt = jnp.minimum(it + 1, n_iter - 1)
            for t in range(DIL_STEPS):
                step = it * DIL_STEPS + t
                cur, prev, has_prev = run_rows(step)
                vcat = jnp.concatenate([gather(vp_ref, prev), gather(vp_ref, cur)], axis=0)
                mask = jnp.concatenate([jnp.logical_and(mask_prev, has_prev), mask_cur], axis=1)
                ms, os_ = [], []
                for hh in range(HEADS_PER_TILE):
                    c = t * HEADS_PER_TILE + hh
                    s = jnp.where(mask, s_refs[c][...], NEG)
                    mb = jnp.max(s, axis=1, keepdims=True)
                    p = jnp.exp(s - mb).astype(BF16)
                    vh = jnp.where(low if hh == 0 else jnp.logical_not(low), vcat, 1.0).astype(BF16)
                    os_.append(jnp.dot(p, vh, preferred_element_type=F32))
                    ms.append(jnp.broadcast_to(mb, (blk, LANES)))
                    s_refs[c][...] = scores(nxt * DIL_STEPS + t, hh)
                m_b = jnp.where(low, ms[0], ms[1])
                msw_b = jnp.where(low, ms[1], ms[0])
                o_b = jnp.where(low, os_[0], os_[1])
                lsw_b = jnp.where(low, os_[1], os_[0])
                for ci, off in enumerate(cur):
                    rows = pl.ds(off, run)
                    piece = slice(ci * run, (ci + 1) * run)
                    if pat == 0:
                        m_ref[rows, :] = m_b[piece]
                        msw_ref[rows, :] = msw_b[piece]
                        acc_ref[rows, :] = o_b[piece]
                        lsw_ref[rows, :] = lsw_b[piece]
                    else:
                        m_o, msw_o = m_ref[rows, :], msw_ref[rows, :]
                        m_n = jnp.maximum(m_o, m_b[piece])
                        msw_n = jnp.maximum(msw_o, msw_b[piece])
                        m_ref[rows, :] = m_n
                        msw_ref[rows, :] = msw_n
                        acc_ref[rows, :] = (acc_ref[rows, :] * jnp.exp(m_o - m_n)
                                            + o_b[piece] * jnp.exp(m_b[piece] - m_n))
                        lsw_ref[rows, :] = (lsw_ref[rows, :] * jnp.exp(msw_o - msw_n)
                                            + lsw_b[piece] * jnp.exp(msw_b[piece] - msw_n))
            return carry

        for t in range(DIL_STEPS):
            for hh in range(HEADS_PER_TILE):
                s_refs[t * HEADS_PER_TILE + hh][...] = scores(t, hh)
        lax.fori_loop(0, n_iter, phase, 0)

    def finish(c, carry):
        r0 = pl.multiple_of(c * 256, 256)
        den = pltpu.roll(lsw_ref[pl.ds(r0, 256), :], HEAD_DIM, 1)
        o_ref[pl.ds(r0, 256), :] = (acc_ref[pl.ds(r0, 256), :] / den).astype(o_ref.dtype)
        return carry

    lax.fori_loop(0, seq // 256, finish, 0)


def _dilated(qkv, bsz, seq):
    ncls = DIL_CLASSES
    assert seq % (ncls * DIL_BLOCK) == 0 and (seq // DIL_BLOCK) % DIL_STEPS == 0 and seq % 256 == 0
    u_len = seq // ncls
    n_tiles = D_MODEL // LANES
    qkv4 = qkv.reshape(bsz, u_len, ncls, 3 * D_MODEL)
    spec = lambda off: pl.BlockSpec((None, u_len, ncls, LANES), lambda b, hp, off=off: (b, 0, 0, off + hp))
    in_specs = [spec(off) for off in (0, n_tiles, 2 * n_tiles)]
    out = pl.pallas_call(
        functools.partial(_dil_kernel, seq=seq),
        grid=(bsz, n_tiles),
        in_specs=in_specs,
        out_specs=pl.BlockSpec((None, seq, LANES), lambda b, hp: (b, 0, hp)),
        out_shape=jax.ShapeDtypeStruct((bsz, seq, D_MODEL), BF16),
        scratch_shapes=[pltpu.VMEM((seq, LANES), F32)] * 7
                       + [pltpu.VMEM((DIL_BLOCK, 2 * DIL_BLOCK), F32)] * (DIL_STEPS * HEADS_PER_TILE),
        compiler_params=_cparams("parallel", "parallel"),
        name="dilated_attn",
    )(*([qkv4] * len(in_specs)))
    out = out.reshape(bsz, ncls, u_len, D_MODEL).transpose(0, 2, 1, 3)
    return out.reshape(bsz * seq, D_MODEL)


def _hgrn_chunk(q, f, iv, lb, st_t, tril):
    sub = HGRN_SUB
    qs = _silu(q)
    fg = lb + (1.0 - lb) * jax.nn.sigmoid(f)
    kk = 1.0 - fg
    bc = jnp.dot(tril, jnp.log(fg), preferred_element_type=F32, precision=lax.Precision.HIGHEST)
    bl = bc[HGRN_CHUNK - 1:HGRN_CHUNK]
    qe = (qs * jnp.exp(bc)).astype(BF16)
    o_inter = lax.dot_general(qe, st_t.astype(BF16), NT_DIMS, preferred_element_type=F32)
    ivb = iv.astype(BF16)
    trow = lax.broadcasted_iota(jnp.int32, (sub, 1), 0)
    outs = []
    for blk in range(HGRN_CHUNK // sub):
        lo = blk * sub
        q_i, b_i = qs[lo:lo + sub], bc[lo:lo + sub]
        o_i = o_inter[lo:lo + sub]
        if blk > 0:
            r_i = bc[lo - 1:lo]
            qt = (q_i * jnp.exp(b_i - r_i)).astype(BF16)
            kt = (kk[:lo] * jnp.exp(r_i - bc[:lo])).astype(BF16)
            att = lax.dot_general(qt, kt, NT_DIMS, preferred_element_type=F32)
            o_i = o_i + jnp.dot(att.astype(BF16), ivb[:lo], preferred_element_type=F32)
        k_i, iv_i = kk[lo:lo + sub], iv[lo:lo + sub]
        for s in range(sub):
            w = jnp.exp(jnp.minimum(b_i - b_i[s:s + 1], 0.0))
            a_col = jnp.sum(q_i * k_i[s:s + 1] * w, axis=1, keepdims=True)
            a_col = jnp.where(trow >= s, a_col, 0.0)
            o_i = o_i + a_col * iv_i[s:s + 1]
        outs.append(o_i)
    o = jnp.concatenate(outs, axis=0)
    kd = (kk * jnp.exp(bl - bc)).astype(BF16)
    st_t = st_t * jnp.exp(bl) + lax.dot_general(ivb, kd, TN_DIMS, preferred_element_type=F32)
    return o, st_t


def _hgrn_kernel(q_ref, f_ref, i_ref, g_ref, lb_ref, ng_ref, o_ref, st_ref, *, tc):
    @pl.when(pl.program_id(2) == 0)
    def _():
        st_ref[...] = jnp.zeros_like(st_ref)

    r = lax.broadcasted_iota(jnp.int32, (HGRN_CHUNK, HGRN_CHUNK), 0)
    c = lax.broadcasted_iota(jnp.int32, (HGRN_CHUNK, HGRN_CHUNK), 1)
    tril = (c <= r).astype(F32)
    lb = lb_ref[...]
    st_t = st_ref[...]
    for ch in range(tc // HGRN_CHUNK):
        rows = slice(ch * HGRN_CHUNK, (ch + 1) * HGRN_CHUNK)
        o, st_t = _hgrn_chunk(q_ref[rows, :], f_ref[rows, :], i_ref[rows, :], lb, st_t, tril)
        o = o * lax.rsqrt(jnp.mean(o * o, axis=-1, keepdims=True) + RMS_EPS) * ng_ref[...]
        o_ref[rows, :] = (o * _silu(g_ref[rows, :])).astype(o_ref.dtype)
    st_ref[...] = st_t


def _hgrn(proj, lb, norm_g, bsz, seq, tc=256):
    t = bsz * seq
    tiles = seq // tc
    hd = HGRN_HEAD_DIM
    spec = lambda off: pl.BlockSpec((tc, hd), lambda b, h, j, off=off: (b * tiles + j, off + h))
    return pl.pallas_call(
        functools.partial(_hgrn_kernel, tc=tc),
        grid=(bsz, HGRN_HEADS, tiles),
        in_specs=[spec(0), spec(HGRN_HEADS), spec(2 * HGRN_HEADS), spec(3 * HGRN_HEADS),
                  pl.BlockSpec((1, hd), lambda b, h, j: (0, h)),
                  pl.BlockSpec((1, hd), lambda b, h, j: (0, 0))],
        out_specs=spec(0),
        out_shape=jax.ShapeDtypeStruct((t, D_MODEL), BF16),
        scratch_shapes=[pltpu.VMEM((hd, hd), F32)],
        compiler_params=_cparams("parallel", "parallel", "arbitrary"),
        name="hgrn2",
    )(proj, proj, proj, proj, lb.reshape(1, D_MODEL), norm_g.astype(F32).reshape(1, hd))


MOBA_SUPER = 2


def _moba_kernel(q_ref, k_ref, v_ref, o_ref, km_ref, kp_ref, vt_ref, qp_ref, acc_ref, *s_refs, n_blk, n_sel):
    blk = MOBA_BLOCK
    sbk = MOBA_SUPER * blk
    lane = lax.broadcasted_iota(jnp.int32, (1, LANES), 1)
    head_of_lane = lane // HEAD_DIM
    hot_lane0 = [(1 - hh) * HEAD_DIM for hh in range(HEADS_PER_TILE)]
    n_rows = HEADS_PER_TILE * n_blk

    for n in range(n_blk):
        rows = slice(n * blk, (n + 1) * blk)
        kb = k_ref[rows, :]
        vt_ref[:, rows] = v_ref[rows, :].T.astype(BF16)
        km = jnp.sum(kb, axis=0, keepdims=True) * (1.0 / blk)
        for hh in range(HEADS_PER_TILE):
            mine = head_of_lane == hh
            km_ref[hh * n_blk + n:hh * n_blk + n + 1, :] = jnp.where(mine, km, 0.0)
            hot = (lane == hot_lane0[hh] + n).astype(F32)
            kp_ref[rows, hh * LANES:(hh + 1) * LANES] = jnp.where(mine, kb, hot).astype(BF16)

    prow = lax.broadcasted_iota(jnp.int32, (n_rows, HEADS_PER_TILE * LANES), 0)
    pcol = lax.broadcasted_iota(jnp.int32, (n_rows, HEADS_PER_TILE * LANES), 1)
    target = jnp.zeros_like(prow)
    for hh in range(HEADS_PER_TILE):
        target = jnp.where(prow // n_blk == hh, hh * LANES + hot_lane0[hh] + prow % n_blk, target)
    place = (pcol == target).astype(BF16)
    nid = lax.broadcasted_iota(jnp.int32, (n_blk, blk), 0)
    nid_f = nid.astype(F32)
    km_all = km_ref[...]

    for qi in range(n_blk):
        rows = slice(qi * blk, (qi + 1) * blk)
        q = q_ref[rows, :]
        gate_t = lax.dot_general(km_all, q, NT_DIMS, preferred_element_type=F32,
                                 precision=lax.Precision.HIGHEST)
        past = nid < qi
        bias_rows = []
        for hh in range(HEADS_PER_TILE):
            g = jnp.where(past, gate_t[hh * n_blk:(hh + 1) * n_blk], NEG)
            keep = nid == qi
            for _ in range(n_sel):
                mx = jnp.max(g, axis=0, keepdims=True)
                idx = jnp.min(jnp.where(g == mx, nid_f, float(n_blk)), axis=0, keepdims=True)
                pick = nid_f == idx
                keep = jnp.logical_or(keep, jnp.logical_and(pick, past))
                g = jnp.where(pick, BELOW_NEG, g)
            bias_rows.append(jnp.where(keep, 0.0, NEG))
        bias_t = jnp.concatenate(bias_rows, axis=0).astype(BF16)
        bias_q = lax.dot_general(bias_t, place, TN_DIMS, preferred_element_type=F32)
        for hh in range(HEADS_PER_TILE):
            qp_ref[rows, hh * LANES:(hh + 1) * LANES] = jnp.where(
                head_of_lane == hh, q, bias_q[:, hh * LANES:(hh + 1) * LANES]).astype(BF16)

    krow = lax.broadcasted_iota(jnp.int32, (sbk, blk), 0)
    qcol = lax.broadcasted_iota(jnp.int32, (sbk, blk), 1)
    rel = qcol - krow

    n_chain = MOBA_SUPER * HEADS_PER_TILE

    def q_group(j, carry):
        q0 = pl.multiple_of(j * sbk, sbk)
        qps = [[qp_ref[pl.ds(q0 + t * blk, blk), hh * LANES:(hh + 1) * LANES] for hh in range(HEADS_PER_TILE)]
               for t in range(MOBA_SUPER)]
        acc_ref[...] = jnp.zeros_like(acc_ref)

        def scores_t(k0, t, hh):
            kpn = kp_ref[pl.ds(k0, sbk), hh * LANES:(hh + 1) * LANES]
            return lax.dot_general(kpn, qps[t][hh], NT_DIMS, preferred_element_type=F32)

        def key_step(g, stats, last):
            k0 = pl.multiple_of(g * sbk, sbk)
            v_t = [vt_ref[hh * HEAD_DIM:(hh + 1) * HEAD_DIM, pl.ds(k0, sbk)] for hh in range(HEADS_PER_TILE)]
            new = []
            for t in range(MOBA_SUPER):
                for hh in range(HEADS_PER_TILE):
                    c = t * HEADS_PER_TILE + hh
                    m_o, l_o = stats[2 * c:2 * c + 2]
                    s_t = s_refs[c][...]
                    if last:
                        s_t = jnp.where(rel >= -t * blk, s_t, NEG)
                    m_n = jnp.maximum(m_o, jnp.max(s_t, axis=0, keepdims=True))
                    alpha = jnp.exp(m_o - m_n)
                    p = jnp.exp(s_t - m_n)
                    acc_ref[c] = alpha * acc_ref[c] + jnp.dot(v_t[hh], p.astype(BF16), preferred_element_type=F32)
                    new += [m_n, alpha * l_o + jnp.sum(p, axis=0, keepdims=True)]
                    if not last:
                        s_refs[c][...] = scores_t(k0 + sbk, t, hh)
            return tuple(new)

        for t in range(MOBA_SUPER):
            for hh in range(HEADS_PER_TILE):
                s_refs[t * HEADS_PER_TILE + hh][...] = scores_t(0, t, hh)
        init = (jnp.full((1, blk), BELOW_NEG, F32), jnp.zeros((1, blk), F32)) * n_chain
        stats = lax.fori_loop(0, j, functools.partial(key_step, last=False), init)
        stats = key_step(j, stats, last=True)
        for t in range(MOBA_SUPER):
            o_t = jnp.concatenate([acc_ref[t * HEADS_PER_TILE + hh] / stats[2 * (t * HEADS_PER_TILE + hh) + 1]
                                   for hh in range(HEADS_PER_TILE)], axis=0)
            o_ref[pl.ds(q0 + t * blk, blk), :] = o_t.T.astype(o_ref.dtype)
        return carry

    lax.fori_loop(0, n_blk // MOBA_SUPER, q_group, 0)


def _moba(qkv, bsz, seq):
    blk = MOBA_BLOCK
    assert seq % (blk * MOBA_SUPER) == 0
    n_blk = seq // blk
    n_sel = min(MOBA_TOPK, n_blk - 1)
    n_tiles = D_MODEL // LANES
    qkv3 = qkv.reshape(bsz, seq, 3 * D_MODEL)
    spec = lambda off: pl.BlockSpec((None, seq, LANES), lambda b, hp, off=off: (b, 0, off + hp))
    out = pl.pallas_call(
        functools.partial(_moba_kernel, n_blk=n_blk, n_sel=n_sel),
        grid=(bsz, n_tiles),
        in_specs=[spec(0), spec(n_tiles), spec(2 * n_tiles)],
        out_specs=pl.BlockSpec((None, seq, LANES), lambda b, hp: (b, 0, hp)),
        out_shape=jax.ShapeDtypeStruct((bsz, seq, D_MODEL), BF16),
        scratch_shapes=[pltpu.VMEM((HEADS_PER_TILE * n_blk, LANES), F32),
                        pltpu.VMEM((seq, HEADS_PER_TILE * LANES), BF16),
                        pltpu.VMEM((LANES, seq), BF16),
                        pltpu.VMEM((seq, HEADS_PER_TILE * LANES), BF16),
                        pltpu.VMEM((MOBA_SUPER * HEADS_PER_TILE, HEAD_DIM, blk), F32)]
                       + [pltpu.VMEM((MOBA_SUPER * blk, blk), F32)] * (MOBA_SUPER * HEADS_PER_TILE),
        compiler_params=_cparams("parallel", "parallel"),
        name="moba_attn",
    )(qkv3, qkv3, qkv3)
    return out.reshape(bsz * seq, D_MODEL)


def _rope_tables(positions):
    half = ROPE_DIM // 2
    inv = ROPE_THETA ** (-jnp.arange(0, ROPE_DIM, 2, dtype=F32) / ROPE_DIM)
    ang = positions.astype(F32).reshape(-1)[:, None] * inv
    cos, sin = jnp.cos(ang), jnp.sin(ang)
    t = cos.shape[0]
    rest = HEAD_DIM - ROPE_DIM
    c64 = jnp.concatenate([cos, cos, jnp.ones((t, rest), F32)], axis=1)
    s1 = jnp.concatenate([-sin, jnp.zeros((t, HEAD_DIM - half), F32)], axis=1)
    s2 = jnp.concatenate([jnp.zeros((t, half), F32), sin, jnp.zeros((t, rest), F32)], axis=1)
    tile = lambda a: jnp.tile(a, (1, HEADS_PER_TILE))
    return tile(c64), tile(s1), tile(s2)


def _qkv_weight(w):
    scale = jnp.concatenate([jnp.full((D_MODEL,), HEAD_DIM ** -0.5, F32), jnp.ones((2 * D_MODEL,), F32)])
    return (w.astype(F32) * scale[None, :]).astype(BF16)


def kernel(x, positions, norm_mix, norm_ffn, norm_final, s5_a_re, s5_a_im, s5_log_dt, s5_b_re, s5_b_im, s5_c_re, s5_c_im, s5_d, s5_w_glu, s5_b_glu, dil_w_qkv, dil_w_o, hgrn_w_in, hgrn_lower_bound, hgrn_norm, hgrn_w_o, moba_w_qkv, moba_w_o, ffn_w_gate_up, ffn_w_down):
    bsz, seq, d = x.shape
    depth = norm_mix.shape[0]
    n_mixers = 4
    t = bsz * seq
    rope = _rope_tables(positions)
    lb_w = jax.nn.softmax(hgrn_lower_bound.astype(F32), axis=0)
    lower_bounds = jnp.cumsum(lb_w, axis=0) - lb_w[0]
    norm_mix = norm_mix.astype(F32)
    norm_ffn = norm_ffn.astype(F32)
    wgu = ffn_w_gate_up.astype(BF16)
    wdn = ffn_w_down.astype(BF16)

    h = x.reshape(t, d).astype(F32)
    for layer in range(depth):
        mixer, j = layer % n_mixers, layer // n_mixers
        final_g = norm_final.astype(F32) if layer == depth - 1 else None
        mix = None
        if mixer == 0:
            h = _s5(h, norm_mix[layer], s5_a_re[j], s5_a_im[j], s5_log_dt[j], s5_b_re[j], s5_b_im[j],
                    s5_c_re[j], s5_c_im[j], s5_d[j], s5_w_glu[j], s5_b_glu[j], seq)
        elif mixer == 1:
            qkv = _proj(h, norm_mix[layer], _qkv_weight(dil_w_qkv[j]), rope, 2 * D_MODEL)
            mix = (_dilated(qkv, bsz, seq), dil_w_o[j].astype(BF16))
        elif mixer == 2:
            proj = _proj(h, norm_mix[layer], hgrn_w_in[j].astype(BF16))
            mix = (_hgrn(proj, lower_bounds[layer], hgrn_norm[j], bsz, seq), hgrn_w_o[j].astype(BF16))
        else:
            qkv = _proj(h, norm_mix[layer], _qkv_weight(moba_w_qkv[j]), rope, 2 * D_MODEL)
            mix = (_moba(qkv, bsz, seq), moba_w_o[j].astype(BF16))
        h = _ffn(h, norm_ffn[layer], wgu[layer], wdn[layer], mix=mix, final_g=final_g)
    return h.reshape(bsz, seq, d).astype(x.dtype)
```

```python
import functools
import math

import jax
import jax.numpy as jnp
from jax import lax
from jax.experimental import pallas as pl
from jax.experimental.pallas import tpu as pltpu

F32 = jnp.float32
BF16 = jnp.bfloat16

D_MODEL = 1024
D_FF = 2816
RMS_EPS = 1e-6
NEG = -1e30
BELOW_NEG = -3e38

HEAD_DIM = 64
ATT_HEADS = D_MODEL // HEAD_DIM
ROPE_DIM = HEAD_DIM // 4
ROPE_THETA = 500000.0
LANES = 128
SUBLANES = 8
HEADS_PER_TILE = LANES // HEAD_DIM

S5_GROUP = 16
S5_GROUPS = D_MODEL // S5_GROUP
S5_STATE = 64
S5_CH = S5_GROUPS * S5_STATE
S5_SLABS = D_MODEL // LANES
S5_SLAB_CH = S5_CH // S5_SLABS

DIL_PATTERNS = ((128, 1), (512, 4), (2048, 16))
DIL_BLOCK = 128

HGRN_HEAD_DIM = 128
HGRN_HEADS = D_MODEL // HGRN_HEAD_DIM
HGRN_CHUNK = 64
HGRN_SUB = 16

MOBA_BLOCK = 256
MOBA_TOPK = 3

VMEM_LIMIT = 56 * 1024 * 1024

NT_DIMS = (((1,), (1,)), ((), ()))
TN_DIMS = (((0,), (0,)), ((), ()))


def _cparams(*sem):
    return pltpu.CompilerParams(dimension_semantics=sem, vmem_limit_bytes=VMEM_LIMIT)


def _rms(x, g):
    return x * lax.rsqrt(jnp.mean(x * x, axis=-1, keepdims=True) + RMS_EPS) * g


def _silu(x):
    return x * jax.nn.sigmoid(x)


def _proj_kernel(*refs, n_out, tn, rope_cols):
    if rope_cols:
        x_ref, g_ref, w_ref, c_ref, s1_ref, s2_ref, o_ref = refs
    else:
        x_ref, g_ref, w_ref, o_ref = refs
    x = x_ref[...]
    xn = _rms(x, g_ref[...]).astype(BF16)
    for c in range(n_out // tn):
        y = jnp.dot(xn, w_ref[:, c * tn:(c + 1) * tn], preferred_element_type=F32)
        if c * tn < rope_cols:
            cos, s1, s2 = c_ref[...], s1_ref[...], s2_ref[...]
            parts = []
            for j in range(tn // LANES):
                yj = y[:, j * LANES:(j + 1) * LANES]
                half = ROPE_DIM // 2
                parts.append(yj * cos + pltpu.roll(yj, LANES - half, 1) * s1 + pltpu.roll(yj, half, 1) * s2)
            y = jnp.concatenate(parts, axis=1) if len(parts) > 1 else parts[0]
        o_ref[:, c * tn:(c + 1) * tn] = y


def _proj(h, g, w, rope=None, rope_cols=0, tm=256, tn=256):
    t, d = h.shape
    n_out = w.shape[1]
    in_specs = [pl.BlockSpec((tm, d), lambda i: (i, 0)),
                pl.BlockSpec((1, d), lambda i: (0, 0)),
                pl.BlockSpec((d, n_out), lambda i: (0, 0))]
    args = [h, g.reshape(1, d), w]
    if rope_cols:
        in_specs += [pl.BlockSpec((tm, LANES), lambda i: (i, 0))] * 3
        args += list(rope)
    return pl.pallas_call(
        functools.partial(_proj_kernel, n_out=n_out, tn=tn, rope_cols=rope_cols),
        grid=(t // tm,),
        in_specs=in_specs,
        out_specs=pl.BlockSpec((tm, n_out), lambda i: (i, 0)),
        out_shape=jax.ShapeDtypeStruct((t, n_out), F32),
        compiler_params=_cparams("parallel"),
        name="proj_rope" if rope_cols else "proj",
    )(*args)


def _ffn_kernel(*refs, has_mix, has_final, fc):
    refs = list(refs)
    h_ref, g_ref, wgu_ref, wd_ref = refs[:4]
    pos = 4
    if has_mix:
        a_ref, wo_ref = refs[pos:pos + 2]
        pos += 2
    if has_final:
        fg_ref = refs[pos]
        pos += 1
    o_ref, acc_ref = refs[pos], refs[pos + 1]

    h = h_ref[...]
    if has_mix:
        h = h + jnp.dot(a_ref[...], wo_ref[...], preferred_element_type=F32)
    xn = _rms(h, g_ref[...]).astype(BF16)

    def gate_up(c):
        gate = jnp.dot(xn, wgu_ref[:, c * fc:(c + 1) * fc], preferred_element_type=F32)
        up = jnp.dot(xn, wgu_ref[:, D_FF + c * fc:D_FF + (c + 1) * fc], preferred_element_type=F32)
        return gate, up

    n_chunks = D_FF // fc
    nxt = gate_up(0)
    for c in range(n_chunks):
        gate, up = nxt
        if c + 1 < n_chunks:
            nxt = gate_up(c + 1)
        act = (_silu(gate) * up).astype(BF16)
        contrib = jnp.dot(act, wd_ref[c * fc:(c + 1) * fc, :], preferred_element_type=F32)
        if c == 0:
            acc_ref[...] = contrib
        else:
            acc_ref[...] += contrib
    out = h + acc_ref[...]
    if has_final:
        out = _rms(out, fg_ref[...])
    o_ref[...] = out


def _ffn(h, g, wgu, wd, mix=None, final_g=None, tm=256, fc=256):
    t, d = h.shape
    const = lambda i: (0, 0)
    in_specs = [pl.BlockSpec((tm, d), lambda i: (i, 0)),
                pl.BlockSpec((1, d), const),
                pl.BlockSpec((d, 2 * D_FF), const),
                pl.BlockSpec((D_FF, d), const)]
    args = [h, g.reshape(1, d), wgu, wd]
    if mix is not None:
        a, wo = mix
        in_specs += [pl.BlockSpec((tm, d), lambda i: (i, 0)), pl.BlockSpec((d, d), const)]
        args += [a, wo]
    if final_g is not None:
        in_specs += [pl.BlockSpec((1, d), const)]
        args += [final_g.reshape(1, d)]
    return pl.pallas_call(
        functools.partial(_ffn_kernel, has_mix=mix is not None, has_final=final_g is not None, fc=fc),
        grid=(t // tm,),
        in_specs=in_specs,
        out_specs=pl.BlockSpec((tm, d), lambda i: (i, 0)),
        out_shape=jax.ShapeDtypeStruct((t, d), F32),
        scratch_shapes=[pltpu.VMEM((tm, d), F32)],
        compiler_params=_cparams("parallel"),
        name="ffn",
    )(*args)


def _s5_kernel(x_ref, g_ref, wbr_ref, wbi_ref, akr_ref, aki_ref, pr_ref, pi_ref, wcr_ref, wci_ref,
               d_ref, wglu_ref, bglu_ref, o_ref, er_ref, ei_ref, cr_ref, ci_ref, *, tm, tiles_per_seq):
    i = pl.program_id(0)

    @pl.when(i % tiles_per_seq == 0)
    def _():
        cr_ref[...] = jnp.zeros_like(cr_ref)
        ci_ref[...] = jnp.zeros_like(ci_ref)

    x = x_ref[...]
    u = _rms(x, g_ref[...])
    ub = u.astype(BF16)
    for s in range(S5_SLABS):
        us = ub[:, s * LANES:(s + 1) * LANES]
        er_ref[:, s * S5_SLAB_CH:(s + 1) * S5_SLAB_CH] = jnp.dot(us, wbr_ref[s], preferred_element_type=F32)
        ei_ref[:, s * S5_SLAB_CH:(s + 1) * S5_SLAB_CH] = jnp.dot(us, wbi_ref[s], preferred_element_type=F32)

    def group(j, carry):
        r0 = pl.multiple_of(j * SUBLANES, SUBLANES)
        xr = er_ref[pl.ds(r0, SUBLANES), :]
        xi = ei_ref[pl.ds(r0, SUBLANES), :]
        for lvl in range(3):
            k = 1 << lvl
            sr = pltpu.roll(xr, k, 0)
            si = pltpu.roll(xi, k, 0)
            ar, ai = akr_ref[lvl], aki_ref[lvl]
            xr, xi = xr + ar * sr - ai * si, xi + ar * si + ai * sr
        cr, ci = cr_ref[...], ci_ref[...]
        pr, pi_ = pr_ref[...], pi_ref[...]
        hr = xr + pr * cr - pi_ * ci
        hi = xi + pr * ci + pi_ * cr
        er_ref[pl.ds(r0, SUBLANES), :] = hr
        ei_ref[pl.ds(r0, SUBLANES), :] = hi
        cr_ref[...] = jnp.broadcast_to(hr[SUBLANES - 1:SUBLANES, :], (SUBLANES, S5_CH))
        ci_ref[...] = jnp.broadcast_to(hi[SUBLANES - 1:SUBLANES, :], (SUBLANES, S5_CH))
        return carry

    lax.fori_loop(0, tm // SUBLANES, group, 0)

    ys = []
    for s in range(S5_SLABS):
        hr = er_ref[:, s * S5_SLAB_CH:(s + 1) * S5_SLAB_CH].astype(BF16)
        hi = ei_ref[:, s * S5_SLAB_CH:(s + 1) * S5_SLAB_CH].astype(BF16)
        ys.append(jnp.dot(hr, wcr_ref[s], preferred_element_type=F32)
                  - jnp.dot(hi, wci_ref[s], preferred_element_type=F32))
    y = jnp.concatenate(ys, axis=1) + d_ref[...] * u
    z = jax.nn.gelu(y).astype(BF16)
    zz = jnp.dot(z, wglu_ref[...], preferred_element_type=F32) + bglu_ref[...]
    o_ref[...] = x + zz[:, :D_MODEL] * jax.nn.sigmoid(zz[:, D_MODEL:])


def _block_diag_slabs(w):
    g, r, c = w.shape
    per = g // S5_SLABS
    w = w.reshape(S5_SLABS, per, r, c)
    eye = jnp.eye(per, dtype=w.dtype)
    return jnp.einsum('sgrc,gh->sgrhc', w, eye).reshape(S5_SLABS, per * r, per * c)


def _s5_tables(a_re, a_im, log_dt):
    lr, li = a_re.astype(F32), a_im.astype(F32)
    dt = jnp.exp(log_dt.astype(F32))[:, None]
    mag = jnp.exp(lr * dt)
    ab_re, ab_im = mag * jnp.cos(li * dt), mag * jnp.sin(li * dt)
    den = lr * lr + li * li
    m_re = ab_re - 1.0
    f_re = (m_re * lr + ab_im * li) / den
    f_im = (ab_im * lr - m_re * li) / den

    def power(k):
        return ((jnp.exp(lr * dt * k) * jnp.cos(li * dt * k)).reshape(-1),
                (jnp.exp(lr * dt * k) * jnp.sin(li * dt * k)).reshape(-1))

    rows = jnp.arange(SUBLANES)[:, None]
    akr, aki = [], []
    for k in (1, 2, 4):
        pr, pi_ = power(float(k))
        akr.append(jnp.where(rows >= k, pr[None, :], 0.0))
        aki.append(jnp.where(rows >= k, pi_[None, :], 0.0))
    pw = [power(float(k + 1)) for k in range(SUBLANES)]
    p_re = jnp.stack([p[0] for p in pw])
    p_im = jnp.stack([p[1] for p in pw])
    return f_re, f_im, jnp.stack(akr), jnp.stack(aki), p_re, p_im


def _s5(h, g, a_re, a_im, log_dt, b_re, b_im, c_re, c_im, d_skip, w_glu, b_glu, seq, tm=256):
    t, d = h.shape
    f_re, f_im, akr, aki, p_re, p_im = _s5_tables(a_re, a_im, log_dt)
    bw_re = f_re[:, :, None] * b_re - f_im[:, :, None] * b_im
    bw_im = f_re[:, :, None] * b_im + f_im[:, :, None] * b_re
    wbr = _block_diag_slabs(jnp.swapaxes(bw_re, 1, 2)).astype(BF16)
    wbi = _block_diag_slabs(jnp.swapaxes(bw_im, 1, 2)).astype(BF16)
    wcr = _block_diag_slabs(jnp.swapaxes(c_re.astype(F32), 1, 2)).astype(BF16)
    wci = _block_diag_slabs(jnp.swapaxes(c_im.astype(F32), 1, 2)).astype(BF16)
    c2 = lambda i: (0, 0)
    c3 = lambda i: (0, 0, 0)
    return pl.pallas_call(
        functools.partial(_s5_kernel, tm=tm, tiles_per_seq=seq // tm),
        grid=(t // tm,),
        in_specs=[pl.BlockSpec((tm, d), lambda i: (i, 0)),
                  pl.BlockSpec((1, d), c2),
                  pl.BlockSpec(wbr.shape, c3), pl.BlockSpec(wbi.shape, c3),
                  pl.BlockSpec(akr.shape, c3), pl.BlockSpec(aki.shape, c3),
                  pl.BlockSpec(p_re.shape, c2), pl.BlockSpec(p_im.shape, c2),
                  pl.BlockSpec(wcr.shape, c3), pl.BlockSpec(wci.shape, c3),
                  pl.BlockSpec((1, d), c2),
                  pl.BlockSpec((d, 2 * d), c2),
                  pl.BlockSpec((1, 2 * d), c2)],
        out_specs=pl.BlockSpec((tm, d), lambda i: (i, 0)),
        out_shape=jax.ShapeDtypeStruct((t, d), F32),
        scratch_shapes=[pltpu.VMEM((tm, S5_CH), F32), pltpu.VMEM((tm, S5_CH), F32),
                        pltpu.VMEM((SUBLANES, S5_CH), F32), pltpu.VMEM((SUBLANES, S5_CH), F32)],
        compiler_params=_cparams("arbitrary"),
        name="s5",
    )(h, g.reshape(1, d), wbr, wbi, akr, aki, p_re, p_im, wcr, wci,
      d_skip.astype(F32).reshape(1, d), w_glu.astype(BF16), b_glu.astype(F32).reshape(1, 2 * d))


DIL_CLASSES = 16
DIL_STEPS = 2


def _dil_kernel(*refs, seq):
    ncls = DIL_CLASSES
    u_len = seq // ncls
    blk = DIL_BLOCK
    q_in, k_in, v_in, o_ref = refs[:4]
    qp_ref, kp_ref, vp_ref, acc_ref, lsw_ref, m_ref, msw_ref = refs[4:11]
    s_refs = refs[11:]
    for r in range(ncls):
        rows = slice(r * u_len, (r + 1) * u_len)
        qp_ref[rows, :] = q_in[:, r, :]
        kp_ref[rows, :] = k_in[:, r, :]
        vp_ref[rows, :] = v_in[:, r, :]

    lane = lax.broadcasted_iota(jnp.int32, (1, LANES), 1)
    low = lane < HEAD_DIM
    n_iter = seq // blk // DIL_STEPS

    for pat, (window, dil) in enumerate(DIL_PATTERNS):
        assert window // dil == blk and ncls % dil == 0 and seq % (dil * blk) == 0
        n_run = ncls // dil
        run = SUBLANES * dil
        n_blk = u_len // run
        jq = lax.broadcasted_iota(jnp.int32, (blk, blk), 0)
        jk = lax.broadcasted_iota(jnp.int32, (blk, blk), 1)
        wq = (jq % run) * n_run + jq // run
        wk = (jk % run) * n_run + jk // run
        mask_cur = wk <= wq
        mask_prev = wk >= wq

        def run_rows(step, dil=dil, n_run=n_run, run=run, n_blk=n_blk):
            res = step // n_blk
            bi = step % n_blk
            base = [(res + dil * c) * u_len for c in range(n_run)]
            cur = [pl.multiple_of(b + run * bi, SUBLANES) for b in base]
            prev = [pl.multiple_of(b + run * jnp.maximum(bi - 1, 0), SUBLANES) for b in base]
            return cur, prev, bi > 0

        def gather(ref, offs, run=run):
            parts = [ref[pl.ds(o, run), :] for o in offs]
            return jnp.concatenate(parts, axis=0) if len(parts) > 1 else parts[0]

        def scores(step, hh):
            cur, prev, _ = run_rows(step)
            q = gather(qp_ref, cur)
            q = jnp.where(low if hh == 0 else jnp.logical_not(low), q, 0.0).astype(BF16)
            kcat = jnp.concatenate([gather(kp_ref, prev), gather(kp_ref, cur)], axis=0).astype(BF16)
            return lax.dot_general(q, kcat, NT_DIMS, preferred_element_type=F32)

        def phase(it, carry, pat=pat):
            nxt = jnp.minimum(it + 1, n_iter - 1)
            for t in range(DIL_STEPS):
                step = it * DIL_STEPS + t
                cur, prev, has_prev = run_rows(step)
                vcat = jnp.concatenate([gather(vp_ref, prev), gather(vp_ref, cur)], axis=0)
                mask = jnp.concatenate([jnp.logical_and(mask_prev, has_prev), mask_cur], axis=1)
                ms, os_ = [], []
                for hh in range(HEADS_PER_TILE):
                    c = t * HEADS_PER_TILE + hh
                    s = jnp.where(mask, s_refs[c][...], NEG)
                    mb = jnp.max(s, axis=1, keepdims=True)
                    p = jnp.exp(s - mb).astype(BF16)
                    vh = jnp.where(low if hh == 0 else jnp.logical_not(low), vcat, 1.0).astype(BF16)
                    os_.append(jnp.dot(p, vh, preferred_element_type=F32))
                    ms.append(jnp.broadcast_to(mb, (blk, LANES)))
                    s_refs[c][...] = scores(nxt * DIL_STEPS + t, hh)
                m_b = jnp.where(low, ms[0], ms[1])
                msw_b = jnp.where(low, ms[1], ms[0])
                o_b = jnp.where(low, os_[0], os_[1])
                lsw_b = jnp.where(low, os_[1], os_[0])
                for ci, off in enumerate(cur):
                    rows = pl.ds(off, run)
                    piece = slice(ci * run, (ci + 1) * run)
                    if pat == 0:
                        m_ref[rows, :] = m_b[piece]
                        msw_ref[rows, :] = msw_b[piece]
                        acc_ref[rows, :] = o_b[piece]
                        lsw_ref[rows, :] = lsw_b[piece]
                    else:
                        m_o, msw_o = m_ref[rows, :], msw_ref[rows, :]
                        m_n = jnp.maximum(m_o, m_b[piece])
                        msw_n = jnp.maximum(msw_o, msw_b[piece])
                        m_ref[rows, :] = m_n
                        msw_ref[rows, :] = msw_n
                        acc_ref[rows, :] = (acc_ref[rows, :] * jnp.exp(m_o - m_n)
                                            + o_b[piece] * jnp.exp(m_b[piece] - m_n))
                        lsw_ref[rows, :] = (lsw_ref[rows, :] * jnp.exp(msw_o - msw_n)
                                            + lsw_b[piece] * jnp.exp(msw_b[piece] - msw_n))
            return carry

        for t in range(DIL_STEPS):
            for hh in range(HEADS_PER_TILE):
                s_refs[t * HEADS_PER_TILE + hh][...] = scores(t, hh)
        lax.fori_loop(0, n_iter, phase, 0)

    def finish(c, carry):
        r0 = pl.multiple_of(c * 256, 256)
        den = pltpu.roll(lsw_ref[pl.ds(r0, 256), :], HEAD_DIM, 1)
        o_ref[pl.ds(r0, 256), :] = (acc_ref[pl.ds(r0, 256), :] / den).astype(o_ref.dtype)
        return carry

    lax.fori_loop(0, seq // 256, finish, 0)


def _dilated(qkv, bsz, seq):
    ncls = DIL_CLASSES
    assert seq % (ncls * DIL_BLOCK) == 0 and (seq // DIL_BLOCK) % DIL_STEPS == 0 and seq % 256 == 0
    u_len = seq // ncls
    n_tiles = D_MODEL // LANES
    qkv4 = qkv.reshape(bsz, u_len, ncls, 3 * D_MODEL)
    spec = lambda off: pl.BlockSpec((None, u_len, ncls, LANES), lambda b, hp, off=off: (b, 0, 0, off + hp))
    in_specs = [spec(off) for off in (0, n_tiles, 2 * n_tiles)]
    out = pl.pallas_call(
        functools.partial(_dil_kernel, seq=seq),
        grid=(bsz, n_tiles),
        in_specs=in_specs,
        out_specs=pl.BlockSpec((None, seq, LANES), lambda b, hp: (b, 0, hp)),
        out_shape=jax.ShapeDtypeStruct((bsz, seq, D_MODEL), BF16),
        scratch_shapes=[pltpu.VMEM((seq, LANES), F32)] * 7
                       + [pltpu.VMEM((DIL_BLOCK, 2 * DIL_BLOCK), F32)] * (DIL_STEPS * HEADS_PER_TILE),
        compiler_params=_cparams("parallel", "parallel"),
        name="dilated_attn",
    )(*([qkv4] * len(in_specs)))
    out = out.reshape(bsz, ncls, u_len, D_MODEL).transpose(0, 2, 1, 3)
    return out.reshape(bsz * seq, D_MODEL)


def _hgrn_kernel(q_ref, f_ref, i_ref, g_ref, lb_ref, ng_ref, o_ref, st_ref, *, tc):
    @pl.when(pl.program_id(2) == 0)
    def _():
        st_ref[...] = jnp.zeros_like(st_ref)

    chunk, sub = HGRN_CHUNK, HGRN_SUB
    n_ch = tc // chunk
    n_sub = chunk // sub
    r = lax.broadcasted_iota(jnp.int32, (chunk, chunk), 0)
    c = lax.broadcasted_iota(jnp.int32, (chunk, chunk), 1)
    tril = (c <= r).astype(F32)
    trow = lax.broadcasted_iota(jnp.int32, (sub, 1), 0)
    lb = lb_ref[...]
    rows = [slice(ch * chunk, (ch + 1) * chunk) for ch in range(n_ch)]

    qs = [_silu(q_ref[rw, :]) for rw in rows]
    fg = [lb + (1.0 - lb) * jax.nn.sigmoid(f_ref[rw, :]) for rw in rows]
    kk = [1.0 - x for x in fg]
    iv = [i_ref[rw, :] for rw in rows]
    ivb = [x.astype(BF16) for x in iv]
    bc = [jnp.dot(tril, jnp.log(x), preferred_element_type=F32, precision=lax.Precision.HIGHEST) for x in fg]
    bl = [x[chunk - 1:chunk] for x in bc]

    upd = [lax.dot_general(ivb[ch], (kk[ch] * jnp.exp(bl[ch] - bc[ch])).astype(BF16), TN_DIMS,
                           preferred_element_type=F32) for ch in range(n_ch)]
    att = []
    for ch in range(n_ch):
        row_att = []
        for blk in range(1, n_sub):
            lo = blk * sub
            r_i = bc[ch][lo - 1:lo]
            qt = (qs[ch][lo:lo + sub] * jnp.exp(bc[ch][lo:lo + sub] - r_i)).astype(BF16)
            kt = (kk[ch][:lo] * jnp.exp(r_i - bc[ch][:lo])).astype(BF16)
            row_att.append(lax.dot_general(qt, kt, NT_DIMS, preferred_element_type=F32).astype(BF16))
        att.append(row_att)

    st_t = st_ref[...]
    states = []
    for ch in range(n_ch):
        states.append(st_t.astype(BF16))
        st_t = st_t * jnp.exp(bl[ch]) + upd[ch]
    st_ref[...] = st_t

    o_inter = [lax.dot_general((qs[ch] * jnp.exp(bc[ch])).astype(BF16), states[ch], NT_DIMS,
                               preferred_element_type=F32) for ch in range(n_ch)]
    o_off = [[jnp.dot(att[ch][blk - 1], ivb[ch][:blk * sub], preferred_element_type=F32)
              for blk in range(1, n_sub)] for ch in range(n_ch)]

    for ch in range(n_ch):
        outs = []
        for blk in range(n_sub):
            lo = blk * sub
            o_i = o_inter[ch][lo:lo + sub]
            if blk > 0:
                o_i = o_i + o_off[ch][blk - 1]
            q_i, k_i, iv_i, fg_i = (x[ch][lo:lo + sub] for x in (qs, kk, iv, fg))
            u = jnp.zeros_like(q_i)
            for s in reversed(range(sub)):
                decayed = u * fg_i[s + 1:s + 2] if s + 1 < sub else u
                u = jnp.where(trow == s, q_i, decayed)
                a_col = jnp.sum(u * k_i[s:s + 1], axis=1, keepdims=True)
                o_i = o_i + a_col * iv_i[s:s + 1]
            outs.append(o_i)
        o = jnp.concatenate(outs, axis=0)
        o = o * lax.rsqrt(jnp.mean(o * o, axis=-1, keepdims=True) + RMS_EPS) * ng_ref[...]
        o_ref[rows[ch], :] = (o * _silu(g_ref[rows[ch], :])).astype(o_ref.dtype)


def _hgrn(proj, lb, norm_g, bsz, seq, tc=512):
    t = bsz * seq
    tiles = seq // tc
    hd = HGRN_HEAD_DIM
    spec = lambda off: pl.BlockSpec((tc, hd), lambda b, h, j, off=off: (b * tiles + j, off + h))
    return pl.pallas_call(
        functools.partial(_hgrn_kernel, tc=tc),
        grid=(bsz, HGRN_HEADS, tiles),
        in_specs=[spec(0), spec(HGRN_HEADS), spec(2 * HGRN_HEADS), spec(3 * HGRN_HEADS),
                  pl.BlockSpec((1, hd), lambda b, h, j: (0, h)),
                  pl.BlockSpec((1, hd), lambda b, h, j: (0, 0))],
        out_specs=spec(0),
        out_shape=jax.ShapeDtypeStruct((t, D_MODEL), BF16),
        scratch_shapes=[pltpu.VMEM((hd, hd), F32)],
        compiler_params=_cparams("parallel", "parallel", "arbitrary"),
        name="hgrn2",
    )(proj, proj, proj, proj, lb.reshape(1, D_MODEL), norm_g.astype(F32).reshape(1, hd))


MOBA_SUPER = 2


def _moba_kernel(q_ref, k_ref, v_ref, o_ref, km_ref, kp_ref, vt_ref, qp_ref, acc_ref, *s_refs, n_blk, n_sel):
    blk = MOBA_BLOCK
    sbk = MOBA_SUPER * blk
    lane = lax.broadcasted_iota(jnp.int32, (1, LANES), 1)
    head_of_lane = lane // HEAD_DIM
    hot_lane0 = [(1 - hh) * HEAD_DIM for hh in range(HEADS_PER_TILE)]
    n_rows = HEADS_PER_TILE * n_blk

    for n in range(n_blk):
        rows = slice(n * blk, (n + 1) * blk)
        kb = k_ref[rows, :]
        vt_ref[:, rows] = v_ref[rows, :].T.astype(BF16)
        km = jnp.sum(kb, axis=0, keepdims=True) * (1.0 / blk)
        for hh in range(HEADS_PER_TILE):
            mine = head_of_lane == hh
            km_ref[hh * n_blk + n:hh * n_blk + n + 1, :] = jnp.where(mine, km, 0.0)
            hot = (lane == hot_lane0[hh] + n).astype(F32)
            kp_ref[rows, hh * LANES:(hh + 1) * LANES] = jnp.where(mine, kb, hot).astype(BF16)

    prow = lax.broadcasted_iota(jnp.int32, (n_rows, HEADS_PER_TILE * LANES), 0)
    pcol = lax.broadcasted_iota(jnp.int32, (n_rows, HEADS_PER_TILE * LANES), 1)
    target = jnp.zeros_like(prow)
    for hh in range(HEADS_PER_TILE):
        target = jnp.where(prow // n_blk == hh, hh * LANES + hot_lane0[hh] + prow % n_blk, target)
    place = (pcol == target).astype(BF16)
    nid = lax.broadcasted_iota(jnp.int32, (n_blk, blk), 0)
    nid_f = nid.astype(F32)
    km_all = km_ref[...]

    q_rows = [slice(qi * blk, (qi + 1) * blk) for qi in range(n_blk)]
    gates = [lax.dot_general(km_all, q_ref[rw, :], NT_DIMS, preferred_element_type=F32,
                             precision=lax.Precision.HIGHEST) for rw in q_rows]
    biases = []
    for qi in range(n_blk):
        past = nid < qi
        bias_rows = []
        for hh in range(HEADS_PER_TILE):
            g = jnp.where(past, gates[qi][hh * n_blk:(hh + 1) * n_blk], NEG)
            keep = nid == qi
            for _ in range(n_sel):
                mx = jnp.max(g, axis=0, keepdims=True)
                idx = jnp.min(jnp.where(g == mx, nid_f, float(n_blk)), axis=0, keepdims=True)
                pick = nid_f == idx
                keep = jnp.logical_or(keep, jnp.logical_and(pick, past))
                g = jnp.where(pick, BELOW_NEG, g)
            bias_rows.append(jnp.where(keep, 0.0, NEG))
        biases.append(jnp.concatenate(bias_rows, axis=0).astype(BF16))
    bias_qs = [lax.dot_general(b, place, TN_DIMS, preferred_element_type=F32) for b in biases]
    for qi in range(n_blk):
        q = q_ref[q_rows[qi], :]
        for hh in range(HEADS_PER_TILE):
            qp_ref[q_rows[qi], hh * LANES:(hh + 1) * LANES] = jnp.where(
                head_of_lane == hh, q, bias_qs[qi][:, hh * LANES:(hh + 1) * LANES]).astype(BF16)

    krow = lax.broadcasted_iota(jnp.int32, (sbk, blk), 0)
    qcol = lax.broadcasted_iota(jnp.int32, (sbk, blk), 1)
    rel = qcol - krow

    n_chain = MOBA_SUPER * HEADS_PER_TILE

    def q_group(j, carry):
        q0 = pl.multiple_of(j * sbk, sbk)
        qps = [[qp_ref[pl.ds(q0 + t * blk, blk), hh * LANES:(hh + 1) * LANES] for hh in range(HEADS_PER_TILE)]
               for t in range(MOBA_SUPER)]
        acc_ref[...] = jnp.zeros_like(acc_ref)

        def scores_t(k0, t, hh):
            kpn = kp_ref[pl.ds(k0, sbk), hh * LANES:(hh + 1) * LANES]
            return lax.dot_general(kpn, qps[t][hh], NT_DIMS, preferred_element_type=F32)

        def key_step(g, stats, last):
            k0 = pl.multiple_of(g * sbk, sbk)
            v_t = [vt_ref[hh * HEAD_DIM:(hh + 1) * HEAD_DIM, pl.ds(k0, sbk)] for hh in range(HEADS_PER_TILE)]
            new = []
            for t in range(MOBA_SUPER):
                for hh in range(HEADS_PER_TILE):
                    c = t * HEADS_PER_TILE + hh
                    m_o, l_o = stats[2 * c:2 * c + 2]
                    s_t = s_refs[c][...]
                    if last:
                        s_t = jnp.where(rel >= -t * blk, s_t, NEG)
                    m_n = jnp.maximum(m_o, jnp.max(s_t, axis=0, keepdims=True))
                    alpha = jnp.exp(m_o - m_n)
                    p = jnp.exp(s_t - m_n)
                    acc_ref[c] = alpha * acc_ref[c] + jnp.dot(v_t[hh], p.astype(BF16), preferred_element_type=F32)
                    new += [m_n, alpha * l_o + jnp.sum(p, axis=0, keepdims=True)]
                    if not last:
                        s_refs[c][...] = scores_t(k0 + sbk, t, hh)
            return tuple(new)

        for t in range(MOBA_SUPER):
            for hh in range(HEADS_PER_TILE):
                s_refs[t * HEADS_PER_TILE + hh][...] = scores_t(0, t, hh)
        init = (jnp.full((1, blk), BELOW_NEG, F32), jnp.zeros((1, blk), F32)) * n_chain
        stats = lax.fori_loop(0, j, functools.partial(key_step, last=False), init)
        stats = key_step(j, stats, last=True)
        for t in range(MOBA_SUPER):
            o_t = jnp.concatenate([acc_ref[t * HEADS_PER_TILE + hh] / stats[2 * (t * HEADS_PER_TILE + hh) + 1]
                                   for hh in range(HEADS_PER_TILE)], axis=0)
            o_ref[pl.ds(q0 + t * blk, blk), :] = o_t.T.astype(o_ref.dtype)
        return carry

    lax.fori_loop(0, n_blk // MOBA_SUPER, q_group, 0)


def _moba(qkv, bsz, seq):
    blk = MOBA_BLOCK
    assert seq % (blk * MOBA_SUPER) == 0
    n_blk = seq // blk
    n_sel = min(MOBA_TOPK, n_blk - 1)
    n_tiles = D_MODEL // LANES
    qkv3 = qkv.reshape(bsz, seq, 3 * D_MODEL)
    spec = lambda off: pl.BlockSpec((None, seq, LANES), lambda b, hp, off=off: (b, 0, off + hp))
    out = pl.pallas_call(
        functools.partial(_moba_kernel, n_blk=n_blk, n_sel=n_sel),
        grid=(bsz, n_tiles),
        in_specs=[spec(0), spec(n_tiles), spec(2 * n_tiles)],
        out_specs=pl.BlockSpec((None, seq, LANES), lambda b, hp: (b, 0, hp)),
        out_shape=jax.ShapeDtypeStruct((bsz, seq, D_MODEL), BF16),
        scratch_shapes=[pltpu.VMEM((HEADS_PER_TILE * n_blk, LANES), F32),
                        pltpu.VMEM((seq, HEADS_PER_TILE * LANES), BF16),
                        pltpu.VMEM((LANES, seq), BF16),
                        pltpu.VMEM((seq, HEADS_PER_TILE * LANES), BF16),
                        pltpu.VMEM((MOBA_SUPER * HEADS_PER_TILE, HEAD_DIM, blk), F32)]
                       + [pltpu.VMEM((MOBA_SUPER * blk, blk), F32)] * (MOBA_SUPER * HEADS_PER_TILE),
        compiler_params=_cparams("parallel", "parallel"),
        name="moba_attn",
    )(qkv3, qkv3, qkv3)
    return out.reshape(bsz * seq, D_MODEL)


def _rope_tables(positions):
    half = ROPE_DIM // 2
    inv = ROPE_THETA ** (-jnp.arange(0, ROPE_DIM, 2, dtype=F32) / ROPE_DIM)
    ang = positions.astype(F32).reshape(-1)[:, None] * inv
    cos, sin = jnp.cos(ang), jnp.sin(ang)
    t = cos.shape[0]
    rest = HEAD_DIM - ROPE_DIM
    c64 = jnp.concatenate([cos, cos, jnp.ones((t, rest), F32)], axis=1)
    s1 = jnp.concatenate([-sin, jnp.zeros((t, HEAD_DIM - half), F32)], axis=1)
    s2 = jnp.concatenate([jnp.zeros((t, half), F32), sin, jnp.zeros((t, rest), F32)], axis=1)
    tile = lambda a: jnp.tile(a, (1, HEADS_PER_TILE))
    return tile(c64), tile(s1), tile(s2)


def _qkv_weight(w):
    scale = jnp.concatenate([jnp.full((D_MODEL,), HEAD_DIM ** -0.5, F32), jnp.ones((2 * D_MODEL,), F32)])
    return (w.astype(F32) * scale[None, :]).astype(BF16)


def kernel(x, positions, norm_mix, norm_ffn, norm_final, s5_a_re, s5_a_im, s5_log_dt, s5_b_re, s5_b_im, s5_c_re, s5_c_im, s5_d, s5_w_glu, s5_b_glu, dil_w_qkv, dil_w_o, hgrn_w_in, hgrn_lower_bound, hgrn_norm, hgrn_w_o, moba_w_qkv, moba_w_o, ffn_w_gate_up, ffn_w_down):
    bsz, seq, d = x.shape
    depth = norm_mix.shape[0]
    n_mixers = 4
    t = bsz * seq
    rope = _rope_tables(positions)
    lb_w = jax.nn.softmax(hgrn_lower_bound.astype(F32), axis=0)
    lower_bounds = jnp.cumsum(lb_w, axis=0) - lb_w[0]
    norm_mix = norm_mix.astype(F32)
    norm_ffn = norm_ffn.astype(F32)
    wgu = ffn_w_gate_up.astype(BF16)
    wdn = ffn_w_down.astype(BF16)

    h = x.reshape(t, d).astype(F32)
    for layer in range(depth):
        mixer, j = layer % n_mixers, layer // n_mixers
        final_g = norm_final.astype(F32) if layer == depth - 1 else None
        mix = None
        if mixer == 0:
            h = _s5(h, norm_mix[layer], s5_a_re[j], s5_a_im[j], s5_log_dt[j], s5_b_re[j], s5_b_im[j],
                    s5_c_re[j], s5_c_im[j], s5_d[j], s5_w_glu[j], s5_b_glu[j], seq)
        elif mixer == 1:
            qkv = _proj(h, norm_mix[layer], _qkv_weight(dil_w_qkv[j]), rope, 2 * D_MODEL)
            mix = (_dilated(qkv, bsz, seq), dil_w_o[j].astype(BF16))
        elif mixer == 2:
            proj = _proj(h, norm_mix[layer], hgrn_w_in[j].astype(BF16))
            mix = (_hgrn(proj, lower_bounds[layer], hgrn_norm[j], bsz, seq), hgrn_w_o[j].astype(BF16))
        else:
            qkv = _proj(h, norm_mix[layer], _qkv_weight(moba_w_qkv[j]), rope, 2 * D_MODEL)
            mix = (_moba(qkv, bsz, seq), moba_w_o[j].astype(BF16))
        h = _ffn(h, norm_ffn[layer], wgu[layer], wdn[layer], mix=mix, final_g=final_g)
    return h.reshape(bsz, seq, d).astype(x.dtype)
```

```python
import functools
import math

import jax
import jax.numpy as jnp
from jax import lax
from jax.experimental import pallas as pl
from jax.experimental.pallas import tpu as pltpu

F32 = jnp.float32
BF16 = jnp.bfloat16

D_MODEL = 1024
D_FF = 2816
RMS_EPS = 1e-6
NEG = -1e30
BELOW_NEG = -3e38

HEAD_DIM = 64
ATT_HEADS = D_MODEL // HEAD_DIM
ROPE_DIM = HEAD_DIM // 4
ROPE_THETA = 500000.0
LANES = 128
SUBLANES = 8
HEADS_PER_TILE = LANES // HEAD_DIM

S5_GROUP = 16
S5_GROUPS = D_MODEL // S5_GROUP
S5_STATE = 64
S5_CH = S5_GROUPS * S5_STATE
S5_SLABS = D_MODEL // LANES
S5_SLAB_CH = S5_CH // S5_SLABS

DIL_PATTERNS = ((128, 1), (512, 4), (2048, 16))
DIL_BLOCK = 128

HGRN_HEAD_DIM = 128
HGRN_HEADS = D_MODEL // HGRN_HEAD_DIM
HGRN_CHUNK = 64
HGRN_SUB = 16

MOBA_BLOCK = 256
MOBA_TOPK = 3

VMEM_LIMIT = 56 * 1024 * 1024

NT_DIMS = (((1,), (1,)), ((), ()))
TN_DIMS = (((0,), (0,)), ((), ()))


def _cparams(*sem):
    return pltpu.CompilerParams(dimension_semantics=sem, vmem_limit_bytes=VMEM_LIMIT)


def _rms(x, g):
    return x * lax.rsqrt(jnp.mean(x * x, axis=-1, keepdims=True) + RMS_EPS) * g


def _silu(x):
    return x * jax.nn.sigmoid(x)


def _proj_kernel(*refs, n_out, tn, rope_cols):
    if rope_cols:
        x_ref, g_ref, w_ref, c_ref, s1_ref, s2_ref, o_ref = refs
    else:
        x_ref, g_ref, w_ref, o_ref = refs
    x = x_ref[...]
    xn = _rms(x, g_ref[...]).astype(BF16)
    for c in range(n_out // tn):
        y = jnp.dot(xn, w_ref[:, c * tn:(c + 1) * tn], preferred_element_type=F32)
        if c * tn < rope_cols:
            cos, s1, s2 = c_ref[...], s1_ref[...], s2_ref[...]
            parts = []
            for j in range(tn // LANES):
                yj = y[:, j * LANES:(j + 1) * LANES]
                half = ROPE_DIM // 2
                parts.append(yj * cos + pltpu.roll(yj, LANES - half, 1) * s1 + pltpu.roll(yj, half, 1) * s2)
            y = jnp.concatenate(parts, axis=1) if len(parts) > 1 else parts[0]
        o_ref[:, c * tn:(c + 1) * tn] = y


def _proj(h, g, w, rope=None, rope_cols=0, tm=256, tn=256):
    t, d = h.shape
    n_out = w.shape[1]
    in_specs = [pl.BlockSpec((tm, d), lambda i: (i, 0)),
                pl.BlockSpec((1, d), lambda i: (0, 0)),
                pl.BlockSpec((d, n_out), lambda i: (0, 0))]
    args = [h, g.reshape(1, d), w]
    if rope_cols:
        in_specs += [pl.BlockSpec((tm, LANES), lambda i: (i, 0))] * 3
        args += list(rope)
    return pl.pallas_call(
        functools.partial(_proj_kernel, n_out=n_out, tn=tn, rope_cols=rope_cols),
        grid=(t // tm,),
        in_specs=in_specs,
        out_specs=pl.BlockSpec((tm, n_out), lambda i: (i, 0)),
        out_shape=jax.ShapeDtypeStruct((t, n_out), F32),
        compiler_params=_cparams("parallel"),
        name="proj_rope" if rope_cols else "proj",
    )(*args)


def _ffn_kernel(*refs, has_mix, has_final, fc):
    refs = list(refs)
    h_ref, g_ref, wgu_ref, wd_ref = refs[:4]
    pos = 4
    if has_mix:
        a_ref, wo_ref = refs[pos:pos + 2]
        pos += 2
    if has_final:
        fg_ref = refs[pos]
        pos += 1
    o_ref, acc_ref = refs[pos], refs[pos + 1]

    h = h_ref[...]
    if has_mix:
        h = h + jnp.dot(a_ref[...], wo_ref[...], preferred_element_type=F32)
    xn = _rms(h, g_ref[...]).astype(BF16)

    def gate_up(c):
        gate = jnp.dot(xn, wgu_ref[:, c * fc:(c + 1) * fc], preferred_element_type=F32)
        up = jnp.dot(xn, wgu_ref[:, D_FF + c * fc:D_FF + (c + 1) * fc], preferred_element_type=F32)
        return gate, up

    n_chunks = D_FF // fc
    nxt = gate_up(0)
    for c in range(n_chunks):
        gate, up = nxt
        if c + 1 < n_chunks:
            nxt = gate_up(c + 1)
        act = (_silu(gate) * up).astype(BF16)
        contrib = jnp.dot(act, wd_ref[c * fc:(c + 1) * fc, :], preferred_element_type=F32)
        if c == 0:
            acc_ref[...] = contrib
        else:
            acc_ref[...] += contrib
    out = h + acc_ref[...]
    if has_final:
        out = _rms(out, fg_ref[...])
    o_ref[...] = out


def _ffn(h, g, wgu, wd, mix=None, final_g=None, tm=256, fc=256):
    t, d = h.shape
    const = lambda i: (0, 0)
    in_specs = [pl.BlockSpec((tm, d), lambda i: (i, 0)),
                pl.BlockSpec((1, d), const),
                pl.BlockSpec((d, 2 * D_FF), const),
                pl.BlockSpec((D_FF, d), const)]
    args = [h, g.reshape(1, d), wgu, wd]
    if mix is not None:
        a, wo = mix
        in_specs += [pl.BlockSpec((tm, d), lambda i: (i, 0)), pl.BlockSpec((d, d), const)]
        args += [a, wo]
    if final_g is not None:
        in_specs += [pl.BlockSpec((1, d), const)]
        args += [final_g.reshape(1, d)]
    return pl.pallas_call(
        functools.partial(_ffn_kernel, has_mix=mix is not None, has_final=final_g is not None, fc=fc),
        grid=(t // tm,),
        in_specs=in_specs,
        out_specs=pl.BlockSpec((tm, d), lambda i: (i, 0)),
        out_shape=jax.ShapeDtypeStruct((t, d), F32),
        scratch_shapes=[pltpu.VMEM((tm, d), F32)],
        compiler_params=_cparams("parallel"),
        name="ffn",
    )(*args)


def _s5_kernel(x_ref, g_ref, wbr_ref, wbi_ref, akr_ref, aki_ref, pr_ref, pi_ref, wcr_ref, wci_ref,
               d_ref, wglu_ref, bglu_ref, o_ref, er_ref, ei_ref, cr_ref, ci_ref, *, tm, tiles_per_seq):
    i = pl.program_id(0)

    @pl.when(i % tiles_per_seq == 0)
    def _():
        cr_ref[...] = jnp.zeros_like(cr_ref)
        ci_ref[...] = jnp.zeros_like(ci_ref)

    x = x_ref[...]
    u = _rms(x, g_ref[...])
    ub = u.astype(BF16)
    for s in range(S5_SLABS):
        us = ub[:, s * LANES:(s + 1) * LANES]
        er_ref[:, s * S5_SLAB_CH:(s + 1) * S5_SLAB_CH] = jnp.dot(us, wbr_ref[s], preferred_element_type=F32)
        ei_ref[:, s * S5_SLAB_CH:(s + 1) * S5_SLAB_CH] = jnp.dot(us, wbi_ref[s], preferred_element_type=F32)

    def group(j, carry):
        r0 = pl.multiple_of(j * SUBLANES, SUBLANES)
        xr = er_ref[pl.ds(r0, SUBLANES), :]
        xi = ei_ref[pl.ds(r0, SUBLANES), :]
        for lvl in range(3):
            k = 1 << lvl
            sr = pltpu.roll(xr, k, 0)
            si = pltpu.roll(xi, k, 0)
            ar, ai = akr_ref[lvl], aki_ref[lvl]
            xr, xi = xr + ar * sr - ai * si, xi + ar * si + ai * sr
        cr, ci = cr_ref[...], ci_ref[...]
        pr, pi_ = pr_ref[...], pi_ref[...]
        hr = xr + pr * cr - pi_ * ci
        hi = xi + pr * ci + pi_ * cr
        er_ref[pl.ds(r0, SUBLANES), :] = hr
        ei_ref[pl.ds(r0, SUBLANES), :] = hi
        cr_ref[...] = jnp.broadcast_to(hr[SUBLANES - 1:SUBLANES, :], (SUBLANES, S5_CH))
        ci_ref[...] = jnp.broadcast_to(hi[SUBLANES - 1:SUBLANES, :], (SUBLANES, S5_CH))
        return carry

    lax.fori_loop(0, tm // SUBLANES, group, 0)

    ys = []
    for s in range(S5_SLABS):
        hr = er_ref[:, s * S5_SLAB_CH:(s + 1) * S5_SLAB_CH].astype(BF16)
        hi = ei_ref[:, s * S5_SLAB_CH:(s + 1) * S5_SLAB_CH].astype(BF16)
        ys.append(jnp.dot(hr, wcr_ref[s], preferred_element_type=F32)
                  - jnp.dot(hi, wci_ref[s], preferred_element_type=F32))
    y = jnp.concatenate(ys, axis=1) + d_ref[...] * u
    z = jax.nn.gelu(y).astype(BF16)
    zz = jnp.dot(z, wglu_ref[...], preferred_element_type=F32) + bglu_ref[...]
    o_ref[...] = x + zz[:, :D_MODEL] * jax.nn.sigmoid(zz[:, D_MODEL:])


def _block_diag_slabs(w):
    g, r, c = w.shape
    per = g // S5_SLABS
    w = w.reshape(S5_SLABS, per, r, c)
    eye = jnp.eye(per, dtype=w.dtype)
    return jnp.einsum('sgrc,gh->sgrhc', w, eye).reshape(S5_SLABS, per * r, per * c)


def _s5_tables(a_re, a_im, log_dt):
    lr, li = a_re.astype(F32), a_im.astype(F32)
    dt = jnp.exp(log_dt.astype(F32))[:, None]
    mag = jnp.exp(lr * dt)
    ab_re, ab_im = mag * jnp.cos(li * dt), mag * jnp.sin(li * dt)
    den = lr * lr + li * li
    m_re = ab_re - 1.0
    f_re = (m_re * lr + ab_im * li) / den
    f_im = (ab_im * lr - m_re * li) / den

    def power(k):
        return ((jnp.exp(lr * dt * k) * jnp.cos(li * dt * k)).reshape(-1),
                (jnp.exp(lr * dt * k) * jnp.sin(li * dt * k)).reshape(-1))

    rows = jnp.arange(SUBLANES)[:, None]
    akr, aki = [], []
    for k in (1, 2, 4):
        pr, pi_ = power(float(k))
        akr.append(jnp.where(rows >= k, pr[None, :], 0.0))
        aki.append(jnp.where(rows >= k, pi_[None, :], 0.0))
    pw = [power(float(k + 1)) for k in range(SUBLANES)]
    p_re = jnp.stack([p[0] for p in pw])
    p_im = jnp.stack([p[1] for p in pw])
    return f_re, f_im, jnp.stack(akr), jnp.stack(aki), p_re, p_im


def _s5(h, g, a_re, a_im, log_dt, b_re, b_im, c_re, c_im, d_skip, w_glu, b_glu, seq, tm=256):
    t, d = h.shape
    f_re, f_im, akr, aki, p_re, p_im = _s5_tables(a_re, a_im, log_dt)
    bw_re = f_re[:, :, None] * b_re - f_im[:, :, None] * b_im
    bw_im = f_re[:, :, None] * b_im + f_im[:, :, None] * b_re
    wbr = _block_diag_slabs(jnp.swapaxes(bw_re, 1, 2)).astype(BF16)
    wbi = _block_diag_slabs(jnp.swapaxes(bw_im, 1, 2)).astype(BF16)
    wcr = _block_diag_slabs(jnp.swapaxes(c_re.astype(F32), 1, 2)).astype(BF16)
    wci = _block_diag_slabs(jnp.swapaxes(c_im.astype(F32), 1, 2)).astype(BF16)
    c2 = lambda i: (0, 0)
    c3 = lambda i: (0, 0, 0)
    return pl.pallas_call(
        functools.partial(_s5_kernel, tm=tm, tiles_per_seq=seq // tm),
        grid=(t // tm,),
        in_specs=[pl.BlockSpec((tm, d), lambda i: (i, 0)),
                  pl.BlockSpec((1, d), c2),
                  pl.BlockSpec(wbr.shape, c3), pl.BlockSpec(wbi.shape, c3),
                  pl.BlockSpec(akr.shape, c3), pl.BlockSpec(aki.shape, c3),
                  pl.BlockSpec(p_re.shape, c2), pl.BlockSpec(p_im.shape, c2),
                  pl.BlockSpec(wcr.shape, c3), pl.BlockSpec(wci.shape, c3),
                  pl.BlockSpec((1, d), c2),
                  pl.BlockSpec((d, 2 * d), c2),
                  pl.BlockSpec((1, 2 * d), c2)],
        out_specs=pl.BlockSpec((tm, d), lambda i: (i, 0)),
        out_shape=jax.ShapeDtypeStruct((t, d), F32),
        scratch_shapes=[pltpu.VMEM((tm, S5_CH), F32), pltpu.VMEM((tm, S5_CH), F32),
                        pltpu.VMEM((SUBLANES, S5_CH), F32), pltpu.VMEM((SUBLANES, S5_CH), F32)],
        compiler_params=_cparams("arbitrary"),
        name="s5",
    )(h, g.reshape(1, d), wbr, wbi, akr, aki, p_re, p_im, wcr, wci,
      d_skip.astype(F32).reshape(1, d), w_glu.astype(BF16), b_glu.astype(F32).reshape(1, 2 * d))


DIL_CLASSES = 16
DIL_STEPS = 2


def _dil_kernel(*refs, seq):
    ncls = DIL_CLASSES
    u_len = seq // ncls
    blk = DIL_BLOCK
    q_cls, k_cls, v_cls = refs[0:ncls], refs[ncls:2 * ncls], refs[2 * ncls:3 * ncls]
    o_ref = refs[3 * ncls]
    qp_ref, kp_ref, vp_ref, acc_ref, lsw_ref, m_ref, msw_ref = refs[3 * ncls + 1:3 * ncls + 8]
    n_chain = DIL_STEPS * HEADS_PER_TILE
    bufs = refs[3 * ncls + 8:]
    buf_a, buf_b = bufs[:n_chain], bufs[n_chain:]
    for r in range(ncls):
        rows = slice(r * u_len, (r + 1) * u_len)
        qp_ref[rows, :] = q_cls[r][...]
        kp_ref[rows, :] = k_cls[r][...]
        vp_ref[rows, :] = v_cls[r][...]

    lane = lax.broadcasted_iota(jnp.int32, (1, LANES), 1)
    low = lane < HEAD_DIM
    n_iter = seq // blk // DIL_STEPS

    for pat, (window, dil) in enumerate(DIL_PATTERNS):
        assert window // dil == blk and ncls % dil == 0 and seq % (dil * blk) == 0
        n_run = ncls // dil
        run = SUBLANES * dil
        n_blk = u_len // run
        jq = lax.broadcasted_iota(jnp.int32, (blk, blk), 0)
        jk = lax.broadcasted_iota(jnp.int32, (blk, blk), 1)
        wq = (jq % run) * n_run + jq // run
        wk = (jk % run) * n_run + jk // run
        mask_cur = wk <= wq
        mask_prev = wk >= wq

        def run_rows(step, dil=dil, n_run=n_run, run=run, n_blk=n_blk):
            res = step // n_blk
            bi = step % n_blk
            base = [(res + dil * c) * u_len for c in range(n_run)]
            cur = [pl.multiple_of(b + run * bi, SUBLANES) for b in base]
            prev = [pl.multiple_of(b + run * jnp.maximum(bi - 1, 0), SUBLANES) for b in base]
            return cur, prev, bi > 0

        def gather(ref, offs, run=run):
            parts = [ref[pl.ds(o, run), :] for o in offs]
            return jnp.concatenate(parts, axis=0) if len(parts) > 1 else parts[0]

        def scores(step, hh):
            cur, prev, _ = run_rows(step)
            q = gather(qp_ref, cur)
            q = jnp.where(low if hh == 0 else jnp.logical_not(low), q, 0.0).astype(BF16)
            kcat = jnp.concatenate([gather(kp_ref, prev), gather(kp_ref, cur)], axis=0).astype(BF16)
            return lax.dot_general(q, kcat, NT_DIMS, preferred_element_type=F32)

        def phase(it, bufs, nxt_bufs, pat=pat):
            nxt = jnp.minimum(it + 1, n_iter - 1)
            ahead = [(t, hh) for t in range(DIL_STEPS) for hh in range(HEADS_PER_TILE)]

            def stage(i):
                for t2, hh2 in ahead[i:i + 1]:
                    nxt_bufs[t2 * HEADS_PER_TILE + hh2][...] = scores(nxt * DIL_STEPS + t2, hh2)

            stage(0)
            stage(1)
            for t in range(DIL_STEPS):
                step = it * DIL_STEPS + t
                cur, prev, has_prev = run_rows(step)
                vcat = jnp.concatenate([gather(vp_ref, prev), gather(vp_ref, cur)], axis=0)
                mask = jnp.concatenate([jnp.logical_and(mask_prev, has_prev), mask_cur], axis=1)
                ms, os_ = [], []
                for hh in range(HEADS_PER_TILE):
                    c = t * HEADS_PER_TILE + hh
                    s = jnp.where(mask, bufs[c][...], NEG)
                    mb = jnp.max(s, axis=1, keepdims=True)
                    p = jnp.exp2(s - mb).astype(BF16)
                    vh = jnp.where(low if hh == 0 else jnp.logical_not(low), vcat, 1.0).astype(BF16)
                    os_.append(jnp.dot(p, vh, preferred_element_type=F32))
                    ms.append(jnp.broadcast_to(mb, (blk, LANES)))
                    stage(c + 2)
                m_b = jnp.where(low, ms[0], ms[1])
                msw_b = jnp.where(low, ms[1], ms[0])
                o_b = jnp.where(low, os_[0], os_[1])
                lsw_b = jnp.where(low, os_[1], os_[0])
                for ci, off in enumerate(cur):
                    rows = pl.ds(off, run)
                    piece = slice(ci * run, (ci + 1) * run)
                    if pat == 0:
                        m_ref[rows, :] = m_b[piece]
                        msw_ref[rows, :] = msw_b[piece]
                        acc_ref[rows, :] = o_b[piece]
                        lsw_ref[rows, :] = lsw_b[piece]
                    else:
                        m_o, msw_o = m_ref[rows, :], msw_ref[rows, :]
                        m_n = jnp.maximum(m_o, m_b[piece])
                        msw_n = jnp.maximum(msw_o, msw_b[piece])
                        m_ref[rows, :] = m_n
                        msw_ref[rows, :] = msw_n
                        acc_ref[rows, :] = (acc_ref[rows, :] * jnp.exp2(m_o - m_n)
                                            + o_b[piece] * jnp.exp2(m_b[piece] - m_n))
                        lsw_ref[rows, :] = (lsw_ref[rows, :] * jnp.exp2(msw_o - msw_n)
                                            + lsw_b[piece] * jnp.exp2(msw_b[piece] - msw_n))

        def two_phases(i, carry, phase=phase):
            phase(2 * i, buf_a, buf_b)
            phase(2 * i + 1, buf_b, buf_a)
            return carry

        for t in range(DIL_STEPS):
            for hh in range(HEADS_PER_TILE):
                buf_a[t * HEADS_PER_TILE + hh][...] = scores(t, hh)
        lax.fori_loop(0, n_iter // 2, two_phases, 0)

    def finish(c, carry):
        r0 = pl.multiple_of(c * 256, 256)
        den = pltpu.roll(lsw_ref[pl.ds(r0, 256), :], HEAD_DIM, 1)
        o_ref[pl.ds(r0, 256), :] = (acc_ref[pl.ds(r0, 256), :] / den).astype(o_ref.dtype)
        return carry

    lax.fori_loop(0, seq // 256, finish, 0)


def _dilated(qkv, bsz, seq):
    ncls = DIL_CLASSES
    assert seq % (ncls * DIL_BLOCK) == 0 and (seq // DIL_BLOCK) % (2 * DIL_STEPS) == 0 and seq % 256 == 0
    u_len = seq // ncls
    n_tiles = D_MODEL // LANES
    qkv3 = qkv.reshape(bsz, u_len, ncls * 3 * D_MODEL)
    cols = 3 * n_tiles
    spec = lambda off, r: pl.BlockSpec((None, u_len, LANES), lambda b, hp, off=off, r=r: (b, 0, r * cols + off + hp))
    in_specs = [spec(off, r) for off in (0, n_tiles, 2 * n_tiles) for r in range(ncls)]
    out = pl.pallas_call(
        functools.partial(_dil_kernel, seq=seq),
        grid=(bsz, n_tiles),
        in_specs=in_specs,
        out_specs=pl.BlockSpec((None, seq, LANES), lambda b, hp: (b, 0, hp)),
        out_shape=jax.ShapeDtypeStruct((bsz, seq, D_MODEL), BF16),
        scratch_shapes=[pltpu.VMEM((seq, LANES), F32)] * 7
                       + [pltpu.VMEM((DIL_BLOCK, 2 * DIL_BLOCK), F32)] * (2 * DIL_STEPS * HEADS_PER_TILE),
        compiler_params=_cparams("parallel", "parallel"),
        name="dilated_attn",
    )(*([qkv3] * len(in_specs)))
    out = out.reshape(bsz, ncls, u_len, D_MODEL).transpose(0, 2, 1, 3)
    return out.reshape(bsz * seq, D_MODEL)


def _hgrn_kernel(q_ref, f_ref, i_ref, g_ref, lb_ref, ng_ref, o_ref, st_ref, *, tc):
    @pl.when(pl.program_id(2) == 0)
    def _():
        st_ref[...] = jnp.zeros_like(st_ref)

    chunk, sub = HGRN_CHUNK, HGRN_SUB
    n_ch = tc // chunk
    n_sub = chunk // sub
    r = lax.broadcasted_iota(jnp.int32, (chunk, chunk), 0)
    c = lax.broadcasted_iota(jnp.int32, (chunk, chunk), 1)
    tril = (c <= r).astype(F32)
    trow = lax.broadcasted_iota(jnp.int32, (sub, 1), 0)
    lb = lb_ref[...]
    rows = [slice(ch * chunk, (ch + 1) * chunk) for ch in range(n_ch)]

    qs = [_silu(q_ref[rw, :]) for rw in rows]
    fg = [lb + (1.0 - lb) * jax.nn.sigmoid(f_ref[rw, :]) for rw in rows]
    kk = [1.0 - x for x in fg]
    iv = [i_ref[rw, :] for rw in rows]
    ivb = [x.astype(BF16) for x in iv]
    bc = [jnp.dot(tril, jnp.log(x), preferred_element_type=F32, precision=lax.Precision.HIGHEST) for x in fg]
    bl = [x[chunk - 1:chunk] for x in bc]

    upd = [lax.dot_general(ivb[ch], (kk[ch] * jnp.exp(bl[ch] - bc[ch])).astype(BF16), TN_DIMS,
                           preferred_element_type=F32) for ch in range(n_ch)]
    att = []
    for ch in range(n_ch):
        row_att = []
        for blk in range(1, n_sub):
            lo = blk * sub
            r_i = bc[ch][lo - 1:lo]
            qt = (qs[ch][lo:lo + sub] * jnp.exp(bc[ch][lo:lo + sub] - r_i)).astype(BF16)
            kt = (kk[ch][:lo] * jnp.exp(r_i - bc[ch][:lo])).astype(BF16)
            row_att.append(lax.dot_general(qt, kt, NT_DIMS, preferred_element_type=F32).astype(BF16))
        att.append(row_att)

    st_t = st_ref[...]
    states = []
    for ch in range(n_ch):
        states.append(st_t.astype(BF16))
        st_t = st_t * jnp.exp(bl[ch]) + upd[ch]
    st_ref[...] = st_t

    o_inter = [lax.dot_general((qs[ch] * jnp.exp(bc[ch])).astype(BF16), states[ch], NT_DIMS,
                               preferred_element_type=F32) for ch in range(n_ch)]
    o_off = [[jnp.dot(att[ch][blk - 1], ivb[ch][:blk * sub], preferred_element_type=F32)
              for blk in range(1, n_sub)] for ch in range(n_ch)]

    for ch in range(n_ch):
        outs = []
        for blk in range(n_sub):
            lo = blk * sub
            o_i = o_inter[ch][lo:lo + sub]
            if blk > 0:
                o_i = o_i + o_off[ch][blk - 1]
            q_i, k_i, iv_i, fg_i = (x[ch][lo:lo + sub] for x in (qs, kk, iv, fg))
            u = jnp.zeros_like(q_i)
            for s in reversed(range(sub)):
                decayed = u * fg_i[s + 1:s + 2] if s + 1 < sub else u
                u = jnp.where(trow == s, q_i, decayed)
                a_col = jnp.sum(u * k_i[s:s + 1], axis=1, keepdims=True)
                o_i = o_i + a_col * iv_i[s:s + 1]
            outs.append(o_i)
        o = jnp.concatenate(outs, axis=0)
        o = o * lax.rsqrt(jnp.mean(o * o, axis=-1, keepdims=True) + RMS_EPS) * ng_ref[...]
        o_ref[rows[ch], :] = (o * _silu(g_ref[rows[ch], :])).astype(o_ref.dtype)


def _hgrn(proj, lb, norm_g, bsz, seq, tc=512):
    t = bsz * seq
    tiles = seq // tc
    hd = HGRN_HEAD_DIM
    spec = lambda off: pl.BlockSpec((tc, hd), lambda b, h, j, off=off: (b * tiles + j, off + h))
    return pl.pallas_call(
        functools.partial(_hgrn_kernel, tc=tc),
        grid=(bsz, HGRN_HEADS, tiles),
        in_specs=[spec(0), spec(HGRN_HEADS), spec(2 * HGRN_HEADS), spec(3 * HGRN_HEADS),
                  pl.BlockSpec((1, hd), lambda b, h, j: (0, h)),
                  pl.BlockSpec((1, hd), lambda b, h, j: (0, 0))],
        out_specs=spec(0),
        out_shape=jax.ShapeDtypeStruct((t, D_MODEL), BF16),
        scratch_shapes=[pltpu.VMEM((hd, hd), F32)],
        compiler_params=_cparams("parallel", "parallel", "arbitrary"),
        name="hgrn2",
    )(proj, proj, proj, proj, lb.reshape(1, D_MODEL), norm_g.astype(F32).reshape(1, hd))


MOBA_SUPER = 2
MOBA_Q_TILES = 4
MOBA_VT_ROWS = HEAD_DIM + 16


def _moba_kernel(q_ref, k_ref, v_ref, o_ref, km_ref, kp_ref, vt_ref, qp_ref, acc_ref, *s_refs, n_blk, n_sel):
    blk = MOBA_BLOCK
    sbk = MOBA_SUPER * blk
    lane = lax.broadcasted_iota(jnp.int32, (1, LANES), 1)
    head_of_lane = lane // HEAD_DIM
    hot_lane0 = [(1 - hh) * HEAD_DIM for hh in range(HEADS_PER_TILE)]
    n_rows = HEADS_PER_TILE * n_blk

    ones_row = (lax.broadcasted_iota(jnp.int32, (MOBA_VT_ROWS - HEAD_DIM, 1), 0) == 0).astype(BF16)
    for hh in range(HEADS_PER_TILE):
        vt_ref[hh * MOBA_VT_ROWS + HEAD_DIM:(hh + 1) * MOBA_VT_ROWS, :] = jnp.broadcast_to(
            ones_row, (MOBA_VT_ROWS - HEAD_DIM, n_blk * blk))
    for n in range(n_blk):
        rows = slice(n * blk, (n + 1) * blk)
        kb = k_ref[rows, :]
        v_t = v_ref[rows, :].T
        for hh in range(HEADS_PER_TILE):
            vt_ref[hh * MOBA_VT_ROWS:hh * MOBA_VT_ROWS + HEAD_DIM, rows] = (
                v_t[hh * HEAD_DIM:(hh + 1) * HEAD_DIM].astype(BF16))
        km = jnp.sum(kb, axis=0, keepdims=True) * (1.0 / blk)
        for hh in range(HEADS_PER_TILE):
            mine = head_of_lane == hh
            km_ref[hh * n_blk + n:hh * n_blk + n + 1, :] = jnp.where(mine, km, 0.0)
            hot = (lane == hot_lane0[hh] + n).astype(F32)
            kp_ref[rows, hh * LANES:(hh + 1) * LANES] = jnp.where(mine, kb, hot).astype(BF16)

    prow = lax.broadcasted_iota(jnp.int32, (n_rows, HEADS_PER_TILE * LANES), 0)
    pcol = lax.broadcasted_iota(jnp.int32, (n_rows, HEADS_PER_TILE * LANES), 1)
    target = jnp.zeros_like(prow)
    for hh in range(HEADS_PER_TILE):
        target = jnp.where(prow // n_blk == hh, hh * LANES + hot_lane0[hh] + prow % n_blk, target)
    place = (pcol == target).astype(BF16)
    nid = lax.broadcasted_iota(jnp.int32, (n_blk, blk), 0)
    nid_f = nid.astype(F32)
    km_all = km_ref[...]

    q_rows = [slice(qi * blk, (qi + 1) * blk) for qi in range(n_blk)]
    gates = [lax.dot_general(km_all, q_ref[rw, :], NT_DIMS, preferred_element_type=F32,
                             precision=lax.Precision.HIGHEST) for rw in q_rows]
    biases = []
    for qi in range(n_blk):
        past = nid < qi
        bias_rows = []
        for hh in range(HEADS_PER_TILE):
            g = jnp.where(past, gates[qi][hh * n_blk:(hh + 1) * n_blk], NEG)
            keep = nid == qi
            for _ in range(n_sel):
                mx = jnp.max(g, axis=0, keepdims=True)
                idx = jnp.min(jnp.where(g == mx, nid_f, float(n_blk)), axis=0, keepdims=True)
                pick = nid_f == idx
                keep = jnp.logical_or(keep, jnp.logical_and(pick, past))
                g = jnp.where(pick, BELOW_NEG, g)
            bias_rows.append(jnp.where(keep, 0.0, NEG))
        biases.append(jnp.concatenate(bias_rows, axis=0).astype(BF16))
    bias_qs = [lax.dot_general(b, place, TN_DIMS, preferred_element_type=F32) for b in biases]
    for qi in range(n_blk):
        q = q_ref[q_rows[qi], :]
        for hh in range(HEADS_PER_TILE):
            qp_ref[q_rows[qi], hh * LANES:(hh + 1) * LANES] = jnp.where(
                head_of_lane == hh, q, bias_qs[qi][:, hh * LANES:(hh + 1) * LANES]).astype(BF16)

    krow = lax.broadcasted_iota(jnp.int32, (sbk, blk), 0)
    qcol = lax.broadcasted_iota(jnp.int32, (sbk, blk), 1)
    rel = qcol - krow

    n_qt = MOBA_Q_TILES
    assert n_qt == 2 * MOBA_SUPER
    n_chain = n_qt * HEADS_PER_TILE
    buf_a, buf_b = s_refs[:n_chain], s_refs[n_chain:]
    all_tiles = tuple(range(n_qt))

    def q_operands(g):
        q0 = pl.multiple_of(g * n_qt * blk, n_qt * blk)
        return [[qp_ref[pl.ds(q0 + t * blk, blk), hh * LANES:(hh + 1) * LANES] for hh in range(HEADS_PER_TILE)]
                for t in range(n_qt)]

    def stage(bufs, k0, qps, t, hh):
        kpn = kp_ref[pl.ds(k0, sbk), hh * LANES:(hh + 1) * LANES]
        bufs[t * HEADS_PER_TILE + hh][...] = lax.dot_general(kpn, qps[t][hh], NT_DIMS, preferred_element_type=F32)

    n_groups = n_blk // n_qt

    def q_group(g, carry):
        q0 = pl.multiple_of(g * n_qt * blk, n_qt * blk)
        qps = q_operands(g)
        acc_ref[...] = jnp.zeros_like(acc_ref)

        def step(stats, cur, k0, tiles, causal, nxt=None, nxt_tiles=(), nxt_k0=None, nxt_q=qps):
            chains = [(t, hh) for t in tiles for hh in range(HEADS_PER_TILE)]
            ahead = [(t, hh) for t in nxt_tiles for hh in range(HEADS_PER_TILE)]
            nxt_k0 = k0 + sbk if nxt_k0 is None else nxt_k0
            v_t = [vt_ref[hh * MOBA_VT_ROWS:(hh + 1) * MOBA_VT_ROWS, pl.ds(k0, sbk)] for hh in range(HEADS_PER_TILE)]
            stats = list(stats)
            for t, hh in ahead[:2]:
                stage(nxt, nxt_k0, nxt_q, t, hh)
            for i, (t, hh) in enumerate(chains):
                c = t * HEADS_PER_TILE + hh
                m_o = stats[c]
                s_t = cur[c][...]
                if t in causal:
                    s_t = jnp.where(rel >= causal[t], s_t, NEG)
                m_n = jnp.maximum(m_o, jnp.max(s_t, axis=0, keepdims=True))
                p = jnp.exp2(s_t - m_n).astype(BF16)
                acc_ref[c] = jnp.exp2(m_o - m_n) * acc_ref[c] + jnp.dot(v_t[hh], p, preferred_element_type=F32)
                stats[c] = m_n
                for t2, hh2 in ahead[i + 2:i + 3]:
                    stage(nxt, nxt_k0, nxt_q, t2, hh2)
            for t2, hh2 in ahead[len(chains) + 2:]:
                stage(nxt, nxt_k0, nxt_q, t2, hh2)
            return tuple(stats)

        def pair(i, stats):
            k0 = pl.multiple_of(i * 2 * sbk, 2 * sbk)
            stats = step(stats, buf_a, k0, all_tiles, {}, buf_b, all_tiles)
            return step(stats, buf_b, k0 + sbk, all_tiles, {}, buf_a, all_tiles)

        init = (jnp.full((1, blk), BELOW_NEG, F32),) * n_chain
        stats = lax.fori_loop(0, g, pair, init)
        late = all_tiles[MOBA_SUPER:]
        stats = step(stats, buf_a, q0, all_tiles, {t: -t * blk for t in all_tiles[:MOBA_SUPER]}, buf_b, late)
        step(stats, buf_b, q0 + sbk, late, {t: -(t - MOBA_SUPER) * blk for t in late},
             buf_a, all_tiles, 0, q_operands(jnp.minimum(g + 1, n_groups - 1)))
        for t in all_tiles:
            accs = [acc_ref[t * HEADS_PER_TILE + hh] for hh in range(HEADS_PER_TILE)]
            o_t = jnp.concatenate([a[:HEAD_DIM] / a[HEAD_DIM:HEAD_DIM + 1] for a in accs], axis=0)
            o_ref[pl.ds(q0 + t * blk, blk), :] = o_t.T.astype(o_ref.dtype)
        return carry

    first_q = q_operands(0)
    for t in all_tiles:
        for hh in range(HEADS_PER_TILE):
            stage(buf_a, 0, first_q, t, hh)
    lax.fori_loop(0, n_groups, q_group, 0)


def _moba(qkv, bsz, seq):
    blk = MOBA_BLOCK
    assert seq % (blk * MOBA_Q_TILES) == 0
    n_chain = MOBA_Q_TILES * HEADS_PER_TILE
    n_blk = seq // blk
    n_sel = min(MOBA_TOPK, n_blk - 1)
    n_tiles = D_MODEL // LANES
    qkv3 = qkv.reshape(bsz, seq, 3 * D_MODEL)
    spec = lambda off: pl.BlockSpec((None, seq, LANES), lambda b, hp, off=off: (b, 0, off + hp))
    out = pl.pallas_call(
        functools.partial(_moba_kernel, n_blk=n_blk, n_sel=n_sel),
        grid=(bsz, n_tiles),
        in_specs=[spec(0), spec(n_tiles), spec(2 * n_tiles)],
        out_specs=pl.BlockSpec((None, seq, LANES), lambda b, hp: (b, 0, hp)),
        out_shape=jax.ShapeDtypeStruct((bsz, seq, D_MODEL), BF16),
        scratch_shapes=[pltpu.VMEM((HEADS_PER_TILE * n_blk, LANES), F32),
                        pltpu.VMEM((seq, HEADS_PER_TILE * LANES), BF16),
                        pltpu.VMEM((HEADS_PER_TILE * MOBA_VT_ROWS, seq), BF16),
                        pltpu.VMEM((seq, HEADS_PER_TILE * LANES), BF16),
                        pltpu.VMEM((n_chain, MOBA_VT_ROWS, blk), F32)]
                       + [pltpu.VMEM((MOBA_SUPER * blk, blk), F32)] * (2 * n_chain),
        compiler_params=_cparams("parallel", "parallel"),
        name="moba_attn",
    )(qkv3, qkv3, qkv3)
    return out.reshape(bsz * seq, D_MODEL)


def _rope_tables(positions):
    half = ROPE_DIM // 2
    inv = ROPE_THETA ** (-jnp.arange(0, ROPE_DIM, 2, dtype=F32) / ROPE_DIM)
    ang = positions.astype(F32).reshape(-1)[:, None] * inv
    cos, sin = jnp.cos(ang), jnp.sin(ang)
    t = cos.shape[0]
    rest = HEAD_DIM - ROPE_DIM
    c64 = jnp.concatenate([cos, cos, jnp.ones((t, rest), F32)], axis=1)
    s1 = jnp.concatenate([-sin, jnp.zeros((t, HEAD_DIM - half), F32)], axis=1)
    s2 = jnp.concatenate([jnp.zeros((t, half), F32), sin, jnp.zeros((t, rest), F32)], axis=1)
    tile = lambda a: jnp.tile(a, (1, HEADS_PER_TILE))
    return tile(c64), tile(s1), tile(s2)


def _qkv_weight(w):
    q_scale = HEAD_DIM ** -0.5 * math.log2(math.e)
    scale = jnp.concatenate([jnp.full((D_MODEL,), q_scale, F32), jnp.ones((2 * D_MODEL,), F32)])
    return (w.astype(F32) * scale[None, :]).astype(BF16)


def kernel(x, positions, norm_mix, norm_ffn, norm_final, s5_a_re, s5_a_im, s5_log_dt, s5_b_re, s5_b_im, s5_c_re, s5_c_im, s5_d, s5_w_glu, s5_b_glu, dil_w_qkv, dil_w_o, hgrn_w_in, hgrn_lower_bound, hgrn_norm, hgrn_w_o, moba_w_qkv, moba_w_o, ffn_w_gate_up, ffn_w_down):
    bsz, seq, d = x.shape
    depth = norm_mix.shape[0]
    n_mixers = 4
    t = bsz * seq
    rope = _rope_tables(positions)
    lb_w = jax.nn.softmax(hgrn_lower_bound.astype(F32), axis=0)
    lower_bounds = jnp.cumsum(lb_w, axis=0) - lb_w[0]
    norm_mix = norm_mix.astype(F32)
    norm_ffn = norm_ffn.astype(F32)
    wgu = ffn_w_gate_up.astype(BF16)
    wdn = ffn_w_down.astype(BF16)

    h = x.reshape(t, d).astype(F32)
    for layer in range(depth):
        mixer, j = layer % n_mixers, layer // n_mixers
        final_g = norm_final.astype(F32) if layer == depth - 1 else None
        mix = None
        if mixer == 0:
            h = _s5(h, norm_mix[layer], s5_a_re[j], s5_a_im[j], s5_log_dt[j], s5_b_re[j], s5_b_im[j],
                    s5_c_re[j], s5_c_im[j], s5_d[j], s5_w_glu[j], s5_b_glu[j], seq)
        elif mixer == 1:
            qkv = _proj(h, norm_mix[layer], _qkv_weight(dil_w_qkv[j]), rope, 2 * D_MODEL)
            mix = (_dilated(qkv, bsz, seq), dil_w_o[j].astype(BF16))
        elif mixer == 2:
            proj = _proj(h, norm_mix[layer], hgrn_w_in[j].astype(BF16))
            mix = (_hgrn(proj, lower_bounds[layer], hgrn_norm[j], bsz, seq), hgrn_w_o[j].astype(BF16))
        else:
            qkv = _proj(h, norm_mix[layer], _qkv_weight(moba_w_qkv[j]), rope, 2 * D_MODEL)
            mix = (_moba(qkv, bsz, seq), moba_w_o[j].astype(BF16))
        h = _ffn(h, norm_ffn[layer], wgu[layer], wdn[layer], mix=mix, final_g=final_g)
    return h.reshape(bsz, seq, d).astype(x.dtype)
```

```python
import functools
import math

import jax
import jax.numpy as jnp
from jax import lax
from jax.experimental import pallas as pl
from jax.experimental.pallas import tpu as pltpu

F32 = jnp.float32
BF16 = jnp.bfloat16

D_MODEL = 1024
D_FF = 2816
RMS_EPS = 1e-6
NEG = -1e30
BELOW_NEG = -3e38

HEAD_DIM = 64
ATT_HEADS = D_MODEL // HEAD_DIM
ROPE_DIM = HEAD_DIM // 4
ROPE_THETA = 500000.0
LANES = 128
SUBLANES = 8
HEADS_PER_TILE = LANES // HEAD_DIM

S5_GROUP = 16
S5_GROUPS = D_MODEL // S5_GROUP
S5_STATE = 64
S5_CH = S5_GROUPS * S5_STATE
S5_SLABS = D_MODEL // LANES
S5_SLAB_CH = S5_CH // S5_SLABS
S5_SCAN_SHIFTS = (2, 4)

DIL_PATTERNS = ((128, 1), (512, 4), (2048, 16))
DIL_BLOCK = 128

HGRN_HEAD_DIM = 128
HGRN_HEADS = D_MODEL // HGRN_HEAD_DIM
HGRN_CHUNK = 64
HGRN_SUB = 16

MOBA_BLOCK = 256
MOBA_TOPK = 3

VMEM_LIMIT = 56 * 1024 * 1024

NT_DIMS = (((1,), (1,)), ((), ()))
TN_DIMS = (((0,), (0,)), ((), ()))


def _cparams(*sem):
    return pltpu.CompilerParams(dimension_semantics=sem, vmem_limit_bytes=VMEM_LIMIT)


def _rms(x, g):
    return x * lax.rsqrt(jnp.mean(x * x, axis=-1, keepdims=True) + RMS_EPS) * g


def _silu(x):
    return x * jax.nn.sigmoid(x)


def _proj_kernel(*refs, n_out, tn, rope_cols, n_cls):
    if rope_cols:
        x_ref, g_ref, w_ref, c_ref, s1_ref, s2_ref, o_ref = refs
    else:
        x_ref, g_ref, w_ref, o_ref = refs
    tm = x_ref.shape[0]
    per = tm // n_cls
    xn = _rms(x_ref[...], g_ref[...]).astype(BF16)
    if n_cls > 1:
        dst = lax.broadcasted_iota(jnp.int32, (tm, tm), 0)
        src = lax.broadcasted_iota(jnp.int32, (tm, tm), 1)
        perm = (src == (dst % per) * n_cls + dst // per).astype(BF16)
        xn = jnp.dot(perm, xn, preferred_element_type=F32).astype(BF16)
    if rope_cols:
        cos, s1, s2 = c_ref[...], s1_ref[...], s2_ref[...]
    for c in range(n_out // tn):
        y = jnp.dot(xn, w_ref[:, c * tn:(c + 1) * tn], preferred_element_type=F32)
        if c * tn < rope_cols:
            parts = []
            for j in range(tn // LANES):
                yj = y[:, j * LANES:(j + 1) * LANES]
                half = ROPE_DIM // 2
                parts.append(yj * cos + pltpu.roll(yj, LANES - half, 1) * s1 + pltpu.roll(yj, half, 1) * s2)
            y = jnp.concatenate(parts, axis=1) if len(parts) > 1 else parts[0]
        if n_cls == 1:
            o_ref[:, c * tn:(c + 1) * tn] = y
        else:
            for r in range(n_cls):
                o_ref[r, :, c * tn:(c + 1) * tn] = y[r * per:(r + 1) * per]


def _proj(h, g, w, rope=None, rope_cols=0, class_major=None, tm=256, tn=256):
    t, d = h.shape
    n_out = w.shape[1]
    in_specs = [pl.BlockSpec((tm, d), lambda i: (i, 0)),
                pl.BlockSpec((1, d), lambda i: (0, 0)),
                pl.BlockSpec((d, n_out), lambda i: (0, 0))]
    args = [h, g.reshape(1, d), w]
    if rope_cols:
        in_specs += [pl.BlockSpec((tm, LANES), lambda i: (i, 0))] * 3
        if class_major is not None:
            n_cls = class_major[2]
            rope = [a.reshape(t // tm, tm // n_cls, n_cls, LANES).transpose(0, 2, 1, 3).reshape(t, LANES)
                    for a in rope]
        args += list(rope)
    if class_major is None:
        n_cls = 1
        out_spec = pl.BlockSpec((tm, n_out), lambda i: (i, 0))
        out_shape = jax.ShapeDtypeStruct((t, n_out), F32)
    else:
        bsz, seq, n_cls = class_major
        tiles = seq // tm
        per = tm // n_cls
        assert seq % tm == 0 and per % SUBLANES == 0
        out_spec = pl.BlockSpec((None, n_cls, per, n_out), lambda i: (i // tiles, 0, i % tiles, 0))
        out_shape = jax.ShapeDtypeStruct((bsz, n_cls, seq // n_cls, n_out), F32)
    return pl.pallas_call(
        functools.partial(_proj_kernel, n_out=n_out, tn=tn, rope_cols=rope_cols, n_cls=n_cls),
        grid=(t // tm,),
        in_specs=in_specs,
        out_specs=out_spec,
        out_shape=out_shape,
        compiler_params=_cparams("parallel"),
        name="proj_rope" if rope_cols else "proj",
    )(*args)


def _ffn_kernel(*refs, has_mix, has_final, fc):
    refs = list(refs)
    h_ref, g_ref, wgu_ref, wd_ref = refs[:4]
    pos = 4
    if has_mix:
        a_ref, wo_ref = refs[pos:pos + 2]
        pos += 2
    if has_final:
        fg_ref = refs[pos]
        pos += 1
    o_ref, acc_ref = refs[pos], refs[pos + 1]

    h = h_ref[...]
    if has_mix:
        h = h + jnp.dot(a_ref[...], wo_ref[...], preferred_element_type=F32)
    xn = _rms(h, g_ref[...]).astype(BF16)

    def gate_up(c):
        gate = jnp.dot(xn, wgu_ref[:, c * fc:(c + 1) * fc], preferred_element_type=F32)
        up = jnp.dot(xn, wgu_ref[:, D_FF + c * fc:D_FF + (c + 1) * fc], preferred_element_type=F32)
        return gate, up

    n_chunks = D_FF // fc
    nxt = gate_up(0)
    for c in range(n_chunks):
        gate, up = nxt
        if c + 1 < n_chunks:
            nxt = gate_up(c + 1)
        act = (_silu(gate) * up).astype(BF16)
        contrib = jnp.dot(act, wd_ref[c * fc:(c + 1) * fc, :], preferred_element_type=F32)
        if c == 0:
            acc_ref[...] = contrib
        else:
            acc_ref[...] += contrib
    out = h + acc_ref[...]
    if has_final:
        out = _rms(out, fg_ref[...])
    o_ref[...] = out


def _ffn(h, g, wgu, wd, mix=None, final_g=None, tm=256, fc=256):
    t, d = h.shape
    const = lambda i: (0, 0)
    in_specs = [pl.BlockSpec((tm, d), lambda i: (i, 0)),
                pl.BlockSpec((1, d), const),
                pl.BlockSpec((d, 2 * D_FF), const),
                pl.BlockSpec((D_FF, d), const)]
    args = [h, g.reshape(1, d), wgu, wd]
    if mix is not None:
        a, wo = mix
        in_specs += [pl.BlockSpec((tm, d), lambda i: (i, 0)), pl.BlockSpec((d, d), const)]
        args += [a, wo]
    if final_g is not None:
        in_specs += [pl.BlockSpec((1, d), const)]
        args += [final_g.reshape(1, d)]
    return pl.pallas_call(
        functools.partial(_ffn_kernel, has_mix=mix is not None, has_final=final_g is not None, fc=fc),
        grid=(t // tm,),
        in_specs=in_specs,
        out_specs=pl.BlockSpec((tm, d), lambda i: (i, 0)),
        out_shape=jax.ShapeDtypeStruct((t, d), F32),
        scratch_shapes=[pltpu.VMEM((tm, d), F32)],
        compiler_params=_cparams("parallel"),
        name="ffn",
    )(*args)


def _s5_kernel(x_ref, g_ref, wbr_ref, wbi_ref, akr_ref, aki_ref, pr_ref, pi_ref, wcr_ref, wci_ref,
               d_ref, wglu_ref, bglu_ref, o_ref, er_ref, ei_ref, cr_ref, ci_ref, *, tm, tiles_per_seq):
    i = pl.program_id(0)

    @pl.when(i % tiles_per_seq == 0)
    def _():
        cr_ref[...] = jnp.zeros_like(cr_ref)
        ci_ref[...] = jnp.zeros_like(ci_ref)

    x = x_ref[...]
    u = _rms(x, g_ref[...])
    ub = u.astype(BF16)
    first = lax.broadcasted_iota(jnp.int32, (tm, 1), 0) % SUBLANES == 0
    ub_prev = jnp.where(first, 0.0, pltpu.roll(u, 1, 0)).astype(BF16)
    for s in range(S5_SLABS):
        us = jnp.concatenate([ub[:, s * LANES:(s + 1) * LANES], ub_prev[:, s * LANES:(s + 1) * LANES]], axis=1)
        er_ref[:, s * S5_SLAB_CH:(s + 1) * S5_SLAB_CH] = jnp.dot(us, wbr_ref[s], preferred_element_type=F32)
        ei_ref[:, s * S5_SLAB_CH:(s + 1) * S5_SLAB_CH] = jnp.dot(us, wbi_ref[s], preferred_element_type=F32)

    def group(j, carry):
        r0 = pl.multiple_of(j * SUBLANES, SUBLANES)
        xr = er_ref[pl.ds(r0, SUBLANES), :]
        xi = ei_ref[pl.ds(r0, SUBLANES), :]
        for lvl, k in enumerate(S5_SCAN_SHIFTS):
            sr = pltpu.roll(xr, k, 0)
            si = pltpu.roll(xi, k, 0)
            ar, ai = akr_ref[lvl], aki_ref[lvl]
            xr, xi = xr + ar * sr - ai * si, xi + ar * si + ai * sr
        cr, ci = cr_ref[...], ci_ref[...]
        pr, pi_ = pr_ref[...], pi_ref[...]
        hr = xr + pr * cr - pi_ * ci
        hi = xi + pr * ci + pi_ * cr
        er_ref[pl.ds(r0, SUBLANES), :] = hr
        ei_ref[pl.ds(r0, SUBLANES), :] = hi
        cr_ref[...] = jnp.broadcast_to(hr[SUBLANES - 1:SUBLANES, :], (SUBLANES, S5_CH))
        ci_ref[...] = jnp.broadcast_to(hi[SUBLANES - 1:SUBLANES, :], (SUBLANES, S5_CH))
        return carry

    lax.fori_loop(0, tm // SUBLANES, group, 0)

    ys = []
    for s in range(S5_SLABS):
        hr = er_ref[:, s * S5_SLAB_CH:(s + 1) * S5_SLAB_CH].astype(BF16)
        hi = ei_ref[:, s * S5_SLAB_CH:(s + 1) * S5_SLAB_CH].astype(BF16)
        ys.append(jnp.dot(hr, wcr_ref[s], preferred_element_type=F32)
                  - jnp.dot(hi, wci_ref[s], preferred_element_type=F32))
    y = jnp.concatenate(ys, axis=1) + d_ref[...] * u
    z = jax.nn.gelu(y).astype(BF16)
    zz = jnp.dot(z, wglu_ref[...], preferred_element_type=F32) + bglu_ref[...]
    o_ref[...] = x + zz[:, :D_MODEL] * jax.nn.sigmoid(zz[:, D_MODEL:])


def _block_diag_slabs(w):
    g, r, c = w.shape
    per = g // S5_SLABS
    w = w.reshape(S5_SLABS, per, r, c)
    eye = jnp.eye(per, dtype=w.dtype)
    return jnp.einsum('sgrc,gh->sgrhc', w, eye).reshape(S5_SLABS, per * r, per * c)


def _s5_tables(a_re, a_im, log_dt):
    lr, li = a_re.astype(F32), a_im.astype(F32)
    dt = jnp.exp(log_dt.astype(F32))[:, None]
    mag = jnp.exp(lr * dt)
    ab_re, ab_im = mag * jnp.cos(li * dt), mag * jnp.sin(li * dt)
    den = lr * lr + li * li
    m_re = ab_re - 1.0
    f_re = (m_re * lr + ab_im * li) / den
    f_im = (ab_im * lr - m_re * li) / den

    def power(k):
        return ((jnp.exp(lr * dt * k) * jnp.cos(li * dt * k)).reshape(-1),
                (jnp.exp(lr * dt * k) * jnp.sin(li * dt * k)).reshape(-1))

    rows = jnp.arange(SUBLANES)[:, None]
    akr, aki = [], []
    for k in S5_SCAN_SHIFTS:
        pr, pi_ = power(float(k))
        akr.append(jnp.where(rows >= k, pr[None, :], 0.0))
        aki.append(jnp.where(rows >= k, pi_[None, :], 0.0))
    pw = [power(float(k + 1)) for k in range(SUBLANES)]
    p_re = jnp.stack([p[0] for p in pw])
    p_im = jnp.stack([p[1] for p in pw])
    return f_re, f_im, ab_re, ab_im, jnp.stack(akr), jnp.stack(aki), p_re, p_im


def _s5(h, g, a_re, a_im, log_dt, b_re, b_im, c_re, c_im, d_skip, w_glu, b_glu, seq, tm=256):
    t, d = h.shape
    f_re, f_im, ab_re, ab_im, akr, aki, p_re, p_im = _s5_tables(a_re, a_im, log_dt)
    bw_re = f_re[:, :, None] * b_re - f_im[:, :, None] * b_im
    bw_im = f_re[:, :, None] * b_im + f_im[:, :, None] * b_re
    bw1_re = ab_re[:, :, None] * bw_re - ab_im[:, :, None] * bw_im
    bw1_im = ab_re[:, :, None] * bw_im + ab_im[:, :, None] * bw_re
    slabs = lambda w: _block_diag_slabs(jnp.swapaxes(w, 1, 2))
    wbr = jnp.concatenate([slabs(bw_re), slabs(bw1_re)], axis=1).astype(BF16)
    wbi = jnp.concatenate([slabs(bw_im), slabs(bw1_im)], axis=1).astype(BF16)
    wcr = _block_diag_slabs(jnp.swapaxes(c_re.astype(F32), 1, 2)).astype(BF16)
    wci = _block_diag_slabs(jnp.swapaxes(c_im.astype(F32), 1, 2)).astype(BF16)
    c2 = lambda i: (0, 0)
    c3 = lambda i: (0, 0, 0)
    return pl.pallas_call(
        functools.partial(_s5_kernel, tm=tm, tiles_per_seq=seq // tm),
        grid=(t // tm,),
        in_specs=[pl.BlockSpec((tm, d), lambda i: (i, 0)),
                  pl.BlockSpec((1, d), c2),
                  pl.BlockSpec(wbr.shape, c3), pl.BlockSpec(wbi.shape, c3),
                  pl.BlockSpec(akr.shape, c3), pl.BlockSpec(aki.shape, c3),
                  pl.BlockSpec(p_re.shape, c2), pl.BlockSpec(p_im.shape, c2),
                  pl.BlockSpec(wcr.shape, c3), pl.BlockSpec(wci.shape, c3),
                  pl.BlockSpec((1, d), c2),
                  pl.BlockSpec((d, 2 * d), c2),
                  pl.BlockSpec((1, 2 * d), c2)],
        out_specs=pl.BlockSpec((tm, d), lambda i: (i, 0)),
        out_shape=jax.ShapeDtypeStruct((t, d), F32),
        scratch_shapes=[pltpu.VMEM((tm, S5_CH), F32), pltpu.VMEM((tm, S5_CH), F32),
                        pltpu.VMEM((SUBLANES, S5_CH), F32), pltpu.VMEM((SUBLANES, S5_CH), F32)],
        compiler_params=_cparams("arbitrary"),
        name="s5",
    )(h, g.reshape(1, d), wbr, wbi, akr, aki, p_re, p_im, wcr, wci,
      d_skip.astype(F32).reshape(1, d), w_glu.astype(BF16), b_glu.astype(F32).reshape(1, 2 * d))


DIL_CLASSES = 16
DIL_STEPS = 2


def _dil_kernel(*refs, seq):
    ncls = DIL_CLASSES
    u_len = seq // ncls
    blk = DIL_BLOCK
    qp_ref, kp_ref, vp_ref, o_ref = refs[:4]
    acc_ref, lsw_ref, m_ref, msw_ref = refs[4:8]
    n_chain = DIL_STEPS * HEADS_PER_TILE
    bufs = refs[8:]
    buf_a, buf_b = bufs[:n_chain], bufs[n_chain:]

    lane = lax.broadcasted_iota(jnp.int32, (1, LANES), 1)
    low = lane < HEAD_DIM
    n_iter = seq // blk // DIL_STEPS

    for pat, (window, dil) in enumerate(DIL_PATTERNS):
        assert window // dil == blk and ncls % dil == 0 and seq % (dil * blk) == 0
        n_run = ncls // dil
        run = SUBLANES * dil
        n_blk = u_len // run
        jq = lax.broadcasted_iota(jnp.int32, (blk, blk), 0)
        jk = lax.broadcasted_iota(jnp.int32, (blk, blk), 1)
        wq = (jq % run) * n_run + jq // run
        wk = (jk % run) * n_run + jk // run
        mask_cur = wk <= wq
        mask_prev = wk >= wq

        def run_rows(step, dil=dil, n_run=n_run, run=run, n_blk=n_blk):
            res = step // n_blk
            bi = step % n_blk
            base = [(res + dil * c) * u_len for c in range(n_run)]
            cur = [pl.multiple_of(b + run * bi, SUBLANES) for b in base]
            prev = [pl.multiple_of(b + run * jnp.maximum(bi - 1, 0), SUBLANES) for b in base]
            return cur, prev, bi > 0

        def gather(ref, offs, run=run):
            parts = [ref[pl.ds(o, run), :] for o in offs]
            return jnp.concatenate(parts, axis=0) if len(parts) > 1 else parts[0]

        def scores(step, hh):
            cur, prev, _ = run_rows(step)
            q = gather(qp_ref, cur)
            q = jnp.where(low if hh == 0 else jnp.logical_not(low), q, 0.0).astype(BF16)
            kcat = jnp.concatenate([gather(kp_ref, prev), gather(kp_ref, cur)], axis=0).astype(BF16)
            return lax.dot_general(q, kcat, NT_DIMS, preferred_element_type=F32)

        def phase(it, bufs, nxt_bufs, pat=pat):
            nxt = jnp.minimum(it + 1, n_iter - 1)
            ahead = [(t, hh) for t in range(DIL_STEPS) for hh in range(HEADS_PER_TILE)]

            def stage(i):
                for t2, hh2 in ahead[i:i + 1]:
                    nxt_bufs[t2 * HEADS_PER_TILE + hh2][...] = scores(nxt * DIL_STEPS + t2, hh2)

            stage(0)
            stage(1)
            for t in range(DIL_STEPS):
                step = it * DIL_STEPS + t
                cur, prev, has_prev = run_rows(step)
                vcat = jnp.concatenate([gather(vp_ref, prev), gather(vp_ref, cur)], axis=0)
                mask = jnp.concatenate([jnp.logical_and(mask_prev, has_prev), mask_cur], axis=1)
                ms, os_ = [], []
                for hh in range(HEADS_PER_TILE):
                    c = t * HEADS_PER_TILE + hh
                    s = jnp.where(mask, bufs[c][...], NEG)
                    mb = jnp.max(s, axis=1, keepdims=True)
                    p = jnp.exp2(s - mb).astype(BF16)
                    vh = jnp.where(low if hh == 0 else jnp.logical_not(low), vcat, 1.0).astype(BF16)
                    os_.append(jnp.dot(p, vh, preferred_element_type=F32))
                    ms.append(jnp.broadcast_to(mb, (blk, LANES)))
                    stage(c + 2)
                m_b = jnp.where(low, ms[0], ms[1])
                msw_b = jnp.where(low, ms[1], ms[0])
                o_b = jnp.where(low, os_[0], os_[1])
                lsw_b = jnp.where(low, os_[1], os_[0])
                for ci, off in enumerate(cur):
                    rows = pl.ds(off, run)
                    piece = slice(ci * run, (ci + 1) * run)
                    if pat == 0:
                        m_ref[rows, :] = m_b[piece]
                        msw_ref[rows, :] = msw_b[piece]
                        acc_ref[rows, :] = o_b[piece]
                        lsw_ref[rows, :] = lsw_b[piece]
                    else:
                        m_o, msw_o = m_ref[rows, :], msw_ref[rows, :]
                        m_n = jnp.maximum(m_o, m_b[piece])
                        msw_n = jnp.maximum(msw_o, msw_b[piece])
                        m_ref[rows, :] = m_n
                        msw_ref[rows, :] = msw_n
                        acc_ref[rows, :] = (acc_ref[rows, :] * jnp.exp2(m_o - m_n)
                                            + o_b[piece] * jnp.exp2(m_b[piece] - m_n))
                        lsw_ref[rows, :] = (lsw_ref[rows, :] * jnp.exp2(msw_o - msw_n)
                                            + lsw_b[piece] * jnp.exp2(msw_b[piece] - msw_n))

        def two_phases(i, carry, phase=phase):
            phase(2 * i, buf_a, buf_b)
            phase(2 * i + 1, buf_b, buf_a)
            return carry

        for t in range(DIL_STEPS):
            for hh in range(HEADS_PER_TILE):
                buf_a[t * HEADS_PER_TILE + hh][...] = scores(t, hh)
        lax.fori_loop(0, n_iter // 2, two_phases, 0)

    def finish(c, carry):
        r0 = pl.multiple_of(c * 256, 256)
        den = pltpu.roll(lsw_ref[pl.ds(r0, 256), :], HEAD_DIM, 1)
        o_ref[pl.ds(r0, 256), :] = (acc_ref[pl.ds(r0, 256), :] / den).astype(o_ref.dtype)
        return carry

    lax.fori_loop(0, seq // 256, finish, 0)


def _dilated(qkv, bsz, seq):
    ncls = DIL_CLASSES
    assert seq % (ncls * DIL_BLOCK) == 0 and (seq // DIL_BLOCK) % (2 * DIL_STEPS) == 0 and seq % 256 == 0
    u_len = seq // ncls
    n_tiles = D_MODEL // LANES
    qkv3 = qkv.reshape(bsz, seq, 3 * D_MODEL)
    spec = lambda off: pl.BlockSpec((None, seq, LANES), lambda b, hp, off=off: (b, 0, off + hp))
    in_specs = [spec(off) for off in (0, n_tiles, 2 * n_tiles)]
    out = pl.pallas_call(
        functools.partial(_dil_kernel, seq=seq),
        grid=(bsz, n_tiles),
        in_specs=in_specs,
        out_specs=pl.BlockSpec((None, seq, LANES), lambda b, hp: (b, 0, hp)),
        out_shape=jax.ShapeDtypeStruct((bsz, seq, D_MODEL), BF16),
        scratch_shapes=[pltpu.VMEM((seq, LANES), F32)] * 4
                       + [pltpu.VMEM((DIL_BLOCK, 2 * DIL_BLOCK), F32)] * (2 * DIL_STEPS * HEADS_PER_TILE),
        compiler_params=_cparams("parallel", "parallel"),
        name="dilated_attn",
    )(*([qkv3] * len(in_specs)))
    out = out.reshape(bsz, ncls, u_len, D_MODEL).transpose(0, 2, 1, 3)
    return out.reshape(bsz * seq, D_MODEL)


def _hgrn_kernel(q_ref, f_ref, i_ref, g_ref, lb_ref, ng_ref, o_ref, st_ref, *, tc):
    @pl.when(pl.program_id(2) == 0)
    def _():
        st_ref[...] = jnp.zeros_like(st_ref)

    chunk, sub = HGRN_CHUNK, HGRN_SUB
    n_ch = tc // chunk
    n_sub = chunk // sub
    r = lax.broadcasted_iota(jnp.int32, (chunk, chunk), 0)
    c = lax.broadcasted_iota(jnp.int32, (chunk, chunk), 1)
    tril = (c <= r).astype(F32)
    trow = lax.broadcasted_iota(jnp.int32, (sub, 1), 0)
    lb = lb_ref[...]
    rows = [slice(ch * chunk, (ch + 1) * chunk) for ch in range(n_ch)]

    qs = [_silu(q_ref[rw, :]) for rw in rows]
    fg = [lb + (1.0 - lb) * jax.nn.sigmoid(f_ref[rw, :]) for rw in rows]
    kk = [1.0 - x for x in fg]
    iv = [i_ref[rw, :] for rw in rows]
    ivb = [x.astype(BF16) for x in iv]
    bc = [jnp.dot(tril, jnp.log(x), preferred_element_type=F32, precision=lax.Precision.HIGHEST) for x in fg]
    bl = [x[chunk - 1:chunk] for x in bc]

    upd = [lax.dot_general(ivb[ch], (kk[ch] * jnp.exp(bl[ch] - bc[ch])).astype(BF16), TN_DIMS,
                           preferred_element_type=F32) for ch in range(n_ch)]
    att = []
    for ch in range(n_ch):
        row_att = []
        for blk in range(1, n_sub):
            lo = blk * sub
            r_i = bc[ch][lo - 1:lo]
            qt = (qs[ch][lo:lo + sub] * jnp.exp(bc[ch][lo:lo + sub] - r_i)).astype(BF16)
            kt = (kk[ch][:lo] * jnp.exp(r_i - bc[ch][:lo])).astype(BF16)
            row_att.append(lax.dot_general(qt, kt, NT_DIMS, preferred_element_type=F32).astype(BF16))
        att.append(row_att)

    st_t = st_ref[...]
    states = []
    for ch in range(n_ch):
        states.append(st_t.astype(BF16))
        st_t = st_t * jnp.exp(bl[ch]) + upd[ch]
    st_ref[...] = st_t

    o_inter = [lax.dot_general((qs[ch] * jnp.exp(bc[ch])).astype(BF16), states[ch], NT_DIMS,
                               preferred_element_type=F32) for ch in range(n_ch)]
    o_off = [[jnp.dot(att[ch][blk - 1], ivb[ch][:blk * sub], preferred_element_type=F32)
              for blk in range(1, n_sub)] for ch in range(n_ch)]

    for ch in range(n_ch):
        outs = []
        for blk in range(n_sub):
            lo = blk * sub
            o_i = o_inter[ch][lo:lo + sub]
            if blk > 0:
                o_i = o_i + o_off[ch][blk - 1]
            q_i, k_i, iv_i, fg_i = (x[ch][lo:lo + sub] for x in (qs, kk, iv, fg))
            u = jnp.zeros_like(q_i)
            for s in reversed(range(sub)):
                decayed = u * fg_i[s + 1:s + 2] if s + 1 < sub else u
                u = jnp.where(trow == s, q_i, decayed)
                a_col = jnp.sum(u * k_i[s:s + 1], axis=1, keepdims=True)
                o_i = o_i + a_col * iv_i[s:s + 1]
            outs.append(o_i)
        o = jnp.concatenate(outs, axis=0)
        o = o * lax.rsqrt(jnp.mean(o * o, axis=-1, keepdims=True) + RMS_EPS) * ng_ref[...]
        o_ref[rows[ch], :] = (o * _silu(g_ref[rows[ch], :])).astype(o_ref.dtype)


def _hgrn(proj, lb, norm_g, bsz, seq, tc=512):
    t = bsz * seq
    tiles = seq // tc
    hd = HGRN_HEAD_DIM
    spec = lambda off: pl.BlockSpec((tc, hd), lambda b, h, j, off=off: (b * tiles + j, off + h))
    return pl.pallas_call(
        functools.partial(_hgrn_kernel, tc=tc),
        grid=(bsz, HGRN_HEADS, tiles),
        in_specs=[spec(0), spec(HGRN_HEADS), spec(2 * HGRN_HEADS), spec(3 * HGRN_HEADS),
                  pl.BlockSpec((1, hd), lambda b, h, j: (0, h)),
                  pl.BlockSpec((1, hd), lambda b, h, j: (0, 0))],
        out_specs=spec(0),
        out_shape=jax.ShapeDtypeStruct((t, D_MODEL), BF16),
        scratch_shapes=[pltpu.VMEM((hd, hd), F32)],
        compiler_params=_cparams("parallel", "parallel", "arbitrary"),
        name="hgrn2",
    )(proj, proj, proj, proj, lb.reshape(1, D_MODEL), norm_g.astype(F32).reshape(1, hd))


MOBA_SUPER = 2
MOBA_Q_TILES = 4
MOBA_VT_ROWS = HEAD_DIM + 16


def _moba_kernel(q_ref, k_ref, v_ref, o_ref, km_ref, kp_ref, vt_ref, qp_ref, acc_ref, *s_refs, n_blk, n_sel):
    blk = MOBA_BLOCK
    sbk = MOBA_SUPER * blk
    lane = lax.broadcasted_iota(jnp.int32, (1, LANES), 1)
    head_of_lane = lane // HEAD_DIM
    hot_lane0 = [(1 - hh) * HEAD_DIM for hh in range(HEADS_PER_TILE)]
    n_rows = HEADS_PER_TILE * n_blk

    ones_row = (lax.broadcasted_iota(jnp.int32, (MOBA_VT_ROWS - HEAD_DIM, 1), 0) == 0).astype(BF16)
    for hh in range(HEADS_PER_TILE):
        vt_ref[hh * MOBA_VT_ROWS + HEAD_DIM:(hh + 1) * MOBA_VT_ROWS, :] = jnp.broadcast_to(
            ones_row, (MOBA_VT_ROWS - HEAD_DIM, n_blk * blk))
    for n in range(n_blk):
        rows = slice(n * blk, (n + 1) * blk)
        kb = k_ref[rows, :]
        v_t = v_ref[rows, :].T
        for hh in range(HEADS_PER_TILE):
            vt_ref[hh * MOBA_VT_ROWS:hh * MOBA_VT_ROWS + HEAD_DIM, rows] = (
                v_t[hh * HEAD_DIM:(hh + 1) * HEAD_DIM].astype(BF16))
        km = jnp.sum(kb, axis=0, keepdims=True) * (1.0 / blk)
        for hh in range(HEADS_PER_TILE):
            mine = head_of_lane == hh
            km_ref[hh * n_blk + n:hh * n_blk + n + 1, :] = jnp.where(mine, km, 0.0)
            hot = (lane == hot_lane0[hh] + n).astype(F32)
            kp_ref[rows, hh * LANES:(hh + 1) * LANES] = jnp.where(mine, kb, hot).astype(BF16)

    prow = lax.broadcasted_iota(jnp.int32, (n_rows, HEADS_PER_TILE * LANES), 0)
    pcol = lax.broadcasted_iota(jnp.int32, (n_rows, HEADS_PER_TILE * LANES), 1)
    target = jnp.zeros_like(prow)
    for hh in range(HEADS_PER_TILE):
        target = jnp.where(prow // n_blk == hh, hh * LANES + hot_lane0[hh] + prow % n_blk, target)
    place = (pcol == target).astype(BF16)
    nid = lax.broadcasted_iota(jnp.int32, (n_blk, blk), 0)
    nid_f = nid.astype(F32)
    km_all = km_ref[...]

    q_rows = [slice(qi * blk, (qi + 1) * blk) for qi in range(n_blk)]
    gates = [lax.dot_general(km_all, q_ref[rw, :], NT_DIMS, preferred_element_type=F32,
                             precision=lax.Precision.HIGHEST) for rw in q_rows]
    biases = []
    for qi in range(n_blk):
        past = nid < qi
        bias_rows = []
        for hh in range(HEADS_PER_TILE):
            g = jnp.where(past, gates[qi][hh * n_blk:(hh + 1) * n_blk], NEG)
            keep = nid == qi
            for _ in range(n_sel):
                mx = jnp.max(g, axis=0, keepdims=True)
                idx = jnp.min(jnp.where(g == mx, nid_f, float(n_blk)), axis=0, keepdims=True)
                pick = nid_f == idx
                keep = jnp.logical_or(keep, jnp.logical_and(pick, past))
                g = jnp.where(pick, BELOW_NEG, g)
            bias_rows.append(jnp.where(keep, 0.0, NEG))
        biases.append(jnp.concatenate(bias_rows, axis=0).astype(BF16))
    bias_qs = [lax.dot_general(b, place, TN_DIMS, preferred_element_type=F32) for b in biases]
    for qi in range(n_blk):
        q = q_ref[q_rows[qi], :]
        for hh in range(HEADS_PER_TILE):
            qp_ref[q_rows[qi], hh * LANES:(hh + 1) * LANES] = jnp.where(
                head_of_lane == hh, q, bias_qs[qi][:, hh * LANES:(hh + 1) * LANES]).astype(BF16)

    krow = lax.broadcasted_iota(jnp.int32, (sbk, blk), 0)
    qcol = lax.broadcasted_iota(jnp.int32, (sbk, blk), 1)
    rel = qcol - krow

    n_qt = MOBA_Q_TILES
    assert n_qt == 2 * MOBA_SUPER
    n_chain = n_qt * HEADS_PER_TILE
    buf_a, buf_b = s_refs[:n_chain], s_refs[n_chain:]
    all_tiles = tuple(range(n_qt))

    def q_operands(g):
        q0 = pl.multiple_of(g * n_qt * blk, n_qt * blk)
        return [[qp_ref[pl.ds(q0 + t * blk, blk), hh * LANES:(hh + 1) * LANES] for hh in range(HEADS_PER_TILE)]
                for t in range(n_qt)]

    def stage(bufs, k0, qps, t, hh):
        kpn = kp_ref[pl.ds(k0, sbk), hh * LANES:(hh + 1) * LANES]
        bufs[t * HEADS_PER_TILE + hh][...] = lax.dot_general(kpn, qps[t][hh], NT_DIMS, preferred_element_type=F32)

    n_groups = n_blk // n_qt

    def q_group(g, carry):
        q0 = pl.multiple_of(g * n_qt * blk, n_qt * blk)
        qps = q_operands(g)
        acc_ref[...] = jnp.zeros_like(acc_ref)

        def step(stats, cur, k0, tiles, causal, nxt=None, nxt_tiles=(), nxt_k0=None, nxt_q=qps):
            chains = [(t, hh) for t in tiles for hh in range(HEADS_PER_TILE)]
            ahead = [(t, hh) for t in nxt_tiles for hh in range(HEADS_PER_TILE)]
            nxt_k0 = k0 + sbk if nxt_k0 is None else nxt_k0
            v_t = [vt_ref[hh * MOBA_VT_ROWS:(hh + 1) * MOBA_VT_ROWS, pl.ds(k0, sbk)] for hh in range(HEADS_PER_TILE)]
            stats = list(stats)
            for t, hh in ahead[:2]:
                stage(nxt, nxt_k0, nxt_q, t, hh)
            for i, (t, hh) in enumerate(chains):
                c = t * HEADS_PER_TILE + hh
                m_o = stats[c]
                s_t = cur[c][...]
                if t in causal:
                    s_t = jnp.where(rel >= causal[t], s_t, NEG)
                m_n = jnp.maximum(m_o, jnp.max(s_t, axis=0, keepdims=True))
                p = jnp.exp2(s_t - m_n).astype(BF16)
                acc_ref[c] = jnp.exp2(m_o - m_n) * acc_ref[c] + jnp.dot(v_t[hh], p, preferred_element_type=F32)
                stats[c] = m_n
                for t2, hh2 in ahead[i + 2:i + 3]:
                    stage(nxt, nxt_k0, nxt_q, t2, hh2)
            for t2, hh2 in ahead[len(chains) + 2:]:
                stage(nxt, nxt_k0, nxt_q, t2, hh2)
            return tuple(stats)

        def pair(i, stats):
            k0 = pl.multiple_of(i * 2 * sbk, 2 * sbk)
            stats = step(stats, buf_a, k0, all_tiles, {}, buf_b, all_tiles)
            return step(stats, buf_b, k0 + sbk, all_tiles, {}, buf_a, all_tiles)

        init = (jnp.full((1, blk), BELOW_NEG, F32),) * n_chain
        stats = lax.fori_loop(0, g, pair, init)
        late = all_tiles[MOBA_SUPER:]
        stats = step(stats, buf_a, q0, all_tiles, {t: -t * blk for t in all_tiles[:MOBA_SUPER]}, buf_b, late)
        step(stats, buf_b, q0 + sbk, late, {t: -(t - MOBA_SUPER) * blk for t in late},
             buf_a, all_tiles, 0, q_operands(jnp.minimum(g + 1, n_groups - 1)))
        for t in all_tiles:
            accs = [acc_ref[t * HEADS_PER_TILE + hh] for hh in range(HEADS_PER_TILE)]
            o_t = jnp.concatenate([a[:HEAD_DIM] / a[HEAD_DIM:HEAD_DIM + 1] for a in accs], axis=0)
            o_ref[pl.ds(q0 + t * blk, blk), :] = o_t.T.astype(o_ref.dtype)
        return carry

    first_q = q_operands(0)
    for t in all_tiles:
        for hh in range(HEADS_PER_TILE):
            stage(buf_a, 0, first_q, t, hh)
    lax.fori_loop(0, n_groups, q_group, 0)


def _moba(qkv, bsz, seq):
    blk = MOBA_BLOCK
    assert seq % (blk * MOBA_Q_TILES) == 0
    n_chain = MOBA_Q_TILES * HEADS_PER_TILE
    n_blk = seq // blk
    n_sel = min(MOBA_TOPK, n_blk - 1)
    n_tiles = D_MODEL // LANES
    qkv3 = qkv.reshape(bsz, seq, 3 * D_MODEL)
    spec = lambda off: pl.BlockSpec((None, seq, LANES), lambda b, hp, off=off: (b, 0, off + hp))
    out = pl.pallas_call(
        functools.partial(_moba_kernel, n_blk=n_blk, n_sel=n_sel),
        grid=(bsz, n_tiles),
        in_specs=[spec(0), spec(n_tiles), spec(2 * n_tiles)],
        out_specs=pl.BlockSpec((None, seq, LANES), lambda b, hp: (b, 0, hp)),
        out_shape=jax.ShapeDtypeStruct((bsz, seq, D_MODEL), BF16),
        scratch_shapes=[pltpu.VMEM((HEADS_PER_TILE * n_blk, LANES), F32),
                        pltpu.VMEM((seq, HEADS_PER_TILE * LANES), BF16),
                        pltpu.VMEM((HEADS_PER_TILE * MOBA_VT_ROWS, seq), BF16),
                        pltpu.VMEM((seq, HEADS_PER_TILE * LANES), BF16),
                        pltpu.VMEM((n_chain, MOBA_VT_ROWS, blk), F32)]
                       + [pltpu.VMEM((MOBA_SUPER * blk, blk), F32)] * (2 * n_chain),
        compiler_params=_cparams("parallel", "parallel"),
        name="moba_attn",
    )(qkv3, qkv3, qkv3)
    return out.reshape(bsz * seq, D_MODEL)


def _rope_tables(positions):
    half = ROPE_DIM // 2
    inv = ROPE_THETA ** (-jnp.arange(0, ROPE_DIM, 2, dtype=F32) / ROPE_DIM)
    ang = positions.astype(F32).reshape(-1)[:, None] * inv
    cos, sin = jnp.cos(ang), jnp.sin(ang)
    t = cos.shape[0]
    rest = HEAD_DIM - ROPE_DIM
    c64 = jnp.concatenate([cos, cos, jnp.ones((t, rest), F32)], axis=1)
    s1 = jnp.concatenate([-sin, jnp.zeros((t, HEAD_DIM - half), F32)], axis=1)
    s2 = jnp.concatenate([jnp.zeros((t, half), F32), sin, jnp.zeros((t, rest), F32)], axis=1)
    tile = lambda a: jnp.tile(a, (1, HEADS_PER_TILE))
    return tile(c64), tile(s1), tile(s2)


def _qkv_weight(w):
    q_scale = HEAD_DIM ** -0.5 * math.log2(math.e)
    scale = jnp.concatenate([jnp.full((D_MODEL,), q_scale, F32), jnp.ones((2 * D_MODEL,), F32)])
    return (w.astype(F32) * scale[None, :]).astype(BF16)


def kernel(x, positions, norm_mix, norm_ffn, norm_final, s5_a_re, s5_a_im, s5_log_dt, s5_b_re, s5_b_im, s5_c_re, s5_c_im, s5_d, s5_w_glu, s5_b_glu, dil_w_qkv, dil_w_o, hgrn_w_in, hgrn_lower_bound, hgrn_norm, hgrn_w_o, moba_w_qkv, moba_w_o, ffn_w_gate_up, ffn_w_down):
    bsz, seq, d = x.shape
    depth = norm_mix.shape[0]
    n_mixers = 4
    t = bsz * seq
    rope = _rope_tables(positions)
    lb_w = jax.nn.softmax(hgrn_lower_bound.astype(F32), axis=0)
    lower_bounds = jnp.cumsum(lb_w, axis=0) - lb_w[0]
    norm_mix = norm_mix.astype(F32)
    norm_ffn = norm_ffn.astype(F32)

    h = x.reshape(t, d).astype(F32)
    for layer in range(depth):
        mixer, j = layer % n_mixers, layer // n_mixers
        final_g = norm_final.astype(F32) if layer == depth - 1 else None
        mix = None
        if mixer == 0:
            h = _s5(h, norm_mix[layer], s5_a_re[j], s5_a_im[j], s5_log_dt[j], s5_b_re[j], s5_b_im[j],
                    s5_c_re[j], s5_c_im[j], s5_d[j], s5_w_glu[j], s5_b_glu[j], seq)
        elif mixer == 1:
            qkv = _proj(h, norm_mix[layer], _qkv_weight(dil_w_qkv[j]), rope, 2 * D_MODEL,
                        class_major=(bsz, seq, DIL_CLASSES))
            mix = (_dilated(qkv, bsz, seq), dil_w_o[j].astype(BF16))
        elif mixer == 2:
            proj = _proj(h, norm_mix[layer], hgrn_w_in[j].astype(BF16))
            mix = (_hgrn(proj, lower_bounds[layer], hgrn_norm[j], bsz, seq), hgrn_w_o[j].astype(BF16))
        else:
            qkv = _proj(h, norm_mix[layer], _qkv_weight(moba_w_qkv[j]), rope, 2 * D_MODEL)
            mix = (_moba(qkv, bsz, seq), moba_w_o[j].astype(BF16))
        h = _ffn(h, norm_ffn[layer], ffn_w_gate_up[layer].astype(BF16), ffn_w_down[layer].astype(BF16),
                 mix=mix, final_g=final_g)
    return h.reshape(bsz, seq, d).astype(x.dtype)
```

```python
import functools
import math

import jax
import jax.numpy as jnp
from jax import lax
from jax.experimental import pallas as pl
from jax.experimental.pallas import tpu as pltpu

F32 = jnp.float32
BF16 = jnp.bfloat16

D_MODEL = 1024
D_FF = 2816
RMS_EPS = 1e-6
NEG = -1e30
BELOW_NEG = -3e38

HEAD_DIM = 64
ATT_HEADS = D_MODEL // HEAD_DIM
ROPE_DIM = HEAD_DIM // 4
ROPE_THETA = 500000.0
LANES = 128
SUBLANES = 8
HEADS_PER_TILE = LANES // HEAD_DIM

S5_GROUP = 16
S5_GROUPS = D_MODEL // S5_GROUP
S5_STATE = 64
S5_CH = S5_GROUPS * S5_STATE
S5_SLABS = D_MODEL // LANES
S5_SLAB_CH = S5_CH // S5_SLABS
S5_SCAN_SHIFTS = (2, 4)

DIL_PATTERNS = ((128, 1), (512, 4), (2048, 16))
DIL_BLOCK = 128

HGRN_HEAD_DIM = 128
HGRN_HEADS = D_MODEL // HGRN_HEAD_DIM
HGRN_CHUNK = 64
HGRN_SUB = 16

MOBA_BLOCK = 256
MOBA_TOPK = 3

VMEM_LIMIT = 56 * 1024 * 1024

NT_DIMS = (((1,), (1,)), ((), ()))
TN_DIMS = (((0,), (0,)), ((), ()))


def _cparams(*sem):
    return pltpu.CompilerParams(dimension_semantics=sem, vmem_limit_bytes=VMEM_LIMIT)


def _rms(x, g):
    return x * lax.rsqrt(jnp.mean(x * x, axis=-1, keepdims=True) + RMS_EPS) * g


def _silu(x):
    return x * jax.nn.sigmoid(x)


def _proj_kernel(*refs, n_out, tn, rope_cols, n_cls):
    if rope_cols:
        x_ref, g_ref, w_ref, c_ref, s1_ref, s2_ref, o_ref = refs
    else:
        x_ref, g_ref, w_ref, o_ref = refs
    tm = x_ref.shape[0]
    per = tm // n_cls
    xn = _rms(x_ref[...], g_ref[...]).astype(BF16)
    if n_cls > 1:
        dst = lax.broadcasted_iota(jnp.int32, (tm, tm), 0)
        src = lax.broadcasted_iota(jnp.int32, (tm, tm), 1)
        perm = (src == (dst % per) * n_cls + dst // per).astype(BF16)
        xn = jnp.dot(perm, xn, preferred_element_type=F32).astype(BF16)
    if rope_cols:
        cos, s1, s2 = c_ref[...], s1_ref[...], s2_ref[...]
    for c in range(n_out // tn):
        y = jnp.dot(xn, w_ref[:, c * tn:(c + 1) * tn], preferred_element_type=F32)
        if c * tn < rope_cols:
            parts = []
            for j in range(tn // LANES):
                yj = y[:, j * LANES:(j + 1) * LANES]
                half = ROPE_DIM // 2
                parts.append(yj * cos + pltpu.roll(yj, LANES - half, 1) * s1 + pltpu.roll(yj, half, 1) * s2)
            y = jnp.concatenate(parts, axis=1) if len(parts) > 1 else parts[0]
        if n_cls == 1:
            o_ref[:, c * tn:(c + 1) * tn] = y
        else:
            for r in range(n_cls):
                o_ref[r, :, c * tn:(c + 1) * tn] = y[r * per:(r + 1) * per]


def _proj(h, g, w, rope=None, rope_cols=0, class_major=None, tm=256, tn=256):
    t, d = h.shape
    n_out = w.shape[1]
    in_specs = [pl.BlockSpec((tm, d), lambda i: (i, 0)),
                pl.BlockSpec((1, d), lambda i: (0, 0)),
                pl.BlockSpec((d, n_out), lambda i: (0, 0))]
    args = [h, g.reshape(1, d), w]
    if rope_cols:
        in_specs += [pl.BlockSpec((tm, LANES), lambda i: (i, 0))] * 3
        args += list(rope)
    if class_major is None:
        n_cls = 1
        out_spec = pl.BlockSpec((tm, n_out), lambda i: (i, 0))
        out_shape = jax.ShapeDtypeStruct((t, n_out), F32)
    else:
        bsz, seq, n_cls = class_major
        tiles = seq // tm
        per = tm // n_cls
        assert seq % tm == 0 and per % SUBLANES == 0
        out_spec = pl.BlockSpec((None, n_cls, per, n_out), lambda i: (i // tiles, 0, i % tiles, 0))
        out_shape = jax.ShapeDtypeStruct((bsz, n_cls, seq // n_cls, n_out), F32)
    return pl.pallas_call(
        functools.partial(_proj_kernel, n_out=n_out, tn=tn, rope_cols=rope_cols, n_cls=n_cls),
        grid=(t // tm,),
        in_specs=in_specs,
        out_specs=out_spec,
        out_shape=out_shape,
        compiler_params=_cparams("parallel"),
        name="proj_rope" if rope_cols else "proj",
    )(*args)


def _ffn_kernel(*refs, has_mix, has_final, fc):
    refs = list(refs)
    h_ref, g_ref, wgu_ref, wd_ref = refs[:4]
    pos = 4
    if has_mix:
        a_ref, wo_ref = refs[pos:pos + 2]
        pos += 2
    if has_final:
        fg_ref = refs[pos]
        pos += 1
    o_ref, acc_ref = refs[pos], refs[pos + 1]

    h = h_ref[...]
    if has_mix:
        h = h + jnp.dot(a_ref[...], wo_ref[...], preferred_element_type=F32)
    xn = _rms(h, g_ref[...]).astype(BF16)

    def gate_up(c):
        gate = jnp.dot(xn, wgu_ref[:, c * fc:(c + 1) * fc], preferred_element_type=F32)
        up = jnp.dot(xn, wgu_ref[:, D_FF + c * fc:D_FF + (c + 1) * fc], preferred_element_type=F32)
        return gate, up

    n_chunks = D_FF // fc
    nxt = gate_up(0)
    for c in range(n_chunks):
        gate, up = nxt
        if c + 1 < n_chunks:
            nxt = gate_up(c + 1)
        act = (_silu(gate) * up).astype(BF16)
        contrib = jnp.dot(act, wd_ref[c * fc:(c + 1) * fc, :], preferred_element_type=F32)
        if c == 0:
            acc_ref[...] = contrib
        else:
            acc_ref[...] += contrib
    out = h + acc_ref[...]
    if has_final:
        out = _rms(out, fg_ref[...])
    o_ref[...] = out


def _ffn(h, g, wgu, wd, layer, mix=None, final_g=None, tm=512, fc=256):
    t, d = h.shape
    const = lambda i: (0, 0)
    in_specs = [pl.BlockSpec((tm, d), lambda i: (i, 0)),
                pl.BlockSpec((1, d), const),
                pl.BlockSpec((None, d, 2 * D_FF), lambda i: (layer, 0, 0)),
                pl.BlockSpec((None, D_FF, d), lambda i: (layer, 0, 0))]
    args = [h, g.reshape(1, d), wgu, wd]
    if mix is not None:
        a, wo = mix
        in_specs += [pl.BlockSpec((tm, d), lambda i: (i, 0)), pl.BlockSpec((d, d), const)]
        args += [a, wo]
    if final_g is not None:
        in_specs += [pl.BlockSpec((1, d), const)]
        args += [final_g.reshape(1, d)]
    return pl.pallas_call(
        functools.partial(_ffn_kernel, has_mix=mix is not None, has_final=final_g is not None, fc=fc),
        grid=(t // tm,),
        in_specs=in_specs,
        out_specs=pl.BlockSpec((tm, d), lambda i: (i, 0)),
        out_shape=jax.ShapeDtypeStruct((t, d), F32),
        scratch_shapes=[pltpu.VMEM((tm, d), F32)],
        compiler_params=_cparams("parallel"),
        name="ffn",
    )(*args)


def _s5_kernel(x_ref, g_ref, wbr_ref, wbi_ref, akr_ref, aki_ref, pr_ref, pi_ref, wcr_ref, wci_ref,
               d_ref, wglu_ref, bglu_ref, o_ref, er_ref, ei_ref, cr_ref, ci_ref, *, tm, tiles_per_seq):
    i = pl.program_id(0)

    @pl.when(i % tiles_per_seq == 0)
    def _():
        cr_ref[...] = jnp.zeros_like(cr_ref)
        ci_ref[...] = jnp.zeros_like(ci_ref)

    x = x_ref[...]
    u = _rms(x, g_ref[...])
    ub = u.astype(BF16)
    first = lax.broadcasted_iota(jnp.int32, (tm, 1), 0) % SUBLANES == 0
    ub_prev = jnp.where(first, 0.0, pltpu.roll(u, 1, 0)).astype(BF16)
    for s in range(S5_SLABS):
        us = jnp.concatenate([ub[:, s * LANES:(s + 1) * LANES], ub_prev[:, s * LANES:(s + 1) * LANES]], axis=1)
        er_ref[:, s * S5_SLAB_CH:(s + 1) * S5_SLAB_CH] = jnp.dot(us, wbr_ref[s], preferred_element_type=F32)
        ei_ref[:, s * S5_SLAB_CH:(s + 1) * S5_SLAB_CH] = jnp.dot(us, wbi_ref[s], preferred_element_type=F32)

    def group(j, carry):
        r0 = pl.multiple_of(j * SUBLANES, SUBLANES)
        xr = er_ref[pl.ds(r0, SUBLANES), :]
        xi = ei_ref[pl.ds(r0, SUBLANES), :]
        for lvl, k in enumerate(S5_SCAN_SHIFTS):
            sr = pltpu.roll(xr, k, 0)
            si = pltpu.roll(xi, k, 0)
            ar, ai = akr_ref[lvl], aki_ref[lvl]
            xr, xi = xr + ar * sr - ai * si, xi + ar * si + ai * sr
        cr, ci = cr_ref[...], ci_ref[...]
        pr, pi_ = pr_ref[...], pi_ref[...]
        hr = xr + pr * cr - pi_ * ci
        hi = xi + pr * ci + pi_ * cr
        er_ref[pl.ds(r0, SUBLANES), :] = hr
        ei_ref[pl.ds(r0, SUBLANES), :] = hi
        cr_ref[...] = jnp.broadcast_to(hr[SUBLANES - 1:SUBLANES, :], (SUBLANES, S5_CH))
        ci_ref[...] = jnp.broadcast_to(hi[SUBLANES - 1:SUBLANES, :], (SUBLANES, S5_CH))
        return carry

    lax.fori_loop(0, tm // SUBLANES, group, 0)

    ys = []
    for s in range(S5_SLABS):
        hr = er_ref[:, s * S5_SLAB_CH:(s + 1) * S5_SLAB_CH].astype(BF16)
        hi = ei_ref[:, s * S5_SLAB_CH:(s + 1) * S5_SLAB_CH].astype(BF16)
        ys.append(jnp.dot(hr, wcr_ref[s], preferred_element_type=F32)
                  - jnp.dot(hi, wci_ref[s], preferred_element_type=F32))
    y = jnp.concatenate(ys, axis=1) + d_ref[...] * u
    z = jax.nn.gelu(y).astype(BF16)
    zz = jnp.dot(z, wglu_ref[...], preferred_element_type=F32) + bglu_ref[...]
    o_ref[...] = x + zz[:, :D_MODEL] * jax.nn.sigmoid(zz[:, D_MODEL:])


def _block_diag_slabs(w):
    g, r, c = w.shape
    per = g // S5_SLABS
    w = w.reshape(S5_SLABS, per, r, c)
    eye = jnp.eye(per, dtype=w.dtype)
    return jnp.einsum('sgrc,gh->sgrhc', w, eye).reshape(S5_SLABS, per * r, per * c)


def _s5_tables(a_re, a_im, log_dt):
    lr, li = a_re.astype(F32), a_im.astype(F32)
    dt = jnp.exp(log_dt.astype(F32))[:, None]
    mag = jnp.exp(lr * dt)
    ab_re, ab_im = mag * jnp.cos(li * dt), mag * jnp.sin(li * dt)
    den = lr * lr + li * li
    m_re = ab_re - 1.0
    f_re = (m_re * lr + ab_im * li) / den
    f_im = (ab_im * lr - m_re * li) / den

    def power(k):
        return ((jnp.exp(lr * dt * k) * jnp.cos(li * dt * k)).reshape(-1),
                (jnp.exp(lr * dt * k) * jnp.sin(li * dt * k)).reshape(-1))

    rows = jnp.arange(SUBLANES)[:, None]
    akr, aki = [], []
    for k in S5_SCAN_SHIFTS:
        pr, pi_ = power(float(k))
        akr.append(jnp.where(rows >= k, pr[None, :], 0.0))
        aki.append(jnp.where(rows >= k, pi_[None, :], 0.0))
    pw = [power(float(k + 1)) for k in range(SUBLANES)]
    p_re = jnp.stack([p[0] for p in pw])
    p_im = jnp.stack([p[1] for p in pw])
    return f_re, f_im, ab_re, ab_im, jnp.stack(akr), jnp.stack(aki), p_re, p_im


def _s5(h, g, a_re, a_im, log_dt, b_re, b_im, c_re, c_im, d_skip, w_glu, b_glu, seq, tm=256):
    t, d = h.shape
    f_re, f_im, ab_re, ab_im, akr, aki, p_re, p_im = _s5_tables(a_re, a_im, log_dt)
    bw_re = f_re[:, :, None] * b_re - f_im[:, :, None] * b_im
    bw_im = f_re[:, :, None] * b_im + f_im[:, :, None] * b_re
    bw1_re = ab_re[:, :, None] * bw_re - ab_im[:, :, None] * bw_im
    bw1_im = ab_re[:, :, None] * bw_im + ab_im[:, :, None] * bw_re
    slabs = lambda w: _block_diag_slabs(jnp.swapaxes(w, 1, 2))
    wbr = jnp.concatenate([slabs(bw_re), slabs(bw1_re)], axis=1).astype(BF16)
    wbi = jnp.concatenate([slabs(bw_im), slabs(bw1_im)], axis=1).astype(BF16)
    wcr = _block_diag_slabs(jnp.swapaxes(c_re.astype(F32), 1, 2)).astype(BF16)
    wci = _block_diag_slabs(jnp.swapaxes(c_im.astype(F32), 1, 2)).astype(BF16)
    c2 = lambda i: (0, 0)
    c3 = lambda i: (0, 0, 0)
    return pl.pallas_call(
        functools.partial(_s5_kernel, tm=tm, tiles_per_seq=seq // tm),
        grid=(t // tm,),
        in_specs=[pl.BlockSpec((tm, d), lambda i: (i, 0)),
                  pl.BlockSpec((1, d), c2),
                  pl.BlockSpec(wbr.shape, c3), pl.BlockSpec(wbi.shape, c3),
                  pl.BlockSpec(akr.shape, c3), pl.BlockSpec(aki.shape, c3),
                  pl.BlockSpec(p_re.shape, c2), pl.BlockSpec(p_im.shape, c2),
                  pl.BlockSpec(wcr.shape, c3), pl.BlockSpec(wci.shape, c3),
                  pl.BlockSpec((1, d), c2),
                  pl.BlockSpec((d, 2 * d), c2),
                  pl.BlockSpec((1, 2 * d), c2)],
        out_specs=pl.BlockSpec((tm, d), lambda i: (i, 0)),
        out_shape=jax.ShapeDtypeStruct((t, d), F32),
        scratch_shapes=[pltpu.VMEM((tm, S5_CH), F32), pltpu.VMEM((tm, S5_CH), F32),
                        pltpu.VMEM((SUBLANES, S5_CH), F32), pltpu.VMEM((SUBLANES, S5_CH), F32)],
        compiler_params=_cparams("arbitrary"),
        name="s5",
    )(h, g.reshape(1, d), wbr, wbi, akr, aki, p_re, p_im, wcr, wci,
      d_skip.astype(F32).reshape(1, d), w_glu.astype(BF16), b_glu.astype(F32).reshape(1, 2 * d))


DIL_CLASSES = 16
DIL_STEPS = 2
DIL_TILE = 256


def _dil_kernel(*refs, seq):
    ncls = DIL_CLASSES
    u_len = seq // ncls
    blk = DIL_BLOCK
    qp_ref, kp_ref, vp_ref, o_ref = refs[:4]
    acc_ref, lsw_ref, m_ref, msw_ref = refs[4:8]
    n_chain = DIL_STEPS * HEADS_PER_TILE
    bufs = refs[8:]
    buf_a, buf_b = bufs[:n_chain], bufs[n_chain:]

    lane = lax.broadcasted_iota(jnp.int32, (1, LANES), 1)
    low = lane < HEAD_DIM
    n_iter = seq // blk // DIL_STEPS

    for pat, (window, dil) in enumerate(DIL_PATTERNS):
        assert window // dil == blk and ncls % dil == 0 and seq % (dil * blk) == 0
        n_run = ncls // dil
        run = SUBLANES * dil
        n_blk = u_len // run
        jq = lax.broadcasted_iota(jnp.int32, (blk, blk), 0)
        jk = lax.broadcasted_iota(jnp.int32, (blk, blk), 1)
        wq = (jq % run) * n_run + jq // run
        wk = (jk % run) * n_run + jk // run
        mask_cur = wk <= wq
        mask_prev = wk >= wq

        def run_rows(step, dil=dil, n_run=n_run, run=run, n_blk=n_blk):
            res = step // n_blk
            bi = step % n_blk
            base = [(res + dil * c) * u_len for c in range(n_run)]
            cur = [pl.multiple_of(b + run * bi, SUBLANES) for b in base]
            prev = [pl.multiple_of(b + run * jnp.maximum(bi - 1, 0), SUBLANES) for b in base]
            return cur, prev, bi > 0

        def gather(ref, offs, run=run):
            parts = [ref[pl.ds(o, run), :] for o in offs]
            return jnp.concatenate(parts, axis=0) if len(parts) > 1 else parts[0]

        def scores(step, hh):
            cur, prev, _ = run_rows(step)
            q = gather(qp_ref, cur)
            q = jnp.where(low if hh == 0 else jnp.logical_not(low), q, 0.0).astype(BF16)
            kcat = jnp.concatenate([gather(kp_ref, prev), gather(kp_ref, cur)], axis=0).astype(BF16)
            return lax.dot_general(q, kcat, NT_DIMS, preferred_element_type=F32)

        def phase(it, bufs, nxt_bufs, pat=pat):
            nxt = jnp.minimum(it + 1, n_iter - 1)
            ahead = [(t, hh) for t in range(DIL_STEPS) for hh in range(HEADS_PER_TILE)]

            def stage(i):
                for t2, hh2 in ahead[i:i + 1]:
                    nxt_bufs[t2 * HEADS_PER_TILE + hh2][...] = scores(nxt * DIL_STEPS + t2, hh2)

            stage(0)
            stage(1)
            for t in range(DIL_STEPS):
                step = it * DIL_STEPS + t
                cur, prev, has_prev = run_rows(step)
                vcat = jnp.concatenate([gather(vp_ref, prev), gather(vp_ref, cur)], axis=0)
                mask = jnp.concatenate([jnp.logical_and(mask_prev, has_prev), mask_cur], axis=1)
                ms, os_ = [], []
                for hh in range(HEADS_PER_TILE):
                    c = t * HEADS_PER_TILE + hh
                    s = jnp.where(mask, bufs[c][...], NEG)
                    mb = jnp.max(s, axis=1, keepdims=True)
                    p = jnp.exp2(s - mb).astype(BF16)
                    vh = jnp.where(low if hh == 0 else jnp.logical_not(low), vcat, 1.0).astype(BF16)
                    os_.append(jnp.dot(p, vh, preferred_element_type=F32))
                    ms.append(jnp.broadcast_to(mb, (blk, LANES)))
                    stage(c + 2)
                m_b = jnp.where(low, ms[0], ms[1])
                msw_b = jnp.where(low, ms[1], ms[0])
                o_b = jnp.where(low, os_[0], os_[1])
                lsw_b = jnp.where(low, os_[1], os_[0])
                for ci, off in enumerate(cur):
                    rows = pl.ds(off, run)
                    piece = slice(ci * run, (ci + 1) * run)
                    if pat == 0:
                        m_ref[rows, :] = m_b[piece]
                        msw_ref[rows, :] = msw_b[piece]
                        acc_ref[rows, :] = o_b[piece]
                        lsw_ref[rows, :] = lsw_b[piece]
                    else:
                        m_o, msw_o = m_ref[rows, :], msw_ref[rows, :]
                        m_n = jnp.maximum(m_o, m_b[piece])
                        msw_n = jnp.maximum(msw_o, msw_b[piece])
                        m_ref[rows, :] = m_n
                        msw_ref[rows, :] = msw_n
                        acc_ref[rows, :] = (acc_ref[rows, :] * jnp.exp2(m_o - m_n)
                                            + o_b[piece] * jnp.exp2(m_b[piece] - m_n))
                        lsw_ref[rows, :] = (lsw_ref[rows, :] * jnp.exp2(msw_o - msw_n)
                                            + lsw_b[piece] * jnp.exp2(msw_b[piece] - msw_n))

        def two_phases(i, carry, phase=phase):
            phase(2 * i, buf_a, buf_b)
            phase(2 * i + 1, buf_b, buf_a)
            return carry

        for t in range(DIL_STEPS):
            for hh in range(HEADS_PER_TILE):
                buf_a[t * HEADS_PER_TILE + hh][...] = scores(t, hh)
        lax.fori_loop(0, n_iter // 2, two_phases, 0)

    def finish(c, carry):
        r0 = pl.multiple_of(c * 256, 256)
        den = pltpu.roll(lsw_ref[pl.ds(r0, 256), :], HEAD_DIM, 1)
        o_ref[pl.ds(r0, 256), :] = (acc_ref[pl.ds(r0, 256), :] / den).astype(o_ref.dtype)
        return carry

    lax.fori_loop(0, seq // 256, finish, 0)


def _dilated(qkv, bsz, seq):
    ncls = DIL_CLASSES
    assert seq % (ncls * DIL_BLOCK) == 0 and (seq // DIL_BLOCK) % (2 * DIL_STEPS) == 0 and seq % 256 == 0
    u_len = seq // ncls
    n_tiles = D_MODEL // LANES
    qkv3 = qkv.reshape(bsz, seq, 3 * D_MODEL)
    spec = lambda off: pl.BlockSpec((None, seq, LANES), lambda b, hp, off=off: (b, 0, off + hp))
    in_specs = [spec(off) for off in (0, n_tiles, 2 * n_tiles)]
    out = pl.pallas_call(
        functools.partial(_dil_kernel, seq=seq),
        grid=(bsz, n_tiles),
        in_specs=in_specs,
        out_specs=pl.BlockSpec((None, seq, LANES), lambda b, hp: (b, 0, hp)),
        out_shape=jax.ShapeDtypeStruct((bsz, seq, D_MODEL), BF16),
        scratch_shapes=[pltpu.VMEM((seq, LANES), F32)] * 4
                       + [pltpu.VMEM((DIL_BLOCK, 2 * DIL_BLOCK), F32)] * (2 * DIL_STEPS * HEADS_PER_TILE),
        compiler_params=_cparams("parallel", "parallel"),
        name="dilated_attn",
    )(*([qkv3] * len(in_specs)))
    out = out.reshape(bsz, ncls, u_len, D_MODEL).transpose(0, 2, 1, 3)
    return out.reshape(bsz * seq, D_MODEL)


def _hgrn_kernel(q_ref, f_ref, i_ref, g_ref, lb_ref, ng_ref, o_ref, st_ref, *, tc):
    @pl.when(pl.program_id(2) == 0)
    def _():
        st_ref[...] = jnp.zeros_like(st_ref)

    chunk, sub = HGRN_CHUNK, HGRN_SUB
    n_ch = tc // chunk
    n_sub = chunk // sub
    r = lax.broadcasted_iota(jnp.int32, (chunk, chunk), 0)
    c = lax.broadcasted_iota(jnp.int32, (chunk, chunk), 1)
    tril = (c <= r).astype(F32)
    trow = lax.broadcasted_iota(jnp.int32, (sub, 1), 0)
    lb = lb_ref[...]
    rows = [slice(ch * chunk, (ch + 1) * chunk) for ch in range(n_ch)]

    qs = [_silu(q_ref[rw, :]) for rw in rows]
    fg = [lb + (1.0 - lb) * jax.nn.sigmoid(f_ref[rw, :]) for rw in rows]
    kk = [1.0 - x for x in fg]
    iv = [i_ref[rw, :] for rw in rows]
    ivb = [x.astype(BF16) for x in iv]
    bc = [jnp.dot(tril, jnp.log(x), preferred_element_type=F32, precision=lax.Precision.HIGHEST) for x in fg]
    bl = [x[chunk - 1:chunk] for x in bc]

    upd = [lax.dot_general(ivb[ch], (kk[ch] * jnp.exp(bl[ch] - bc[ch])).astype(BF16), TN_DIMS,
                           preferred_element_type=F32) for ch in range(n_ch)]
    att = []
    for ch in range(n_ch):
        row_att = []
        for blk in range(1, n_sub):
            lo = blk * sub
            r_i = bc[ch][lo - 1:lo]
            qt = (qs[ch][lo:lo + sub] * jnp.exp(bc[ch][lo:lo + sub] - r_i)).astype(BF16)
            kt = (kk[ch][:lo] * jnp.exp(r_i - bc[ch][:lo])).astype(BF16)
            row_att.append(lax.dot_general(qt, kt, NT_DIMS, preferred_element_type=F32).astype(BF16))
        att.append(row_att)

    st_t = st_ref[...]
    states = []
    for ch in range(n_ch):
        states.append(st_t.astype(BF16))
        st_t = st_t * jnp.exp(bl[ch]) + upd[ch]
    st_ref[...] = st_t

    o_inter = [lax.dot_general((qs[ch] * jnp.exp(bc[ch])).astype(BF16), states[ch], NT_DIMS,
                               preferred_element_type=F32) for ch in range(n_ch)]
    o_off = [[jnp.dot(att[ch][blk - 1], ivb[ch][:blk * sub], preferred_element_type=F32)
              for blk in range(1, n_sub)] for ch in range(n_ch)]

    for ch in range(n_ch):
        outs = []
        for blk in range(n_sub):
            lo = blk * sub
            o_i = o_inter[ch][lo:lo + sub]
            if blk > 0:
                o_i = o_i + o_off[ch][blk - 1]
            q_i, k_i, iv_i, fg_i = (x[ch][lo:lo + sub] for x in (qs, kk, iv, fg))
            u = jnp.zeros_like(q_i)
            for s in reversed(range(sub)):
                decayed = u * fg_i[s + 1:s + 2] if s + 1 < sub else u
                u = jnp.where(trow == s, q_i, decayed)
                a_col = jnp.sum(u * k_i[s:s + 1], axis=1, keepdims=True)
                o_i = o_i + a_col * iv_i[s:s + 1]
            outs.append(o_i)
        o = jnp.concatenate(outs, axis=0)
        o = o * lax.rsqrt(jnp.mean(o * o, axis=-1, keepdims=True) + RMS_EPS) * ng_ref[...]
        o_ref[rows[ch], :] = (o * _silu(g_ref[rows[ch], :])).astype(o_ref.dtype)


def _hgrn(proj, lb, norm_g, bsz, seq, tc=512):
    t = bsz * seq
    tiles = seq // tc
    hd = HGRN_HEAD_DIM
    spec = lambda off: pl.BlockSpec((tc, hd), lambda b, h, j, off=off: (b * tiles + j, off + h))
    return pl.pallas_call(
        functools.partial(_hgrn_kernel, tc=tc),
        grid=(bsz, HGRN_HEADS, tiles),
        in_specs=[spec(0), spec(HGRN_HEADS), spec(2 * HGRN_HEADS), spec(3 * HGRN_HEADS),
                  pl.BlockSpec((1, hd), lambda b, h, j: (0, h)),
                  pl.BlockSpec((1, hd), lambda b, h, j: (0, 0))],
        out_specs=spec(0),
        out_shape=jax.ShapeDtypeStruct((t, D_MODEL), BF16),
        scratch_shapes=[pltpu.VMEM((hd, hd), F32)],
        compiler_params=_cparams("parallel", "parallel", "arbitrary"),
        name="hgrn2",
    )(proj, proj, proj, proj, lb.reshape(1, D_MODEL), norm_g.astype(F32).reshape(1, hd))


MOBA_SUPER = 2
MOBA_Q_TILES = 4
MOBA_VT_ROWS = HEAD_DIM + 16


def _moba_kernel(q_ref, k_ref, v_ref, o_ref, km_ref, kp_ref, vt_ref, qp_ref, acc_ref, *s_refs, n_blk, n_sel):
    blk = MOBA_BLOCK
    sbk = MOBA_SUPER * blk
    lane = lax.broadcasted_iota(jnp.int32, (1, LANES), 1)
    head_of_lane = lane // HEAD_DIM
    hot_lane0 = [(1 - hh) * HEAD_DIM for hh in range(HEADS_PER_TILE)]
    n_rows = HEADS_PER_TILE * n_blk

    ones_row = (lax.broadcasted_iota(jnp.int32, (MOBA_VT_ROWS - HEAD_DIM, 1), 0) == 0).astype(BF16)
    for hh in range(HEADS_PER_TILE):
        vt_ref[hh * MOBA_VT_ROWS + HEAD_DIM:(hh + 1) * MOBA_VT_ROWS, :] = jnp.broadcast_to(
            ones_row, (MOBA_VT_ROWS - HEAD_DIM, n_blk * blk))
    for n in range(n_blk):
        rows = slice(n * blk, (n + 1) * blk)
        kb = k_ref[rows, :]
        v_t = v_ref[rows, :].T
        for hh in range(HEADS_PER_TILE):
            vt_ref[hh * MOBA_VT_ROWS:hh * MOBA_VT_ROWS + HEAD_DIM, rows] = (
                v_t[hh * HEAD_DIM:(hh + 1) * HEAD_DIM].astype(BF16))
        km = jnp.sum(kb, axis=0, keepdims=True) * (1.0 / blk)
        for hh in range(HEADS_PER_TILE):
            mine = head_of_lane == hh
            km_ref[hh * n_blk + n:hh * n_blk + n + 1, :] = jnp.where(mine, km, 0.0)
            hot = (lane == hot_lane0[hh] + n).astype(F32)
            kp_ref[rows, hh * LANES:(hh + 1) * LANES] = jnp.where(mine, kb, hot).astype(BF16)

    prow = lax.broadcasted_iota(jnp.int32, (n_rows, HEADS_PER_TILE * LANES), 0)
    pcol = lax.broadcasted_iota(jnp.int32, (n_rows, HEADS_PER_TILE * LANES), 1)
    target = jnp.zeros_like(prow)
    for hh in range(HEADS_PER_TILE):
        target = jnp.where(prow // n_blk == hh, hh * LANES + hot_lane0[hh] + prow % n_blk, target)
    place = (pcol == target).astype(BF16)
    nid = lax.broadcasted_iota(jnp.int32, (n_blk, blk), 0)
    nid_f = nid.astype(F32)
    km_all = km_ref[...]

    q_rows = [slice(qi * blk, (qi + 1) * blk) for qi in range(n_blk)]
    gates = [lax.dot_general(km_all, q_ref[rw, :], NT_DIMS, preferred_element_type=F32,
                             precision=lax.Precision.HIGHEST) for rw in q_rows]
    biases = []
    for qi in range(n_blk):
        past = nid < qi
        bias_rows = []
        for hh in range(HEADS_PER_TILE):
            g = jnp.where(past, gates[qi][hh * n_blk:(hh + 1) * n_blk], NEG)
            keep = nid == qi
            for _ in range(n_sel):
                mx = jnp.max(g, axis=0, keepdims=True)
                idx = jnp.min(jnp.where(g == mx, nid_f, float(n_blk)), axis=0, keepdims=True)
                pick = nid_f == idx
                keep = jnp.logical_or(keep, jnp.logical_and(pick, past))
                g = jnp.where(pick, BELOW_NEG, g)
            bias_rows.append(jnp.where(keep, 0.0, NEG))
        biases.append(jnp.concatenate(bias_rows, axis=0).astype(BF16))
    bias_qs = [lax.dot_general(b, place, TN_DIMS, preferred_element_type=F32) for b in biases]
    for qi in range(n_blk):
        q = q_ref[q_rows[qi], :]
        for hh in range(HEADS_PER_TILE):
            qp_ref[q_rows[qi], hh * LANES:(hh + 1) * LANES] = jnp.where(
                head_of_lane == hh, q, bias_qs[qi][:, hh * LANES:(hh + 1) * LANES]).astype(BF16)

    krow = lax.broadcasted_iota(jnp.int32, (sbk, blk), 0)
    qcol = lax.broadcasted_iota(jnp.int32, (sbk, blk), 1)
    rel = qcol - krow

    n_qt = MOBA_Q_TILES
    assert n_qt == 2 * MOBA_SUPER
    n_chain = n_qt * HEADS_PER_TILE
    buf_a, buf_b = s_refs[:n_chain], s_refs[n_chain:]
    all_tiles = tuple(range(n_qt))

    def q_operands(g):
        q0 = pl.multiple_of(g * n_qt * blk, n_qt * blk)
        return [[qp_ref[pl.ds(q0 + t * blk, blk), hh * LANES:(hh + 1) * LANES] for hh in range(HEADS_PER_TILE)]
                for t in range(n_qt)]

    def stage(bufs, k0, qps, t, hh):
        kpn = kp_ref[pl.ds(k0, sbk), hh * LANES:(hh + 1) * LANES]
        bufs[t * HEADS_PER_TILE + hh][...] = lax.dot_general(kpn, qps[t][hh], NT_DIMS, preferred_element_type=F32)

    n_groups = n_blk // n_qt

    def q_group(g, carry):
        q0 = pl.multiple_of(g * n_qt * blk, n_qt * blk)
        qps = q_operands(g)
        acc_ref[...] = jnp.zeros_like(acc_ref)

        def step(stats, cur, k0, tiles, causal, nxt=None, nxt_tiles=(), nxt_k0=None, nxt_q=qps):
            chains = [(t, hh) for t in tiles for hh in range(HEADS_PER_TILE)]
            ahead = [(t, hh) for t in nxt_tiles for hh in range(HEADS_PER_TILE)]
            nxt_k0 = k0 + sbk if nxt_k0 is None else nxt_k0
            v_t = [vt_ref[hh * MOBA_VT_ROWS:(hh + 1) * MOBA_VT_ROWS, pl.ds(k0, sbk)] for hh in range(HEADS_PER_TILE)]
            stats = list(stats)
            for t, hh in ahead[:2]:
                stage(nxt, nxt_k0, nxt_q, t, hh)
            for i, (t, hh) in enumerate(chains):
                c = t * HEADS_PER_TILE + hh
                m_o = stats[c]
                s_t = cur[c][...]
                if t in causal:
                    s_t = jnp.where(rel >= causal[t], s_t, NEG)
                m_n = jnp.maximum(m_o, jnp.max(s_t, axis=0, keepdims=True))
                p = jnp.exp2(s_t - m_n).astype(BF16)
                acc_ref[c] = jnp.exp2(m_o - m_n) * acc_ref[c] + jnp.dot(v_t[hh], p, preferred_element_type=F32)
                stats[c] = m_n
                for t2, hh2 in ahead[i + 2:i + 3]:
                    stage(nxt, nxt_k0, nxt_q, t2, hh2)
            for t2, hh2 in ahead[len(chains) + 2:]:
                stage(nxt, nxt_k0, nxt_q, t2, hh2)
            return tuple(stats)

        def pair(i, stats):
            k0 = pl.multiple_of(i * 2 * sbk, 2 * sbk)
            stats = step(stats, buf_a, k0, all_tiles, {}, buf_b, all_tiles)
            return step(stats, buf_b, k0 + sbk, all_tiles, {}, buf_a, all_tiles)

        init = (jnp.full((1, blk), BELOW_NEG, F32),) * n_chain
        stats = lax.fori_loop(0, g, pair, init)
        late = all_tiles[MOBA_SUPER:]
        stats = step(stats, buf_a, q0, all_tiles, {t: -t * blk for t in all_tiles[:MOBA_SUPER]}, buf_b, late)
        step(stats, buf_b, q0 + sbk, late, {t: -(t - MOBA_SUPER) * blk for t in late},
             buf_a, all_tiles, 0, q_operands(jnp.minimum(g + 1, n_groups - 1)))
        for t in all_tiles:
            accs = [acc_ref[t * HEADS_PER_TILE + hh] for hh in range(HEADS_PER_TILE)]
            o_t = jnp.concatenate([a[:HEAD_DIM] / a[HEAD_DIM:HEAD_DIM + 1] for a in accs], axis=0)
            o_ref[pl.ds(q0 + t * blk, blk), :] = o_t.T.astype(o_ref.dtype)
        return carry

    first_q = q_operands(0)
    for t in all_tiles:
        for hh in range(HEADS_PER_TILE):
            stage(buf_a, 0, first_q, t, hh)
    lax.fori_loop(0, n_groups, q_group, 0)


def _moba(qkv, bsz, seq):
    blk = MOBA_BLOCK
    assert seq % (blk * MOBA_Q_TILES) == 0
    n_chain = MOBA_Q_TILES * HEADS_PER_TILE
    n_blk = seq // blk
    n_sel = min(MOBA_TOPK, n_blk - 1)
    n_tiles = D_MODEL // LANES
    qkv3 = qkv.reshape(bsz, seq, 3 * D_MODEL)
    spec = lambda off: pl.BlockSpec((None, seq, LANES), lambda b, hp, off=off: (b, 0, off + hp))
    out = pl.pallas_call(
        functools.partial(_moba_kernel, n_blk=n_blk, n_sel=n_sel),
        grid=(bsz, n_tiles),
        in_specs=[spec(0), spec(n_tiles), spec(2 * n_tiles)],
        out_specs=pl.BlockSpec((None, seq, LANES), lambda b, hp: (b, 0, hp)),
        out_shape=jax.ShapeDtypeStruct((bsz, seq, D_MODEL), BF16),
        scratch_shapes=[pltpu.VMEM((HEADS_PER_TILE * n_blk, LANES), F32),
                        pltpu.VMEM((seq, HEADS_PER_TILE * LANES), BF16),
                        pltpu.VMEM((HEADS_PER_TILE * MOBA_VT_ROWS, seq), BF16),
                        pltpu.VMEM((seq, HEADS_PER_TILE * LANES), BF16),
                        pltpu.VMEM((n_chain, MOBA_VT_ROWS, blk), F32)]
                       + [pltpu.VMEM((MOBA_SUPER * blk, blk), F32)] * (2 * n_chain),
        compiler_params=_cparams("parallel", "parallel"),
        name="moba_attn",
    )(qkv3, qkv3, qkv3)
    return out.reshape(bsz * seq, D_MODEL)


def _rope_tables(positions):
    half = ROPE_DIM // 2
    j = jnp.arange(LANES, dtype=jnp.int32) % HEAD_DIM
    inv = ROPE_THETA ** (-((j % half) * 2).astype(F32) / ROPE_DIM)
    freq = jnp.where(j < ROPE_DIM, inv, 0.0)
    ang = positions.astype(F32).reshape(-1)[:, None] * freq[None, :]
    cos, sin = jnp.cos(ang), jnp.sin(ang)
    s1 = jnp.where(j < half, -sin, 0.0)
    s2 = jnp.where(jnp.logical_and(j >= half, j < ROPE_DIM), sin, 0.0)
    return cos, s1, s2


def _qkv_weight(w):
    q_scale = HEAD_DIM ** -0.5 * math.log2(math.e)
    scale = jnp.concatenate([jnp.full((D_MODEL,), q_scale, F32), jnp.ones((2 * D_MODEL,), F32)])
    return (w.astype(F32) * scale[None, :]).astype(BF16)


def kernel(x, positions, norm_mix, norm_ffn, norm_final, s5_a_re, s5_a_im, s5_log_dt, s5_b_re, s5_b_im, s5_c_re, s5_c_im, s5_d, s5_w_glu, s5_b_glu, dil_w_qkv, dil_w_o, hgrn_w_in, hgrn_lower_bound, hgrn_norm, hgrn_w_o, moba_w_qkv, moba_w_o, ffn_w_gate_up, ffn_w_down):
    bsz, seq, d = x.shape
    depth = norm_mix.shape[0]
    n_mixers = 4
    t = bsz * seq
    rope = _rope_tables(positions)
    per = DIL_TILE // DIL_CLASSES
    pos_cm = positions.reshape(bsz, seq // DIL_TILE, per, DIL_CLASSES).transpose(0, 1, 3, 2)
    rope_cm = _rope_tables(pos_cm)
    wgu = ffn_w_gate_up.astype(BF16)
    wdn = ffn_w_down.astype(BF16)
    lb_w = jax.nn.softmax(hgrn_lower_bound.astype(F32), axis=0)
    lower_bounds = jnp.cumsum(lb_w, axis=0) - lb_w[0]
    norm_mix = norm_mix.astype(F32)
    norm_ffn = norm_ffn.astype(F32)

    h = x.reshape(t, d).astype(F32)
    for layer in range(depth):
        mixer, j = layer % n_mixers, layer // n_mixers
        final_g = norm_final.astype(F32) if layer == depth - 1 else None
        mix = None
        if mixer == 0:
            h = _s5(h, norm_mix[layer], s5_a_re[j], s5_a_im[j], s5_log_dt[j], s5_b_re[j], s5_b_im[j],
                    s5_c_re[j], s5_c_im[j], s5_d[j], s5_w_glu[j], s5_b_glu[j], seq)
        elif mixer == 1:
            qkv = _proj(h, norm_mix[layer], _qkv_weight(dil_w_qkv[j]), rope_cm, 2 * D_MODEL,
                        class_major=(bsz, seq, DIL_CLASSES), tm=DIL_TILE)
            mix = (_dilated(qkv, bsz, seq), dil_w_o[j].astype(BF16))
        elif mixer == 2:
            proj = _proj(h, norm_mix[layer], hgrn_w_in[j].astype(BF16))
            mix = (_hgrn(proj, lower_bounds[layer], hgrn_norm[j], bsz, seq), hgrn_w_o[j].astype(BF16))
        else:
            qkv = _proj(h, norm_mix[layer], _qkv_weight(moba_w_qkv[j]), rope, 2 * D_MODEL)
            mix = (_moba(qkv, bsz, seq), moba_w_o[j].astype(BF16))
        h = _ffn(h, norm_ffn[layer], wgu, wdn, layer, mix=mix, final_g=final_g)
    return h.reshape(bsz, seq, d).astype(x.dtype)
```

```python
import functools
import math

import jax
import jax.numpy as jnp
from jax import lax
from jax.experimental import pallas as pl
from jax.experimental.pallas import tpu as pltpu

F32 = jnp.float32
BF16 = jnp.bfloat16

D_MODEL = 1024
D_FF = 2816
RMS_EPS = 1e-6
NEG = -1e30
BELOW_NEG = -3e38

HEAD_DIM = 64
ATT_HEADS = D_MODEL // HEAD_DIM
ROPE_DIM = HEAD_DIM // 4
ROPE_THETA = 500000.0
LANES = 128
SUBLANES = 8
HEADS_PER_TILE = LANES // HEAD_DIM

S5_GROUP = 16
S5_GROUPS = D_MODEL // S5_GROUP
S5_STATE = 64
S5_CH = S5_GROUPS * S5_STATE
S5_SLABS = D_MODEL // LANES
S5_SLAB_CH = S5_CH // S5_SLABS
S5_SCAN_SHIFTS = (2, 4)

DIL_PATTERNS = ((128, 1), (512, 4), (2048, 16))
DIL_BLOCK = 128

HGRN_HEAD_DIM = 128
HGRN_HEADS = D_MODEL // HGRN_HEAD_DIM
HGRN_CHUNK = 64
HGRN_SUB = 8

MOBA_BLOCK = 256
MOBA_TOPK = 3

VMEM_LIMIT = 56 * 1024 * 1024

NT_DIMS = (((1,), (1,)), ((), ()))
TN_DIMS = (((0,), (0,)), ((), ()))


def _cparams(*sem):
    return pltpu.CompilerParams(dimension_semantics=sem, vmem_limit_bytes=VMEM_LIMIT)


def _rms(x, g):
    return x * lax.rsqrt(jnp.mean(x * x, axis=-1, keepdims=True) + RMS_EPS) * g


def _silu(x):
    return x * jax.nn.sigmoid(x)


def _proj_kernel(*refs, n_out, tn, rope_cols, n_cls):
    if rope_cols:
        x_ref, g_ref, w_ref, c_ref, s1_ref, s2_ref, o_ref = refs
    else:
        x_ref, g_ref, w_ref, o_ref = refs
    tm = x_ref.shape[0]
    per = tm // n_cls
    xn = _rms(x_ref[...], g_ref[...]).astype(BF16)
    if n_cls > 1:
        dst = lax.broadcasted_iota(jnp.int32, (tm, tm), 0)
        src = lax.broadcasted_iota(jnp.int32, (tm, tm), 1)
        perm = (src == (dst % per) * n_cls + dst // per).astype(BF16)
        xn = jnp.dot(perm, xn, preferred_element_type=F32).astype(BF16)
    if rope_cols:
        cos, s1, s2 = c_ref[...], s1_ref[...], s2_ref[...]
    for c in range(n_out // tn):
        y = jnp.dot(xn, w_ref[:, c * tn:(c + 1) * tn], preferred_element_type=F32)
        if c * tn < rope_cols:
            parts = []
            for j in range(tn // LANES):
                yj = y[:, j * LANES:(j + 1) * LANES]
                half = ROPE_DIM // 2
                parts.append(yj * cos + pltpu.roll(yj, LANES - half, 1) * s1 + pltpu.roll(yj, half, 1) * s2)
            y = jnp.concatenate(parts, axis=1) if len(parts) > 1 else parts[0]
        if n_cls == 1:
            o_ref[:, c * tn:(c + 1) * tn] = y
        else:
            for r in range(n_cls):
                o_ref[r, :, c * tn:(c + 1) * tn] = y[r * per:(r + 1) * per]


def _proj(h, g, w, rope=None, rope_cols=0, class_major=None, tm=256, tn=256):
    t, d = h.shape
    n_out = w.shape[1]
    in_specs = [pl.BlockSpec((tm, d), lambda i: (i, 0)),
                pl.BlockSpec((1, d), lambda i: (0, 0)),
                pl.BlockSpec((d, n_out), lambda i: (0, 0))]
    args = [h, g.reshape(1, d), w]
    if rope_cols:
        in_specs += [pl.BlockSpec((tm, LANES), lambda i: (i, 0))] * 3
        args += list(rope)
    if class_major is None:
        n_cls = 1
        out_spec = pl.BlockSpec((tm, n_out), lambda i: (i, 0))
        out_shape = jax.ShapeDtypeStruct((t, n_out), F32)
    else:
        bsz, seq, n_cls = class_major
        tiles = seq // tm
        per = tm // n_cls
        assert seq % tm == 0 and per % SUBLANES == 0
        out_spec = pl.BlockSpec((None, n_cls, per, n_out), lambda i: (i // tiles, 0, i % tiles, 0))
        out_shape = jax.ShapeDtypeStruct((bsz, n_cls, seq // n_cls, n_out), F32)
    return pl.pallas_call(
        functools.partial(_proj_kernel, n_out=n_out, tn=tn, rope_cols=rope_cols, n_cls=n_cls),
        grid=(t // tm,),
        in_specs=in_specs,
        out_specs=out_spec,
        out_shape=out_shape,
        compiler_params=_cparams("parallel"),
        name="proj_rope" if rope_cols else "proj",
    )(*args)


def _ffn_kernel(*refs, has_mix, has_final, fc):
    refs = list(refs)
    h_ref, g_ref, wgu_ref, wd_ref = refs[:4]
    pos = 4
    if has_mix:
        a_ref, wo_ref = refs[pos:pos + 2]
        pos += 2
    if has_final:
        fg_ref = refs[pos]
        pos += 1
    o_ref, acc_ref = refs[pos], refs[pos + 1]

    h = h_ref[...]
    if has_mix:
        h = h + jnp.dot(a_ref[...], wo_ref[...], preferred_element_type=F32)
    xn = _rms(h, g_ref[...]).astype(BF16)

    def gate_up(c):
        gate = jnp.dot(xn, wgu_ref[:, c * fc:(c + 1) * fc], preferred_element_type=F32)
        up = jnp.dot(xn, wgu_ref[:, D_FF + c * fc:D_FF + (c + 1) * fc], preferred_element_type=F32)
        return gate, up

    n_chunks = D_FF // fc
    nxt = gate_up(0)
    for c in range(n_chunks):
        gate, up = nxt
        if c + 1 < n_chunks:
            nxt = gate_up(c + 1)
        act = (_silu(gate) * up).astype(BF16)
        contrib = jnp.dot(act, wd_ref[c * fc:(c + 1) * fc, :], preferred_element_type=F32)
        if c == 0:
            acc_ref[...] = contrib
        else:
            acc_ref[...] += contrib
    out = h + acc_ref[...]
    if has_final:
        out = _rms(out, fg_ref[...])
    o_ref[...] = out


def _ffn(h, g, wgu, wd, layer, mix=None, final_g=None, tm=512, fc=256):
    t, d = h.shape
    const = lambda i: (0, 0)
    in_specs = [pl.BlockSpec((tm, d), lambda i: (i, 0)),
                pl.BlockSpec((1, d), const),
                pl.BlockSpec((None, d, 2 * D_FF), lambda i: (layer, 0, 0)),
                pl.BlockSpec((None, D_FF, d), lambda i: (layer, 0, 0))]
    args = [h, g.reshape(1, d), wgu, wd]
    if mix is not None:
        a, wo = mix
        in_specs += [pl.BlockSpec((tm, d), lambda i: (i, 0)), pl.BlockSpec((d, d), const)]
        args += [a, wo]
    if final_g is not None:
        in_specs += [pl.BlockSpec((1, d), const)]
        args += [final_g.reshape(1, d)]
    return pl.pallas_call(
        functools.partial(_ffn_kernel, has_mix=mix is not None, has_final=final_g is not None, fc=fc),
        grid=(t // tm,),
        in_specs=in_specs,
        out_specs=pl.BlockSpec((tm, d), lambda i: (i, 0)),
        out_shape=jax.ShapeDtypeStruct((t, d), F32),
        scratch_shapes=[pltpu.VMEM((tm, d), F32)],
        compiler_params=_cparams("parallel"),
        name="ffn",
    )(*args)


def _s5_kernel(x_ref, g_ref, wbr_ref, wbi_ref, akr_ref, aki_ref, pr_ref, pi_ref, wcr_ref, wci_ref,
               d_ref, wglu_ref, bglu_ref, o_ref, er_ref, ei_ref, cr_ref, ci_ref, *, tm, tiles_per_seq):
    i = pl.program_id(0)

    @pl.when(i % tiles_per_seq == 0)
    def _():
        cr_ref[...] = jnp.zeros_like(cr_ref)
        ci_ref[...] = jnp.zeros_like(ci_ref)

    x = x_ref[...]
    u = _rms(x, g_ref[...])
    ub = u.astype(BF16)
    first = lax.broadcasted_iota(jnp.int32, (tm, 1), 0) % SUBLANES == 0
    ub_prev = jnp.where(first, 0.0, pltpu.roll(u, 1, 0)).astype(BF16)
    for s in range(S5_SLABS):
        us = jnp.concatenate([ub[:, s * LANES:(s + 1) * LANES], ub_prev[:, s * LANES:(s + 1) * LANES]], axis=1)
        er_ref[:, s * S5_SLAB_CH:(s + 1) * S5_SLAB_CH] = jnp.dot(us, wbr_ref[s], preferred_element_type=F32)
        ei_ref[:, s * S5_SLAB_CH:(s + 1) * S5_SLAB_CH] = jnp.dot(us, wbi_ref[s], preferred_element_type=F32)

    def group(j, carry):
        r0 = pl.multiple_of(j * SUBLANES, SUBLANES)
        xr = er_ref[pl.ds(r0, SUBLANES), :]
        xi = ei_ref[pl.ds(r0, SUBLANES), :]
        for lvl, k in enumerate(S5_SCAN_SHIFTS):
            sr = pltpu.roll(xr, k, 0)
            si = pltpu.roll(xi, k, 0)
            ar, ai = akr_ref[lvl], aki_ref[lvl]
            xr, xi = xr + ar * sr - ai * si, xi + ar * si + ai * sr
        cr, ci = cr_ref[...], ci_ref[...]
        pr, pi_ = pr_ref[...], pi_ref[...]
        hr = xr + pr * cr - pi_ * ci
        hi = xi + pr * ci + pi_ * cr
        er_ref[pl.ds(r0, SUBLANES), :] = hr
        ei_ref[pl.ds(r0, SUBLANES), :] = hi
        cr_ref[...] = jnp.broadcast_to(hr[SUBLANES - 1:SUBLANES, :], (SUBLANES, S5_CH))
        ci_ref[...] = jnp.broadcast_to(hi[SUBLANES - 1:SUBLANES, :], (SUBLANES, S5_CH))
        return carry

    lax.fori_loop(0, tm // SUBLANES, group, 0)

    ys = []
    for s in range(S5_SLABS):
        hr = er_ref[:, s * S5_SLAB_CH:(s + 1) * S5_SLAB_CH].astype(BF16)
        hi = ei_ref[:, s * S5_SLAB_CH:(s + 1) * S5_SLAB_CH].astype(BF16)
        ys.append(jnp.dot(hr, wcr_ref[s], preferred_element_type=F32)
                  - jnp.dot(hi, wci_ref[s], preferred_element_type=F32))
    y = jnp.concatenate(ys, axis=1) + d_ref[...] * u
    z = jax.nn.gelu(y).astype(BF16)
    zz = jnp.dot(z, wglu_ref[...], preferred_element_type=F32) + bglu_ref[...]
    o_ref[...] = x + zz[:, :D_MODEL] * jax.nn.sigmoid(zz[:, D_MODEL:])


def _block_diag_slabs(w):
    g, r, c = w.shape
    per = g // S5_SLABS
    w = w.reshape(S5_SLABS, per, r, c)
    eye = jnp.eye(per, dtype=w.dtype)
    return jnp.einsum('sgrc,gh->sgrhc', w, eye).reshape(S5_SLABS, per * r, per * c)


def _s5_tables(a_re, a_im, log_dt):
    lr, li = a_re.astype(F32), a_im.astype(F32)
    dt = jnp.exp(log_dt.astype(F32))[:, None]
    mag = jnp.exp(lr * dt)
    ab_re, ab_im = mag * jnp.cos(li * dt), mag * jnp.sin(li * dt)
    den = lr * lr + li * li
    m_re = ab_re - 1.0
    f_re = (m_re * lr + ab_im * li) / den
    f_im = (ab_im * lr - m_re * li) / den

    def power(k):
        return ((jnp.exp(lr * dt * k) * jnp.cos(li * dt * k)).reshape(-1),
                (jnp.exp(lr * dt * k) * jnp.sin(li * dt * k)).reshape(-1))

    rows = jnp.arange(SUBLANES)[:, None]
    akr, aki = [], []
    for k in S5_SCAN_SHIFTS:
        pr, pi_ = power(float(k))
        akr.append(jnp.where(rows >= k, pr[None, :], 0.0))
        aki.append(jnp.where(rows >= k, pi_[None, :], 0.0))
    pw = [power(float(k + 1)) for k in range(SUBLANES)]
    p_re = jnp.stack([p[0] for p in pw])
    p_im = jnp.stack([p[1] for p in pw])
    return f_re, f_im, ab_re, ab_im, jnp.stack(akr), jnp.stack(aki), p_re, p_im


def _s5(h, g, a_re, a_im, log_dt, b_re, b_im, c_re, c_im, d_skip, w_glu, b_glu, seq, tm=512):
    t, d = h.shape
    f_re, f_im, ab_re, ab_im, akr, aki, p_re, p_im = _s5_tables(a_re, a_im, log_dt)
    bw_re = f_re[:, :, None] * b_re - f_im[:, :, None] * b_im
    bw_im = f_re[:, :, None] * b_im + f_im[:, :, None] * b_re
    bw1_re = ab_re[:, :, None] * bw_re - ab_im[:, :, None] * bw_im
    bw1_im = ab_re[:, :, None] * bw_im + ab_im[:, :, None] * bw_re
    slabs = lambda w: _block_diag_slabs(jnp.swapaxes(w, 1, 2))
    wbr = jnp.concatenate([slabs(bw_re), slabs(bw1_re)], axis=1).astype(BF16)
    wbi = jnp.concatenate([slabs(bw_im), slabs(bw1_im)], axis=1).astype(BF16)
    wcr = _block_diag_slabs(jnp.swapaxes(c_re.astype(F32), 1, 2)).astype(BF16)
    wci = _block_diag_slabs(jnp.swapaxes(c_im.astype(F32), 1, 2)).astype(BF16)
    c2 = lambda i: (0, 0)
    c3 = lambda i: (0, 0, 0)
    return pl.pallas_call(
        functools.partial(_s5_kernel, tm=tm, tiles_per_seq=seq // tm),
        grid=(t // tm,),
        in_specs=[pl.BlockSpec((tm, d), lambda i: (i, 0)),
                  pl.BlockSpec((1, d), c2),
                  pl.BlockSpec(wbr.shape, c3), pl.BlockSpec(wbi.shape, c3),
                  pl.BlockSpec(akr.shape, c3), pl.BlockSpec(aki.shape, c3),
                  pl.BlockSpec(p_re.shape, c2), pl.BlockSpec(p_im.shape, c2),
                  pl.BlockSpec(wcr.shape, c3), pl.BlockSpec(wci.shape, c3),
                  pl.BlockSpec((1, d), c2),
                  pl.BlockSpec((d, 2 * d), c2),
                  pl.BlockSpec((1, 2 * d), c2)],
        out_specs=pl.BlockSpec((tm, d), lambda i: (i, 0)),
        out_shape=jax.ShapeDtypeStruct((t, d), F32),
        scratch_shapes=[pltpu.VMEM((tm, S5_CH), F32), pltpu.VMEM((tm, S5_CH), F32),
                        pltpu.VMEM((SUBLANES, S5_CH), F32), pltpu.VMEM((SUBLANES, S5_CH), F32)],
        compiler_params=_cparams("arbitrary"),
        name="s5",
    )(h, g.reshape(1, d), wbr, wbi, akr, aki, p_re, p_im, wcr, wci,
      d_skip.astype(F32).reshape(1, d), w_glu.astype(BF16), b_glu.astype(F32).reshape(1, 2 * d))


DIL_CLASSES = 16
DIL_STEPS = 2
DIL_TILE = 256


def _dil_kernel(*refs, seq):
    ncls = DIL_CLASSES
    u_len = seq // ncls
    blk = DIL_BLOCK
    qp_ref, kp_ref, vp_ref, o_ref = refs[:4]
    acc_ref, lsw_ref, m_ref, msw_ref = refs[4:8]
    n_chain = DIL_STEPS * HEADS_PER_TILE
    bufs = refs[8:]
    buf_a, buf_b = bufs[:n_chain], bufs[n_chain:]

    lane = lax.broadcasted_iota(jnp.int32, (1, LANES), 1)
    low = lane < HEAD_DIM
    n_iter = seq // blk // DIL_STEPS

    for pat, (window, dil) in enumerate(DIL_PATTERNS):
        assert window // dil == blk and ncls % dil == 0 and seq % (dil * blk) == 0
        n_run = ncls // dil
        run = SUBLANES * dil
        n_blk = u_len // run
        jq = lax.broadcasted_iota(jnp.int32, (blk, blk), 0)
        jk = lax.broadcasted_iota(jnp.int32, (blk, blk), 1)
        wq = (jq % run) * n_run + jq // run
        wk = (jk % run) * n_run + jk // run
        mask_cur = wk <= wq
        mask_prev = wk >= wq

        def run_rows(step, dil=dil, n_run=n_run, run=run, n_blk=n_blk):
            res = step // n_blk
            bi = step % n_blk
            base = [(res + dil * c) * u_len for c in range(n_run)]
            cur = [pl.multiple_of(b + run * bi, SUBLANES) for b in base]
            prev = [pl.multiple_of(b + run * jnp.maximum(bi - 1, 0), SUBLANES) for b in base]
            return cur, prev, bi > 0

        def gather(ref, offs, run=run):
            parts = [ref[pl.ds(o, run), :] for o in offs]
            return jnp.concatenate(parts, axis=0) if len(parts) > 1 else parts[0]

        def scores(step, hh):
            cur, prev, _ = run_rows(step)
            q = gather(qp_ref, cur)
            q = jnp.where(low if hh == 0 else jnp.logical_not(low), q, 0.0).astype(BF16)
            kcat = jnp.concatenate([gather(kp_ref, prev), gather(kp_ref, cur)], axis=0).astype(BF16)
            return lax.dot_general(q, kcat, NT_DIMS, preferred_element_type=F32)

        def phase(it, bufs, nxt_bufs, pat=pat):
            nxt = jnp.minimum(it + 1, n_iter - 1)
            ahead = [(t, hh) for t in range(DIL_STEPS) for hh in range(HEADS_PER_TILE)]

            def stage(i):
                for t2, hh2 in ahead[i:i + 1]:
                    nxt_bufs[t2 * HEADS_PER_TILE + hh2][...] = scores(nxt * DIL_STEPS + t2, hh2)

            stage(0)
            stage(1)
            for t in range(DIL_STEPS):
                step = it * DIL_STEPS + t
                cur, prev, has_prev = run_rows(step)
                vcat = jnp.concatenate([gather(vp_ref, prev), gather(vp_ref, cur)], axis=0)
                mask = jnp.concatenate([jnp.logical_and(mask_prev, has_prev), mask_cur], axis=1)
                ms, os_ = [], []
                for hh in range(HEADS_PER_TILE):
                    c = t * HEADS_PER_TILE + hh
                    s = jnp.where(mask, bufs[c][...], NEG)
                    mb = jnp.max(s, axis=1, keepdims=True)
                    p = jnp.exp2(s - mb).astype(BF16)
                    vh = jnp.where(low if hh == 0 else jnp.logical_not(low), vcat, 1.0).astype(BF16)
                    os_.append(jnp.dot(p, vh, preferred_element_type=F32))
                    ms.append(jnp.broadcast_to(mb, (blk, LANES)))
                    stage(c + 2)
                m_b = jnp.where(low, ms[0], ms[1])
                msw_b = jnp.where(low, ms[1], ms[0])
                o_b = jnp.where(low, os_[0], os_[1])
                lsw_b = jnp.where(low, os_[1], os_[0])
                for ci, off in enumerate(cur):
                    rows = pl.ds(off, run)
                    piece = slice(ci * run, (ci + 1) * run)
                    if pat == 0:
                        m_ref[rows, :] = m_b[piece]
                        msw_ref[rows, :] = msw_b[piece]
                        acc_ref[rows, :] = o_b[piece]
                        lsw_ref[rows, :] = lsw_b[piece]
                    else:
                        m_o, msw_o = m_ref[rows, :], msw_ref[rows, :]
                        m_n = jnp.maximum(m_o, m_b[piece])
                        msw_n = jnp.maximum(msw_o, msw_b[piece])
                        m_ref[rows, :] = m_n
                        msw_ref[rows, :] = msw_n
                        acc_ref[rows, :] = (acc_ref[rows, :] * jnp.exp2(m_o - m_n)
                                            + o_b[piece] * jnp.exp2(m_b[piece] - m_n))
                        lsw_ref[rows, :] = (lsw_ref[rows, :] * jnp.exp2(msw_o - msw_n)
                                            + lsw_b[piece] * jnp.exp2(msw_b[piece] - msw_n))

        def two_phases(i, carry, phase=phase):
            phase(2 * i, buf_a, buf_b)
            phase(2 * i + 1, buf_b, buf_a)
            return carry

        for t in range(DIL_STEPS):
            for hh in range(HEADS_PER_TILE):
                buf_a[t * HEADS_PER_TILE + hh][...] = scores(t, hh)
        lax.fori_loop(0, n_iter // 2, two_phases, 0)

    def finish(c, carry):
        r0 = pl.multiple_of(c * 256, 256)
        den = pltpu.roll(lsw_ref[pl.ds(r0, 256), :], HEAD_DIM, 1)
        o_ref[pl.ds(r0, 256), :] = (acc_ref[pl.ds(r0, 256), :] / den).astype(o_ref.dtype)
        return carry

    lax.fori_loop(0, seq // 256, finish, 0)


def _dilated(qkv, bsz, seq):
    ncls = DIL_CLASSES
    assert seq % (ncls * DIL_BLOCK) == 0 and (seq // DIL_BLOCK) % (2 * DIL_STEPS) == 0 and seq % 256 == 0
    u_len = seq // ncls
    n_tiles = D_MODEL // LANES
    qkv3 = qkv.reshape(bsz, seq, 3 * D_MODEL)
    spec = lambda off: pl.BlockSpec((None, seq, LANES), lambda b, hp, off=off: (b, 0, off + hp))
    in_specs = [spec(off) for off in (0, n_tiles, 2 * n_tiles)]
    out = pl.pallas_call(
        functools.partial(_dil_kernel, seq=seq),
        grid=(bsz, n_tiles),
        in_specs=in_specs,
        out_specs=pl.BlockSpec((None, seq, LANES), lambda b, hp: (b, 0, hp)),
        out_shape=jax.ShapeDtypeStruct((bsz, seq, D_MODEL), BF16),
        scratch_shapes=[pltpu.VMEM((seq, LANES), F32)] * 4
                       + [pltpu.VMEM((DIL_BLOCK, 2 * DIL_BLOCK), F32)] * (2 * DIL_STEPS * HEADS_PER_TILE),
        compiler_params=_cparams("parallel", "parallel"),
        name="dilated_attn",
    )(*([qkv3] * len(in_specs)))
    out = out.reshape(bsz, ncls, u_len, D_MODEL).transpose(0, 2, 1, 3)
    return out.reshape(bsz * seq, D_MODEL)


def _hgrn_kernel(q_ref, f_ref, i_ref, g_ref, lb_ref, ng_ref, o_ref, st_ref, *, tc):
    @pl.when(pl.program_id(2) == 0)
    def _():
        st_ref[...] = jnp.zeros_like(st_ref)

    chunk, sub = HGRN_CHUNK, HGRN_SUB
    n_ch = tc // chunk
    n_sub = chunk // sub
    r = lax.broadcasted_iota(jnp.int32, (chunk, chunk), 0)
    c = lax.broadcasted_iota(jnp.int32, (chunk, chunk), 1)
    tril = (c <= r).astype(F32)
    trow = lax.broadcasted_iota(jnp.int32, (sub, 1), 0)
    lb = lb_ref[...]
    rows = [slice(ch * chunk, (ch + 1) * chunk) for ch in range(n_ch)]

    qs = [_silu(q_ref[rw, :]) for rw in rows]
    fg = [lb + (1.0 - lb) * jax.nn.sigmoid(f_ref[rw, :]) for rw in rows]
    kk = [1.0 - x for x in fg]
    iv = [i_ref[rw, :] for rw in rows]
    ivb = [x.astype(BF16) for x in iv]
    bc = [jnp.dot(tril, jnp.log(x), preferred_element_type=F32, precision=lax.Precision.HIGHEST) for x in fg]
    bl = [x[chunk - 1:chunk] for x in bc]

    upd = [lax.dot_general(ivb[ch], (kk[ch] * jnp.exp(bl[ch] - bc[ch])).astype(BF16), TN_DIMS,
                           preferred_element_type=F32) for ch in range(n_ch)]
    att = []
    for ch in range(n_ch):
        row_att = []
        for blk in range(1, n_sub):
            lo = blk * sub
            r_i = bc[ch][lo - 1:lo]
            qt = (qs[ch][lo:lo + sub] * jnp.exp(bc[ch][lo:lo + sub] - r_i)).astype(BF16)
            kt = (kk[ch][:lo] * jnp.exp(r_i - bc[ch][:lo])).astype(BF16)
            row_att.append(lax.dot_general(qt, kt, NT_DIMS, preferred_element_type=F32).astype(BF16))
        att.append(row_att)

    st_t = st_ref[...]
    states = []
    for ch in range(n_ch):
        states.append(st_t.astype(BF16))
        st_t = st_t * jnp.exp(bl[ch]) + upd[ch]
    st_ref[...] = st_t

    o_inter = [lax.dot_general((qs[ch] * jnp.exp(bc[ch])).astype(BF16), states[ch], NT_DIMS,
                               preferred_element_type=F32) for ch in range(n_ch)]
    o_off = [[jnp.dot(att[ch][blk - 1], ivb[ch][:blk * sub], preferred_element_type=F32)
              for blk in range(1, n_sub)] for ch in range(n_ch)]

    for ch in range(n_ch):
        outs = []
        for blk in range(n_sub):
            lo = blk * sub
            o_i = o_inter[ch][lo:lo + sub]
            if blk > 0:
                o_i = o_i + o_off[ch][blk - 1]
            q_i, k_i, iv_i, fg_i = (x[ch][lo:lo + sub] for x in (qs, kk, iv, fg))
            u = jnp.zeros_like(q_i)
            for s in reversed(range(sub)):
                decayed = u * fg_i[s + 1:s + 2] if s + 1 < sub else u
                u = jnp.where(trow == s, q_i, decayed)
                a_col = jnp.sum(u * k_i[s:s + 1], axis=1, keepdims=True)
                o_i = o_i + a_col * iv_i[s:s + 1]
            outs.append(o_i)
        o = jnp.concatenate(outs, axis=0)
        o = o * lax.rsqrt(jnp.mean(o * o, axis=-1, keepdims=True) + RMS_EPS) * ng_ref[...]
        o_ref[rows[ch], :] = (o * _silu(g_ref[rows[ch], :])).astype(o_ref.dtype)


def _hgrn(proj, lb, norm_g, bsz, seq, tc=1024):
    t = bsz * seq
    tiles = seq // tc
    hd = HGRN_HEAD_DIM
    spec = lambda off: pl.BlockSpec((tc, hd), lambda b, h, j, off=off: (b * tiles + j, off + h))
    return pl.pallas_call(
        functools.partial(_hgrn_kernel, tc=tc),
        grid=(bsz, HGRN_HEADS, tiles),
        in_specs=[spec(0), spec(HGRN_HEADS), spec(2 * HGRN_HEADS), spec(3 * HGRN_HEADS),
                  pl.BlockSpec((1, hd), lambda b, h, j: (0, h)),
                  pl.BlockSpec((1, hd), lambda b, h, j: (0, 0))],
        out_specs=spec(0),
        out_shape=jax.ShapeDtypeStruct((t, D_MODEL), BF16),
        scratch_shapes=[pltpu.VMEM((hd, hd), F32)],
        compiler_params=_cparams("parallel", "parallel", "arbitrary"),
        name="hgrn2",
    )(proj, proj, proj, proj, lb.reshape(1, D_MODEL), norm_g.astype(F32).reshape(1, hd))


MOBA_SUPER = 2
MOBA_Q_TILES = 4
MOBA_VT_ROWS = HEAD_DIM + 16


def _moba_kernel(q_ref, k_ref, v_ref, o_ref, km_ref, kp_ref, vt_ref, qp_ref, acc_ref, *s_refs, n_blk, n_sel):
    blk = MOBA_BLOCK
    sbk = MOBA_SUPER * blk
    lane = lax.broadcasted_iota(jnp.int32, (1, LANES), 1)
    head_of_lane = lane // HEAD_DIM
    hot_lane0 = [(1 - hh) * HEAD_DIM for hh in range(HEADS_PER_TILE)]
    n_rows = HEADS_PER_TILE * n_blk

    ones_row = (lax.broadcasted_iota(jnp.int32, (MOBA_VT_ROWS - HEAD_DIM, 1), 0) == 0).astype(BF16)
    for hh in range(HEADS_PER_TILE):
        vt_ref[hh * MOBA_VT_ROWS + HEAD_DIM:(hh + 1) * MOBA_VT_ROWS, :] = jnp.broadcast_to(
            ones_row, (MOBA_VT_ROWS - HEAD_DIM, n_blk * blk))
    for n in range(n_blk):
        rows = slice(n * blk, (n + 1) * blk)
        kb = k_ref[rows, :]
        v_t = v_ref[rows, :].T
        for hh in range(HEADS_PER_TILE):
            vt_ref[hh * MOBA_VT_ROWS:hh * MOBA_VT_ROWS + HEAD_DIM, rows] = (
                v_t[hh * HEAD_DIM:(hh + 1) * HEAD_DIM].astype(BF16))
        km = jnp.sum(kb, axis=0, keepdims=True) * (1.0 / blk)
        for hh in range(HEADS_PER_TILE):
            mine = head_of_lane == hh
            km_ref[hh * n_blk + n:hh * n_blk + n + 1, :] = jnp.where(mine, km, 0.0)
            hot = (lane == hot_lane0[hh] + n).astype(F32)
            kp_ref[rows, hh * LANES:(hh + 1) * LANES] = jnp.where(mine, kb, hot).astype(BF16)

    prow = lax.broadcasted_iota(jnp.int32, (n_rows, HEADS_PER_TILE * LANES), 0)
    pcol = lax.broadcasted_iota(jnp.int32, (n_rows, HEADS_PER_TILE * LANES), 1)
    target = jnp.zeros_like(prow)
    for hh in range(HEADS_PER_TILE):
        target = jnp.where(prow // n_blk == hh, hh * LANES + hot_lane0[hh] + prow % n_blk, target)
    place = (pcol == target).astype(BF16)
    nid = lax.broadcasted_iota(jnp.int32, (n_blk, blk), 0)
    nid_f = nid.astype(F32)
    km_all = km_ref[...]

    q_rows = [slice(qi * blk, (qi + 1) * blk) for qi in range(n_blk)]
    gates = [lax.dot_general(km_all, q_ref[rw, :], NT_DIMS, preferred_element_type=F32,
                             precision=lax.Precision.HIGHEST) for rw in q_rows]
    biases = []
    for qi in range(n_blk):
        past = nid < qi
        bias_rows = []
        for hh in range(HEADS_PER_TILE):
            g = jnp.where(past, gates[qi][hh * n_blk:(hh + 1) * n_blk], NEG)
            keep = nid == qi
            for _ in range(n_sel):
                mx = jnp.max(g, axis=0, keepdims=True)
                idx = jnp.min(jnp.where(g == mx, nid_f, float(n_blk)), axis=0, keepdims=True)
                pick = nid_f == idx
                keep = jnp.logical_or(keep, jnp.logical_and(pick, past))
                g = jnp.where(pick, BELOW_NEG, g)
            bias_rows.append(jnp.where(keep, 0.0, NEG))
        biases.append(jnp.concatenate(bias_rows, axis=0).astype(BF16))
    bias_qs = [lax.dot_general(b, place, TN_DIMS, preferred_element_type=F32) for b in biases]
    for qi in range(n_blk):
        q = q_ref[q_rows[qi], :]
        for hh in range(HEADS_PER_TILE):
            qp_ref[q_rows[qi], hh * LANES:(hh + 1) * LANES] = jnp.where(
                head_of_lane == hh, q, bias_qs[qi][:, hh * LANES:(hh + 1) * LANES]).astype(BF16)

    krow = lax.broadcasted_iota(jnp.int32, (sbk, blk), 0)
    qcol = lax.broadcasted_iota(jnp.int32, (sbk, blk), 1)
    rel = qcol - krow

    n_qt = MOBA_Q_TILES
    assert n_qt == 2 * MOBA_SUPER
    n_chain = n_qt * HEADS_PER_TILE
    buf_a, buf_b = s_refs[:n_chain], s_refs[n_chain:]
    all_tiles = tuple(range(n_qt))

    def q_operands(g):
        q0 = pl.multiple_of(g * n_qt * blk, n_qt * blk)
        return [[qp_ref[pl.ds(q0 + t * blk, blk), hh * LANES:(hh + 1) * LANES] for hh in range(HEADS_PER_TILE)]
                for t in range(n_qt)]

    def stage(bufs, k0, qps, t, hh):
        kpn = kp_ref[pl.ds(k0, sbk), hh * LANES:(hh + 1) * LANES]
        bufs[t * HEADS_PER_TILE + hh][...] = lax.dot_general(kpn, qps[t][hh], NT_DIMS, preferred_element_type=F32)

    n_groups = n_blk // n_qt

    def q_group(g, carry):
        q0 = pl.multiple_of(g * n_qt * blk, n_qt * blk)
        qps = q_operands(g)
        acc_ref[...] = jnp.zeros_like(acc_ref)

        def step(stats, cur, k0, tiles, causal, nxt=None, nxt_tiles=(), nxt_k0=None, nxt_q=qps):
            chains = [(t, hh) for t in tiles for hh in range(HEADS_PER_TILE)]
            ahead = [(t, hh) for t in nxt_tiles for hh in range(HEADS_PER_TILE)]
            nxt_k0 = k0 + sbk if nxt_k0 is None else nxt_k0
            v_t = [vt_ref[hh * MOBA_VT_ROWS:(hh + 1) * MOBA_VT_ROWS, pl.ds(k0, sbk)] for hh in range(HEADS_PER_TILE)]
            stats = list(stats)
            for t, hh in ahead[:2]:
                stage(nxt, nxt_k0, nxt_q, t, hh)
            for i, (t, hh) in enumerate(chains):
                c = t * HEADS_PER_TILE + hh
                m_o = stats[c]
                s_t = cur[c][...]
                if t in causal:
                    s_t = jnp.where(rel >= causal[t], s_t, NEG)
                m_n = jnp.maximum(m_o, jnp.max(s_t, axis=0, keepdims=True))
                p = jnp.exp2(s_t - m_n).astype(BF16)
                acc_ref[c] = jnp.exp2(m_o - m_n) * acc_ref[c] + jnp.dot(v_t[hh], p, preferred_element_type=F32)
                stats[c] = m_n
                for t2, hh2 in ahead[i + 2:i + 3]:
                    stage(nxt, nxt_k0, nxt_q, t2, hh2)
            for t2, hh2 in ahead[len(chains) + 2:]:
                stage(nxt, nxt_k0, nxt_q, t2, hh2)
            return tuple(stats)

        def pair(i, stats):
            k0 = pl.multiple_of(i * 2 * sbk, 2 * sbk)
            stats = step(stats, buf_a, k0, all_tiles, {}, buf_b, all_tiles)
            return step(stats, buf_b, k0 + sbk, all_tiles, {}, buf_a, all_tiles)

        init = (jnp.full((1, blk), BELOW_NEG, F32),) * n_chain
        stats = lax.fori_loop(0, g, pair, init)
        late = all_tiles[MOBA_SUPER:]
        stats = step(stats, buf_a, q0, all_tiles, {t: -t * blk for t in all_tiles[:MOBA_SUPER]}, buf_b, late)
        step(stats, buf_b, q0 + sbk, late, {t: -(t - MOBA_SUPER) * blk for t in late},
             buf_a, all_tiles, 0, q_operands(jnp.minimum(g + 1, n_groups - 1)))
        for t in all_tiles:
            accs = [acc_ref[t * HEADS_PER_TILE + hh] for hh in range(HEADS_PER_TILE)]
            o_t = jnp.concatenate([a[:HEAD_DIM] / a[HEAD_DIM:HEAD_DIM + 1] for a in accs], axis=0)
            o_ref[pl.ds(q0 + t * blk, blk), :] = o_t.T.astype(o_ref.dtype)
        return carry

    first_q = q_operands(0)
    for t in all_tiles:
        for hh in range(HEADS_PER_TILE):
            stage(buf_a, 0, first_q, t, hh)
    lax.fori_loop(0, n_groups, q_group, 0)


def _moba(qkv, bsz, seq):
    blk = MOBA_BLOCK
    assert seq % (blk * MOBA_Q_TILES) == 0
    n_chain = MOBA_Q_TILES * HEADS_PER_TILE
    n_blk = seq // blk
    n_sel = min(MOBA_TOPK, n_blk - 1)
    n_tiles = D_MODEL // LANES
    qkv3 = qkv.reshape(bsz, seq, 3 * D_MODEL)
    spec = lambda off: pl.BlockSpec((None, seq, LANES), lambda b, hp, off=off: (b, 0, off + hp))
    out = pl.pallas_call(
        functools.partial(_moba_kernel, n_blk=n_blk, n_sel=n_sel),
        grid=(bsz, n_tiles),
        in_specs=[spec(0), spec(n_tiles), spec(2 * n_tiles)],
        out_specs=pl.BlockSpec((None, seq, LANES), lambda b, hp: (b, 0, hp)),
        out_shape=jax.ShapeDtypeStruct((bsz, seq, D_MODEL), BF16),
        scratch_shapes=[pltpu.VMEM((HEADS_PER_TILE * n_blk, LANES), F32),
                        pltpu.VMEM((seq, HEADS_PER_TILE * LANES), BF16),
                        pltpu.VMEM((HEADS_PER_TILE * MOBA_VT_ROWS, seq), BF16),
                        pltpu.VMEM((seq, HEADS_PER_TILE * LANES), BF16),
                        pltpu.VMEM((n_chain, MOBA_VT_ROWS, blk), F32)]
                       + [pltpu.VMEM((MOBA_SUPER * blk, blk), F32)] * (2 * n_chain),
        compiler_params=_cparams("parallel", "parallel"),
        name="moba_attn",
    )(qkv3, qkv3, qkv3)
    return out.reshape(bsz * seq, D_MODEL)


def _rope_tables(positions):
    half = ROPE_DIM // 2
    j = jnp.arange(LANES, dtype=jnp.int32) % HEAD_DIM
    inv = ROPE_THETA ** (-((j % half) * 2).astype(F32) / ROPE_DIM)
    freq = jnp.where(j < ROPE_DIM, inv, 0.0)
    ang = positions.astype(F32).reshape(-1)[:, None] * freq[None, :]
    cos, sin = jnp.cos(ang), jnp.sin(ang)
    s1 = jnp.where(j < half, -sin, 0.0)
    s2 = jnp.where(jnp.logical_and(j >= half, j < ROPE_DIM), sin, 0.0)
    return cos, s1, s2


def _qkv_weight(w):
    q_scale = HEAD_DIM ** -0.5 * math.log2(math.e)
    scale = jnp.concatenate([jnp.full((D_MODEL,), q_scale, F32), jnp.ones((2 * D_MODEL,), F32)])
    return (w.astype(F32) * scale[None, :]).astype(BF16)


def kernel(x, positions, norm_mix, norm_ffn, norm_final, s5_a_re, s5_a_im, s5_log_dt, s5_b_re, s5_b_im, s5_c_re, s5_c_im, s5_d, s5_w_glu, s5_b_glu, dil_w_qkv, dil_w_o, hgrn_w_in, hgrn_lower_bound, hgrn_norm, hgrn_w_o, moba_w_qkv, moba_w_o, ffn_w_gate_up, ffn_w_down):
    bsz, seq, d = x.shape
    depth = norm_mix.shape[0]
    n_mixers = 4
    t = bsz * seq
    rope = _rope_tables(positions)
    per = DIL_TILE // DIL_CLASSES
    pos_cm = positions.reshape(bsz, seq // DIL_TILE, per, DIL_CLASSES).transpose(0, 1, 3, 2)
    rope_cm = _rope_tables(pos_cm)
    wgu = ffn_w_gate_up.astype(BF16)
    wdn = ffn_w_down.astype(BF16)
    lb_w = jax.nn.softmax(hgrn_lower_bound.astype(F32), axis=0)
    lower_bounds = jnp.cumsum(lb_w, axis=0) - lb_w[0]
    norm_mix = norm_mix.astype(F32)
    norm_ffn = norm_ffn.astype(F32)

    h = x.reshape(t, d).astype(F32)
    for layer in range(depth):
        mixer, j = layer % n_mixers, layer // n_mixers
        final_g = norm_final.astype(F32) if layer == depth - 1 else None
        mix = None
        if mixer == 0:
            h = _s5(h, norm_mix[layer], s5_a_re[j], s5_a_im[j], s5_log_dt[j], s5_b_re[j], s5_b_im[j],
                    s5_c_re[j], s5_c_im[j], s5_d[j], s5_w_glu[j], s5_b_glu[j], seq)
        elif mixer == 1:
            qkv = _proj(h, norm_mix[layer], _qkv_weight(dil_w_qkv[j]), rope_cm, 2 * D_MODEL,
                        class_major=(bsz, seq, DIL_CLASSES), tm=DIL_TILE)
            mix = (_dilated(qkv, bsz, seq), dil_w_o[j].astype(BF16))
        elif mixer == 2:
            proj = _proj(h, norm_mix[layer], hgrn_w_in[j].astype(BF16), tm=512)
            mix = (_hgrn(proj, lower_bounds[layer], hgrn_norm[j], bsz, seq), hgrn_w_o[j].astype(BF16))
        else:
            qkv = _proj(h, norm_mix[layer], _qkv_weight(moba_w_qkv[j]), rope, 2 * D_MODEL, tm=512)
            mix = (_moba(qkv, bsz, seq), moba_w_o[j].astype(BF16))
        h = _ffn(h, norm_ffn[layer], wgu, wdn, layer, mix=mix, final_g=final_g)
    return h.reshape(bsz, seq, d).astype(x.dtype)
```

```python
import functools
import math

import jax
import jax.numpy as jnp
from jax import lax
from jax.experimental import pallas as pl
from jax.experimental.pallas import tpu as pltpu

F32 = jnp.float32
BF16 = jnp.bfloat16

D_MODEL = 1024
D_FF = 2816
RMS_EPS = 1e-6
NEG = -1e30
BELOW_NEG = -3e38

HEAD_DIM = 64
ATT_HEADS = D_MODEL // HEAD_DIM
ROPE_DIM = HEAD_DIM // 4
ROPE_THETA = 500000.0
LANES = 128
SUBLANES = 8
HEADS_PER_TILE = LANES // HEAD_DIM

S5_GROUP = 16
S5_GROUPS = D_MODEL // S5_GROUP
S5_STATE = 64
S5_CH = S5_GROUPS * S5_STATE
S5_SLABS = D_MODEL // LANES
S5_SLAB_CH = S5_CH // S5_SLABS
S5_SCAN_SHIFTS = (2, 4)

DIL_PATTERNS = ((128, 1), (512, 4), (2048, 16))
DIL_BLOCK = 128

HGRN_HEAD_DIM = 128
HGRN_HEADS = D_MODEL // HGRN_HEAD_DIM
HGRN_CHUNK = 64
HGRN_SUB = 8

MOBA_BLOCK = 256
MOBA_TOPK = 3

VMEM_LIMIT = 56 * 1024 * 1024

NT_DIMS = (((1,), (1,)), ((), ()))
TN_DIMS = (((0,), (0,)), ((), ()))


def _cparams(*sem):
    return pltpu.CompilerParams(dimension_semantics=sem, vmem_limit_bytes=VMEM_LIMIT)


def _rms(x, g):
    return x * lax.rsqrt(jnp.mean(x * x, axis=-1, keepdims=True) + RMS_EPS) * g


def _silu(x):
    return x * jax.nn.sigmoid(x)


def _proj_kernel(*refs, n_out, tn, rope_cols, n_cls):
    if rope_cols:
        x_ref, g_ref, w_ref, c_ref, s1_ref, s2_ref, o_ref = refs
    else:
        x_ref, g_ref, w_ref, o_ref = refs
    tm = x_ref.shape[0]
    per = tm // n_cls
    xn = _rms(x_ref[...], g_ref[...]).astype(BF16)
    if n_cls > 1:
        dst = lax.broadcasted_iota(jnp.int32, (tm, tm), 0)
        src = lax.broadcasted_iota(jnp.int32, (tm, tm), 1)
        perm = (src == (dst % per) * n_cls + dst // per).astype(BF16)
        xn = jnp.dot(perm, xn, preferred_element_type=F32).astype(BF16)
    if rope_cols:
        cos, s1, s2 = c_ref[...], s1_ref[...], s2_ref[...]
    for c in range(n_out // tn):
        y = jnp.dot(xn, w_ref[:, c * tn:(c + 1) * tn], preferred_element_type=F32)
        if c * tn < rope_cols:
            parts = []
            for j in range(tn // LANES):
                yj = y[:, j * LANES:(j + 1) * LANES]
                half = ROPE_DIM // 2
                parts.append(yj * cos + pltpu.roll(yj, LANES - half, 1) * s1 + pltpu.roll(yj, half, 1) * s2)
            y = jnp.concatenate(parts, axis=1) if len(parts) > 1 else parts[0]
        if n_cls == 1:
            o_ref[:, c * tn:(c + 1) * tn] = y
        else:
            for r in range(n_cls):
                o_ref[r, :, c * tn:(c + 1) * tn] = y[r * per:(r + 1) * per]


def _proj(h, g, w, rope=None, rope_cols=0, class_major=None, tm=256, tn=256):
    t, d = h.shape
    n_out = w.shape[1]
    in_specs = [pl.BlockSpec((tm, d), lambda i: (i, 0)),
                pl.BlockSpec((1, d), lambda i: (0, 0)),
                pl.BlockSpec((d, n_out), lambda i: (0, 0))]
    args = [h, g.reshape(1, d), w]
    if rope_cols:
        in_specs += [pl.BlockSpec((tm, LANES), lambda i: (i, 0))] * 3
        args += list(rope)
    if class_major is None:
        n_cls = 1
        out_spec = pl.BlockSpec((tm, n_out), lambda i: (i, 0))
        out_shape = jax.ShapeDtypeStruct((t, n_out), F32)
    else:
        bsz, seq, n_cls = class_major
        tiles = seq // tm
        per = tm // n_cls
        assert seq % tm == 0 and per % SUBLANES == 0
        out_spec = pl.BlockSpec((None, n_cls, per, n_out), lambda i: (i // tiles, 0, i % tiles, 0))
        out_shape = jax.ShapeDtypeStruct((bsz, n_cls, seq // n_cls, n_out), F32)
    return pl.pallas_call(
        functools.partial(_proj_kernel, n_out=n_out, tn=tn, rope_cols=rope_cols, n_cls=n_cls),
        grid=(t // tm,),
        in_specs=in_specs,
        out_specs=out_spec,
        out_shape=out_shape,
        compiler_params=_cparams("parallel"),
        name="proj_rope" if rope_cols else "proj",
    )(*args)


def _ffn_kernel(*refs, has_mix, has_final, fc):
    refs = list(refs)
    h_ref, g_ref, wgu_ref, wd_ref = refs[:4]
    pos = 4
    if has_mix:
        a_ref, wo_ref = refs[pos:pos + 2]
        pos += 2
    if has_final:
        fg_ref = refs[pos]
        pos += 1
    o_ref, acc_ref = refs[pos], refs[pos + 1]

    h = h_ref[...]
    if has_mix:
        h = h + jnp.dot(a_ref[...], wo_ref[...], preferred_element_type=F32)
    xn = _rms(h, g_ref[...]).astype(BF16)

    def gate_up(c):
        gate = jnp.dot(xn, wgu_ref[:, c * fc:(c + 1) * fc], preferred_element_type=F32)
        up = jnp.dot(xn, wgu_ref[:, D_FF + c * fc:D_FF + (c + 1) * fc], preferred_element_type=F32)
        return gate, up

    n_chunks = D_FF // fc
    nxt = gate_up(0)
    for c in range(n_chunks):
        gate, up = nxt
        if c + 1 < n_chunks:
            nxt = gate_up(c + 1)
        act = (_silu(gate) * up).astype(BF16)
        contrib = jnp.dot(act, wd_ref[c * fc:(c + 1) * fc, :], preferred_element_type=F32)
        if c == 0:
            acc_ref[...] = contrib
        else:
            acc_ref[...] += contrib
    out = h + acc_ref[...]
    if has_final:
        out = _rms(out, fg_ref[...])
    o_ref[...] = out


def _ffn(h, g, wgu, wd, layer, mix=None, final_g=None, tm=512, fc=256):
    t, d = h.shape
    const = lambda i: (0, 0)
    in_specs = [pl.BlockSpec((tm, d), lambda i: (i, 0)),
                pl.BlockSpec((1, d), const),
                pl.BlockSpec((None, d, 2 * D_FF), lambda i: (layer, 0, 0)),
                pl.BlockSpec((None, D_FF, d), lambda i: (layer, 0, 0))]
    args = [h, g.reshape(1, d), wgu, wd]
    if mix is not None:
        a, wo = mix
        in_specs += [pl.BlockSpec((tm, d), lambda i: (i, 0)), pl.BlockSpec((d, d), const)]
        args += [a, wo]
    if final_g is not None:
        in_specs += [pl.BlockSpec((1, d), const)]
        args += [final_g.reshape(1, d)]
    return pl.pallas_call(
        functools.partial(_ffn_kernel, has_mix=mix is not None, has_final=final_g is not None, fc=fc),
        grid=(t // tm,),
        in_specs=in_specs,
        out_specs=pl.BlockSpec((tm, d), lambda i: (i, 0)),
        out_shape=jax.ShapeDtypeStruct((t, d), F32),
        scratch_shapes=[pltpu.VMEM((tm, d), F32)],
        compiler_params=_cparams("parallel"),
        name="ffn",
    )(*args)


def _s5_kernel(x_ref, g_ref, wbr_ref, wbi_ref, akr_ref, aki_ref, pr_ref, pi_ref, wcr_ref, wci_ref,
               d_ref, wglu_ref, bglu_ref, o_ref, er_ref, ei_ref, cr_ref, ci_ref, *, tm, tiles_per_seq):
    i = pl.program_id(0)

    @pl.when(i % tiles_per_seq == 0)
    def _():
        cr_ref[...] = jnp.zeros_like(cr_ref)
        ci_ref[...] = jnp.zeros_like(ci_ref)

    x = x_ref[...]
    u = _rms(x, g_ref[...])
    ub = u.astype(BF16)
    first = lax.broadcasted_iota(jnp.int32, (tm, 1), 0) % SUBLANES == 0
    ub_prev = jnp.where(first, 0.0, pltpu.roll(u, 1, 0)).astype(BF16)
    for s in range(S5_SLABS):
        us = jnp.concatenate([ub[:, s * LANES:(s + 1) * LANES], ub_prev[:, s * LANES:(s + 1) * LANES]], axis=1)
        er_ref[:, s * S5_SLAB_CH:(s + 1) * S5_SLAB_CH] = jnp.dot(us, wbr_ref[s], preferred_element_type=F32)
        ei_ref[:, s * S5_SLAB_CH:(s + 1) * S5_SLAB_CH] = jnp.dot(us, wbi_ref[s], preferred_element_type=F32)

    def group(j, carry):
        r0 = pl.multiple_of(j * SUBLANES, SUBLANES)
        xr = er_ref[pl.ds(r0, SUBLANES), :]
        xi = ei_ref[pl.ds(r0, SUBLANES), :]
        for lvl, k in enumerate(S5_SCAN_SHIFTS):
            sr = pltpu.roll(xr, k, 0)
            si = pltpu.roll(xi, k, 0)
            ar, ai = akr_ref[lvl], aki_ref[lvl]
            xr, xi = xr + ar * sr - ai * si, xi + ar * si + ai * sr
        cr, ci = cr_ref[...], ci_ref[...]
        pr, pi_ = pr_ref[...], pi_ref[...]
        hr = xr + pr * cr - pi_ * ci
        hi = xi + pr * ci + pi_ * cr
        er_ref[pl.ds(r0, SUBLANES), :] = hr
        ei_ref[pl.ds(r0, SUBLANES), :] = hi
        cr_ref[...] = jnp.broadcast_to(hr[SUBLANES - 1:SUBLANES, :], (SUBLANES, S5_CH))
        ci_ref[...] = jnp.broadcast_to(hi[SUBLANES - 1:SUBLANES, :], (SUBLANES, S5_CH))
        return carry

    lax.fori_loop(0, tm // SUBLANES, group, 0)

    ys = []
    for s in range(S5_SLABS):
        hr = er_ref[:, s * S5_SLAB_CH:(s + 1) * S5_SLAB_CH].astype(BF16)
        hi = ei_ref[:, s * S5_SLAB_CH:(s + 1) * S5_SLAB_CH].astype(BF16)
        ys.append(jnp.dot(hr, wcr_ref[s], preferred_element_type=F32)
                  - jnp.dot(hi, wci_ref[s], preferred_element_type=F32))
    y = jnp.concatenate(ys, axis=1) + d_ref[...] * u
    z = jax.nn.gelu(y).astype(BF16)
    zz = jnp.dot(z, wglu_ref[...], preferred_element_type=F32) + bglu_ref[...]
    o_ref[...] = x + zz[:, :D_MODEL] * jax.nn.sigmoid(zz[:, D_MODEL:])


def _block_diag_slabs(w):
    g, r, c = w.shape
    per = g // S5_SLABS
    w = w.reshape(S5_SLABS, per, r, c)
    eye = jnp.eye(per, dtype=w.dtype)
    return jnp.einsum('sgrc,gh->sgrhc', w, eye).reshape(S5_SLABS, per * r, per * c)


def _s5_tables(a_re, a_im, log_dt):
    lr, li = a_re.astype(F32), a_im.astype(F32)
    dt = jnp.exp(log_dt.astype(F32))[:, None]
    mag = jnp.exp(lr * dt)
    ab_re, ab_im = mag * jnp.cos(li * dt), mag * jnp.sin(li * dt)
    den = lr * lr + li * li
    m_re = ab_re - 1.0
    f_re = (m_re * lr + ab_im * li) / den
    f_im = (ab_im * lr - m_re * li) / den

    def power(k):
        return ((jnp.exp(lr * dt * k) * jnp.cos(li * dt * k)).reshape(-1),
                (jnp.exp(lr * dt * k) * jnp.sin(li * dt * k)).reshape(-1))

    rows = jnp.arange(SUBLANES)[:, None]
    akr, aki = [], []
    for k in S5_SCAN_SHIFTS:
        pr, pi_ = power(float(k))
        akr.append(jnp.where(rows >= k, pr[None, :], 0.0))
        aki.append(jnp.where(rows >= k, pi_[None, :], 0.0))
    pw = [power(float(k + 1)) for k in range(SUBLANES)]
    p_re = jnp.stack([p[0] for p in pw])
    p_im = jnp.stack([p[1] for p in pw])
    return f_re, f_im, ab_re, ab_im, jnp.stack(akr), jnp.stack(aki), p_re, p_im


def _s5(h, g, a_re, a_im, log_dt, b_re, b_im, c_re, c_im, d_skip, w_glu, b_glu, seq, tm=512):
    t, d = h.shape
    f_re, f_im, ab_re, ab_im, akr, aki, p_re, p_im = _s5_tables(a_re, a_im, log_dt)
    bw_re = f_re[:, :, None] * b_re - f_im[:, :, None] * b_im
    bw_im = f_re[:, :, None] * b_im + f_im[:, :, None] * b_re
    bw1_re = ab_re[:, :, None] * bw_re - ab_im[:, :, None] * bw_im
    bw1_im = ab_re[:, :, None] * bw_im + ab_im[:, :, None] * bw_re
    slabs = lambda w: _block_diag_slabs(jnp.swapaxes(w, 1, 2))
    wbr = jnp.concatenate([slabs(bw_re), slabs(bw1_re)], axis=1).astype(BF16)
    wbi = jnp.concatenate([slabs(bw_im), slabs(bw1_im)], axis=1).astype(BF16)
    wcr = _block_diag_slabs(jnp.swapaxes(c_re.astype(F32), 1, 2)).astype(BF16)
    wci = _block_diag_slabs(jnp.swapaxes(c_im.astype(F32), 1, 2)).astype(BF16)
    c2 = lambda i: (0, 0)
    c3 = lambda i: (0, 0, 0)
    return pl.pallas_call(
        functools.partial(_s5_kernel, tm=tm, tiles_per_seq=seq // tm),
        grid=(t // tm,),
        in_specs=[pl.BlockSpec((tm, d), lambda i: (i, 0)),
                  pl.BlockSpec((1, d), c2),
                  pl.BlockSpec(wbr.shape, c3), pl.BlockSpec(wbi.shape, c3),
                  pl.BlockSpec(akr.shape, c3), pl.BlockSpec(aki.shape, c3),
                  pl.BlockSpec(p_re.shape, c2), pl.BlockSpec(p_im.shape, c2),
                  pl.BlockSpec(wcr.shape, c3), pl.BlockSpec(wci.shape, c3),
                  pl.BlockSpec((1, d), c2),
                  pl.BlockSpec((d, 2 * d), c2),
                  pl.BlockSpec((1, 2 * d), c2)],
        out_specs=pl.BlockSpec((tm, d), lambda i: (i, 0)),
        out_shape=jax.ShapeDtypeStruct((t, d), F32),
        scratch_shapes=[pltpu.VMEM((tm, S5_CH), F32), pltpu.VMEM((tm, S5_CH), F32),
                        pltpu.VMEM((SUBLANES, S5_CH), F32), pltpu.VMEM((SUBLANES, S5_CH), F32)],
        compiler_params=_cparams("arbitrary"),
        name="s5",
    )(h, g.reshape(1, d), wbr, wbi, akr, aki, p_re, p_im, wcr, wci,
      d_skip.astype(F32).reshape(1, d), w_glu.astype(BF16), b_glu.astype(F32).reshape(1, 2 * d))


DIL_CLASSES = 16
DIL_STEPS = 2
DIL_TILE = 256


def _dil_kernel(*refs, seq):
    ncls = DIL_CLASSES
    u_len = seq // ncls
    blk = DIL_BLOCK
    qp_ref, kp_ref, vp_ref, o_ref = refs[:4]
    acc_ref, lsw_ref, m_ref, msw_ref = refs[4:8]
    n_chain = DIL_STEPS * HEADS_PER_TILE
    bufs = refs[8:]
    buf_a, buf_b = bufs[:n_chain], bufs[n_chain:]

    lane = lax.broadcasted_iota(jnp.int32, (1, LANES), 1)
    low = lane < HEAD_DIM
    n_iter = seq // blk // DIL_STEPS

    for pat, (window, dil) in enumerate(DIL_PATTERNS):
        assert window // dil == blk and ncls % dil == 0 and seq % (dil * blk) == 0
        n_run = ncls // dil
        run = SUBLANES * dil
        n_blk = u_len // run
        jq = lax.broadcasted_iota(jnp.int32, (blk, blk), 0)
        jk = lax.broadcasted_iota(jnp.int32, (blk, blk), 1)
        wq = (jq % run) * n_run + jq // run
        wk = (jk % run) * n_run + jk // run
        mask_cur = wk <= wq
        mask_prev = wk >= wq

        def run_rows(step, dil=dil, n_run=n_run, run=run, n_blk=n_blk):
            res = step // n_blk
            bi = step % n_blk
            base = [(res + dil * c) * u_len for c in range(n_run)]
            cur = [pl.multiple_of(b + run * bi, SUBLANES) for b in base]
            prev = [pl.multiple_of(b + run * jnp.maximum(bi - 1, 0), SUBLANES) for b in base]
            return cur, prev, bi > 0

        def gather(ref, offs, run=run):
            parts = [ref[pl.ds(o, run), :] for o in offs]
            return jnp.concatenate(parts, axis=0) if len(parts) > 1 else parts[0]

        def scores(step, hh):
            cur, prev, _ = run_rows(step)
            q = gather(qp_ref, cur)
            q = jnp.where(low if hh == 0 else jnp.logical_not(low), q, 0.0).astype(BF16)
            kcat = jnp.concatenate([gather(kp_ref, prev), gather(kp_ref, cur)], axis=0).astype(BF16)
            return lax.dot_general(q, kcat, NT_DIMS, preferred_element_type=F32)

        def phase(it, bufs, nxt_bufs, pat=pat):
            nxt = jnp.minimum(it + 1, n_iter - 1)
            ahead = [(t, hh) for t in range(DIL_STEPS) for hh in range(HEADS_PER_TILE)]

            def stage(i):
                for t2, hh2 in ahead[i:i + 1]:
                    nxt_bufs[t2 * HEADS_PER_TILE + hh2][...] = scores(nxt * DIL_STEPS + t2, hh2)

            stage(0)
            stage(1)
            for t in range(DIL_STEPS):
                step = it * DIL_STEPS + t
                cur, prev, has_prev = run_rows(step)
                vcat = jnp.concatenate([gather(vp_ref, prev), gather(vp_ref, cur)], axis=0)
                mask = jnp.concatenate([jnp.logical_and(mask_prev, has_prev), mask_cur], axis=1)
                ms, os_ = [], []
                for hh in range(HEADS_PER_TILE):
                    c = t * HEADS_PER_TILE + hh
                    s = jnp.where(mask, bufs[c][...], NEG)
                    mb = jnp.max(s, axis=1, keepdims=True)
                    p = jnp.exp2(s - mb).astype(BF16)
                    vh = jnp.where(low if hh == 0 else jnp.logical_not(low), vcat, 1.0).astype(BF16)
                    os_.append(jnp.dot(p, vh, preferred_element_type=F32))
                    ms.append(jnp.broadcast_to(mb, (blk, LANES)))
                    stage(c + 2)
                m_b = jnp.where(low, ms[0], ms[1])
                msw_b = jnp.where(low, ms[1], ms[0])
                o_b = jnp.where(low, os_[0], os_[1])
                lsw_b = jnp.where(low, os_[1], os_[0])
                for ci, off in enumerate(cur):
                    rows = pl.ds(off, run)
                    piece = slice(ci * run, (ci + 1) * run)
                    if pat == 0:
                        m_ref[rows, :] = m_b[piece]
                        msw_ref[rows, :] = msw_b[piece]
                        acc_ref[rows, :] = o_b[piece]
                        lsw_ref[rows, :] = lsw_b[piece]
                    else:
                        m_o, msw_o = m_ref[rows, :], msw_ref[rows, :]
                        m_n = jnp.maximum(m_o, m_b[piece])
                        msw_n = jnp.maximum(msw_o, msw_b[piece])
                        acc_n = acc_ref[rows, :] * jnp.exp2(m_o - m_n) + o_b[piece] * jnp.exp2(m_b[piece] - m_n)
                        lsw_n = (lsw_ref[rows, :] * jnp.exp2(msw_o - msw_n)
                                 + lsw_b[piece] * jnp.exp2(msw_b[piece] - msw_n))
                        if pat + 1 < len(DIL_PATTERNS):
                            m_ref[rows, :] = m_n
                            msw_ref[rows, :] = msw_n
                            acc_ref[rows, :] = acc_n
                            lsw_ref[rows, :] = lsw_n
                        else:
                            den = pltpu.roll(lsw_n, HEAD_DIM, 1)
                            o_ref[rows, :] = (acc_n / den).astype(o_ref.dtype)

        def two_phases(i, carry, phase=phase):
            phase(2 * i, buf_a, buf_b)
            phase(2 * i + 1, buf_b, buf_a)
            return carry

        for t in range(DIL_STEPS):
            for hh in range(HEADS_PER_TILE):
                buf_a[t * HEADS_PER_TILE + hh][...] = scores(t, hh)
        lax.fori_loop(0, n_iter // 2, two_phases, 0)


def _dilated(qkv, bsz, seq):
    ncls = DIL_CLASSES
    assert seq % (ncls * DIL_BLOCK) == 0 and (seq // DIL_BLOCK) % (2 * DIL_STEPS) == 0 and seq % 256 == 0
    u_len = seq // ncls
    n_tiles = D_MODEL // LANES
    qkv3 = qkv.reshape(bsz, seq, 3 * D_MODEL)
    spec = lambda off: pl.BlockSpec((None, seq, LANES), lambda b, hp, off=off: (b, 0, off + hp))
    in_specs = [spec(off) for off in (0, n_tiles, 2 * n_tiles)]
    out = pl.pallas_call(
        functools.partial(_dil_kernel, seq=seq),
        grid=(bsz, n_tiles),
        in_specs=in_specs,
        out_specs=pl.BlockSpec((None, seq, LANES), lambda b, hp: (b, 0, hp)),
        out_shape=jax.ShapeDtypeStruct((bsz, seq, D_MODEL), BF16),
        scratch_shapes=[pltpu.VMEM((seq, LANES), F32)] * 4
                       + [pltpu.VMEM((DIL_BLOCK, 2 * DIL_BLOCK), F32)] * (2 * DIL_STEPS * HEADS_PER_TILE),
        compiler_params=_cparams("parallel", "parallel"),
        name="dilated_attn",
    )(*([qkv3] * len(in_specs)))
    out = out.reshape(bsz, ncls, u_len, D_MODEL).transpose(0, 2, 1, 3)
    return out.reshape(bsz * seq, D_MODEL)


def _hgrn_kernel(q_ref, f_ref, i_ref, g_ref, lb_ref, ng_ref, o_ref, st_ref, *, tc):
    @pl.when(pl.program_id(2) == 0)
    def _():
        st_ref[...] = jnp.zeros_like(st_ref)

    chunk, sub = HGRN_CHUNK, HGRN_SUB
    n_ch = tc // chunk
    n_sub = chunk // sub
    r = lax.broadcasted_iota(jnp.int32, (chunk, chunk), 0)
    c = lax.broadcasted_iota(jnp.int32, (chunk, chunk), 1)
    tril = (c <= r).astype(F32)
    trow = lax.broadcasted_iota(jnp.int32, (sub, 1), 0)
    lb = lb_ref[...]
    rows = [slice(ch * chunk, (ch + 1) * chunk) for ch in range(n_ch)]

    qs = [_silu(q_ref[rw, :]) for rw in rows]
    fg = [lb + (1.0 - lb) * jax.nn.sigmoid(f_ref[rw, :]) for rw in rows]
    kk = [1.0 - x for x in fg]
    iv = [i_ref[rw, :] for rw in rows]
    ivb = [x.astype(BF16) for x in iv]
    bc = [jnp.dot(tril, jnp.log(x), preferred_element_type=F32, precision=lax.Precision.HIGHEST) for x in fg]
    bl = [x[chunk - 1:chunk] for x in bc]

    upd = [lax.dot_general(ivb[ch], (kk[ch] * jnp.exp(bl[ch] - bc[ch])).astype(BF16), TN_DIMS,
                           preferred_element_type=F32) for ch in range(n_ch)]
    att = []
    for ch in range(n_ch):
        row_att = []
        for blk in range(1, n_sub):
            lo = blk * sub
            r_i = bc[ch][lo - 1:lo]
            qt = (qs[ch][lo:lo + sub] * jnp.exp(bc[ch][lo:lo + sub] - r_i)).astype(BF16)
            kt = (kk[ch][:lo] * jnp.exp(r_i - bc[ch][:lo])).astype(BF16)
            row_att.append(lax.dot_general(qt, kt, NT_DIMS, preferred_element_type=F32).astype(BF16))
        att.append(row_att)

    st_t = st_ref[...]
    states = []
    for ch in range(n_ch):
        states.append(st_t.astype(BF16))
        st_t = st_t * jnp.exp(bl[ch]) + upd[ch]
    st_ref[...] = st_t

    o_inter = [lax.dot_general((qs[ch] * jnp.exp(bc[ch])).astype(BF16), states[ch], NT_DIMS,
                               preferred_element_type=F32) for ch in range(n_ch)]
    o_off = [[jnp.dot(att[ch][blk - 1], ivb[ch][:blk * sub], preferred_element_type=F32)
              for blk in range(1, n_sub)] for ch in range(n_ch)]

    for ch in range(n_ch):
        outs = []
        for blk in range(n_sub):
            lo = blk * sub
            o_i = o_inter[ch][lo:lo + sub]
            if blk > 0:
                o_i = o_i + o_off[ch][blk - 1]
            q_i, k_i, iv_i, fg_i = (x[ch][lo:lo + sub] for x in (qs, kk, iv, fg))
            u = jnp.zeros_like(q_i)
            for s in reversed(range(sub)):
                decayed = u * fg_i[s + 1:s + 2] if s + 1 < sub else u
                u = jnp.where(trow == s, q_i, decayed)
                a_col = jnp.sum(u * k_i[s:s + 1], axis=1, keepdims=True)
                o_i = o_i + a_col * iv_i[s:s + 1]
            outs.append(o_i)
        o = jnp.concatenate(outs, axis=0)
        o = o * lax.rsqrt(jnp.mean(o * o, axis=-1, keepdims=True) + RMS_EPS) * ng_ref[...]
        o_ref[rows[ch], :] = (o * _silu(g_ref[rows[ch], :])).astype(o_ref.dtype)


def _hgrn(proj, lb, norm_g, bsz, seq, tc=1024):
    t = bsz * seq
    tiles = seq // tc
    hd = HGRN_HEAD_DIM
    spec = lambda off: pl.BlockSpec((tc, hd), lambda b, h, j, off=off: (b * tiles + j, off + h))
    return pl.pallas_call(
        functools.partial(_hgrn_kernel, tc=tc),
        grid=(bsz, HGRN_HEADS, tiles),
        in_specs=[spec(0), spec(HGRN_HEADS), spec(2 * HGRN_HEADS), spec(3 * HGRN_HEADS),
                  pl.BlockSpec((1, hd), lambda b, h, j: (0, h)),
                  pl.BlockSpec((1, hd), lambda b, h, j: (0, 0))],
        out_specs=spec(0),
        out_shape=jax.ShapeDtypeStruct((t, D_MODEL), BF16),
        scratch_shapes=[pltpu.VMEM((hd, hd), F32)],
        compiler_params=_cparams("parallel", "parallel", "arbitrary"),
        name="hgrn2",
    )(proj, proj, proj, proj, lb.reshape(1, D_MODEL), norm_g.astype(F32).reshape(1, hd))


MOBA_SUPER = 2
MOBA_Q_TILES = 4
MOBA_VT_ROWS = HEAD_DIM + 16


def _moba_kernel(q_ref, k_ref, v_ref, o_ref, km_ref, kp_ref, vt_ref, qp_ref, acc_ref, *s_refs, n_blk, n_sel):
    blk = MOBA_BLOCK
    sbk = MOBA_SUPER * blk
    lane = lax.broadcasted_iota(jnp.int32, (1, LANES), 1)
    head_of_lane = lane // HEAD_DIM
    hot_lane0 = [(1 - hh) * HEAD_DIM for hh in range(HEADS_PER_TILE)]
    n_rows = HEADS_PER_TILE * n_blk

    ones_row = (lax.broadcasted_iota(jnp.int32, (MOBA_VT_ROWS - HEAD_DIM, 1), 0) == 0).astype(BF16)
    for hh in range(HEADS_PER_TILE):
        vt_ref[hh * MOBA_VT_ROWS + HEAD_DIM:(hh + 1) * MOBA_VT_ROWS, :] = jnp.broadcast_to(
            ones_row, (MOBA_VT_ROWS - HEAD_DIM, n_blk * blk))
    for n in range(n_blk):
        rows = slice(n * blk, (n + 1) * blk)
        kb = k_ref[rows, :]
        v_t = v_ref[rows, :].T
        for hh in range(HEADS_PER_TILE):
            vt_ref[hh * MOBA_VT_ROWS:hh * MOBA_VT_ROWS + HEAD_DIM, rows] = (
                v_t[hh * HEAD_DIM:(hh + 1) * HEAD_DIM].astype(BF16))
        km = jnp.sum(kb, axis=0, keepdims=True) * (1.0 / blk)
        for hh in range(HEADS_PER_TILE):
            mine = head_of_lane == hh
            km_ref[hh * n_blk + n:hh * n_blk + n + 1, :] = jnp.where(mine, km, 0.0)
            hot = (lane == hot_lane0[hh] + n).astype(F32)
            kp_ref[rows, hh * LANES:(hh + 1) * LANES] = jnp.where(mine, kb, hot).astype(BF16)

    prow = lax.broadcasted_iota(jnp.int32, (n_rows, HEADS_PER_TILE * LANES), 0)
    pcol = lax.broadcasted_iota(jnp.int32, (n_rows, HEADS_PER_TILE * LANES), 1)
    target = jnp.zeros_like(prow)
    for hh in range(HEADS_PER_TILE):
        target = jnp.where(prow // n_blk == hh, hh * LANES + hot_lane0[hh] + prow % n_blk, target)
    place = (pcol == target).astype(BF16)
    nid = lax.broadcasted_iota(jnp.int32, (n_blk, blk), 0)
    nid_f = nid.astype(F32)
    km_all = km_ref[...]

    q_rows = [slice(qi * blk, (qi + 1) * blk) for qi in range(n_blk)]
    gates = [lax.dot_general(km_all, q_ref[rw, :], NT_DIMS, preferred_element_type=F32,
                             precision=lax.Precision.HIGHEST) for rw in q_rows]
    biases = []
    for qi in range(n_blk):
        past = nid < qi
        bias_rows = []
        for hh in range(HEADS_PER_TILE):
            g = jnp.where(past, gates[qi][hh * n_blk:(hh + 1) * n_blk], NEG)
            keep = nid == qi
            for _ in range(n_sel):
                mx = jnp.max(g, axis=0, keepdims=True)
                idx = jnp.min(jnp.where(g == mx, nid_f, float(n_blk)), axis=0, keepdims=True)
                pick = nid_f == idx
                keep = jnp.logical_or(keep, jnp.logical_and(pick, past))
                g = jnp.where(pick, BELOW_NEG, g)
            bias_rows.append(jnp.where(keep, 0.0, NEG))
        biases.append(jnp.concatenate(bias_rows, axis=0).astype(BF16))
    bias_qs = [lax.dot_general(b, place, TN_DIMS, preferred_element_type=F32) for b in biases]
    for qi in range(n_blk):
        q = q_ref[q_rows[qi], :]
        for hh in range(HEADS_PER_TILE):
            qp_ref[q_rows[qi], hh * LANES:(hh + 1) * LANES] = jnp.where(
                head_of_lane == hh, q, bias_qs[qi][:, hh * LANES:(hh + 1) * LANES]).astype(BF16)

    krow = lax.broadcasted_iota(jnp.int32, (sbk, blk), 0)
    qcol = lax.broadcasted_iota(jnp.int32, (sbk, blk), 1)
    rel = qcol - krow

    n_qt = MOBA_Q_TILES
    assert n_qt == 2 * MOBA_SUPER
    n_chain = n_qt * HEADS_PER_TILE
    buf_a, buf_b = s_refs[:n_chain], s_refs[n_chain:]
    all_tiles = tuple(range(n_qt))

    def q_operands(g):
        q0 = pl.multiple_of(g * n_qt * blk, n_qt * blk)
        return [[qp_ref[pl.ds(q0 + t * blk, blk), hh * LANES:(hh + 1) * LANES] for hh in range(HEADS_PER_TILE)]
                for t in range(n_qt)]

    def stage(bufs, k0, qps, t, hh):
        kpn = kp_ref[pl.ds(k0, sbk), hh * LANES:(hh + 1) * LANES]
        bufs[t * HEADS_PER_TILE + hh][...] = lax.dot_general(kpn, qps[t][hh], NT_DIMS, preferred_element_type=F32)

    n_groups = n_blk // n_qt

    def q_group(g, carry):
        q0 = pl.multiple_of(g * n_qt * blk, n_qt * blk)
        qps = q_operands(g)
        acc_ref[...] = jnp.zeros_like(acc_ref)

        def step(stats, cur, k0, tiles, causal, nxt=None, nxt_tiles=(), nxt_k0=None, nxt_q=qps):
            chains = [(t, hh) for t in tiles for hh in range(HEADS_PER_TILE)]
            ahead = [(t, hh) for t in nxt_tiles for hh in range(HEADS_PER_TILE)]
            nxt_k0 = k0 + sbk if nxt_k0 is None else nxt_k0
            v_t = [vt_ref[hh * MOBA_VT_ROWS:(hh + 1) * MOBA_VT_ROWS, pl.ds(k0, sbk)] for hh in range(HEADS_PER_TILE)]
            stats = list(stats)
            for t, hh in ahead[:2]:
                stage(nxt, nxt_k0, nxt_q, t, hh)
            for i, (t, hh) in enumerate(chains):
                c = t * HEADS_PER_TILE + hh
                m_o = stats[c]
                s_t = cur[c][...]
                if t in causal:
                    s_t = jnp.where(rel >= causal[t], s_t, NEG)
                m_n = jnp.maximum(m_o, jnp.max(s_t, axis=0, keepdims=True))
                p = jnp.exp2(s_t - m_n).astype(BF16)
                acc_ref[c] = jnp.exp2(m_o - m_n) * acc_ref[c] + jnp.dot(v_t[hh], p, preferred_element_type=F32)
                stats[c] = m_n
                for t2, hh2 in ahead[i + 2:i + 3]:
                    stage(nxt, nxt_k0, nxt_q, t2, hh2)
            for t2, hh2 in ahead[len(chains) + 2:]:
                stage(nxt, nxt_k0, nxt_q, t2, hh2)
            return tuple(stats)

        def pair(i, stats):
            k0 = pl.multiple_of(i * 2 * sbk, 2 * sbk)
            stats = step(stats, buf_a, k0, all_tiles, {}, buf_b, all_tiles)
            return step(stats, buf_b, k0 + sbk, all_tiles, {}, buf_a, all_tiles)

        init = (jnp.full((1, blk), BELOW_NEG, F32),) * n_chain
        stats = lax.fori_loop(0, g, pair, init)
        late = all_tiles[MOBA_SUPER:]
        stats = step(stats, buf_a, q0, all_tiles, {t: -t * blk for t in all_tiles[:MOBA_SUPER]}, buf_b, late)
        step(stats, buf_b, q0 + sbk, late, {t: -(t - MOBA_SUPER) * blk for t in late},
             buf_a, all_tiles, 0, q_operands(jnp.minimum(g + 1, n_groups - 1)))
        for t in all_tiles:
            accs = [acc_ref[t * HEADS_PER_TILE + hh] for hh in range(HEADS_PER_TILE)]
            o_t = jnp.concatenate([a[:HEAD_DIM] / a[HEAD_DIM:HEAD_DIM + 1] for a in accs], axis=0)
            o_ref[pl.ds(q0 + t * blk, blk), :] = o_t.T.astype(o_ref.dtype)
        return carry

    first_q = q_operands(0)
    for t in all_tiles:
        for hh in range(HEADS_PER_TILE):
            stage(buf_a, 0, first_q, t, hh)
    lax.fori_loop(0, n_groups, q_group, 0)


def _moba(qkv, bsz, seq):
    blk = MOBA_BLOCK
    assert seq % (blk * MOBA_Q_TILES) == 0
    n_chain = MOBA_Q_TILES * HEADS_PER_TILE
    n_blk = seq // blk
    n_sel = min(MOBA_TOPK, n_blk - 1)
    n_tiles = D_MODEL // LANES
    qkv3 = qkv.reshape(bsz, seq, 3 * D_MODEL)
    spec = lambda off: pl.BlockSpec((None, seq, LANES), lambda b, hp, off=off: (b, 0, off + hp))
    out = pl.pallas_call(
        functools.partial(_moba_kernel, n_blk=n_blk, n_sel=n_sel),
        grid=(bsz, n_tiles),
        in_specs=[spec(0), spec(n_tiles), spec(2 * n_tiles)],
        out_specs=pl.BlockSpec((None, seq, LANES), lambda b, hp: (b, 0, hp)),
        out_shape=jax.ShapeDtypeStruct((bsz, seq, D_MODEL), BF16),
        scratch_shapes=[pltpu.VMEM((HEADS_PER_TILE * n_blk, LANES), F32),
                        pltpu.VMEM((seq, HEADS_PER_TILE * LANES), BF16),
                        pltpu.VMEM((HEADS_PER_TILE * MOBA_VT_ROWS, seq), BF16),
                        pltpu.VMEM((seq, HEADS_PER_TILE * LANES), BF16),
                        pltpu.VMEM((n_chain, MOBA_VT_ROWS, blk), F32)]
                       + [pltpu.VMEM((MOBA_SUPER * blk, blk), F32)] * (2 * n_chain),
        compiler_params=_cparams("parallel", "parallel"),
        name="moba_attn",
    )(qkv3, qkv3, qkv3)
    return out.reshape(bsz * seq, D_MODEL)


def _rope_tables(positions):
    half = ROPE_DIM // 2
    inv = ROPE_THETA ** (-jnp.arange(0, ROPE_DIM, 2, dtype=F32) / ROPE_DIM)
    ang = positions.astype(F32).reshape(-1)[:, None] * inv
    cos_f, sin_f = jnp.cos(ang), jnp.sin(ang)
    j = jnp.arange(LANES, dtype=jnp.int32) % HEAD_DIM
    pick = jnp.logical_and(j[None, :] % half == jnp.arange(half)[:, None], j[None, :] < ROPE_DIM).astype(F32)
    cos = sum(cos_f[:, f:f + 1] * pick[f][None, :] for f in range(half))
    sin = sum(sin_f[:, f:f + 1] * pick[f][None, :] for f in range(half))
    cos = jnp.where(j < ROPE_DIM, cos, 1.0)
    s1 = jnp.where(j < half, -sin, 0.0)
    s2 = jnp.where(jnp.logical_and(j >= half, j < ROPE_DIM), sin, 0.0)
    return cos, s1, s2


def _qkv_weight(w):
    q_scale = HEAD_DIM ** -0.5 * math.log2(math.e)
    scale = jnp.concatenate([jnp.full((D_MODEL,), q_scale, F32), jnp.ones((2 * D_MODEL,), F32)])
    return (w.astype(F32) * scale[None, :]).astype(BF16)


def kernel(x, positions, norm_mix, norm_ffn, norm_final, s5_a_re, s5_a_im, s5_log_dt, s5_b_re, s5_b_im, s5_c_re, s5_c_im, s5_d, s5_w_glu, s5_b_glu, dil_w_qkv, dil_w_o, hgrn_w_in, hgrn_lower_bound, hgrn_norm, hgrn_w_o, moba_w_qkv, moba_w_o, ffn_w_gate_up, ffn_w_down):
    bsz, seq, d = x.shape
    depth = norm_mix.shape[0]
    n_mixers = 4
    t = bsz * seq
    rope = _rope_tables(positions)
    per = DIL_TILE // DIL_CLASSES
    pos_cm = positions.reshape(bsz, seq // DIL_TILE, per, DIL_CLASSES).transpose(0, 1, 3, 2)
    rope_cm = _rope_tables(pos_cm)
    wgu = ffn_w_gate_up.astype(BF16)
    wdn = ffn_w_down.astype(BF16)
    lb_w = jax.nn.softmax(hgrn_lower_bound.astype(F32), axis=0)
    lower_bounds = jnp.cumsum(lb_w, axis=0) - lb_w[0]
    norm_mix = norm_mix.astype(F32)
    norm_ffn = norm_ffn.astype(F32)

    h = x.reshape(t, d).astype(F32)
    for layer in range(depth):
        mixer, j = layer % n_mixers, layer // n_mixers
        final_g = norm_final.astype(F32) if layer == depth - 1 else None
        mix = None
        if mixer == 0:
            h = _s5(h, norm_mix[layer], s5_a_re[j], s5_a_im[j], s5_log_dt[j], s5_b_re[j], s5_b_im[j],
                    s5_c_re[j], s5_c_im[j], s5_d[j], s5_w_glu[j], s5_b_glu[j], seq)
        elif mixer == 1:
            qkv = _proj(h, norm_mix[layer], _qkv_weight(dil_w_qkv[j]), rope_cm, 2 * D_MODEL,
                        class_major=(bsz, seq, DIL_CLASSES), tm=DIL_TILE)
            mix = (_dilated(qkv, bsz, seq), dil_w_o[j].astype(BF16))
        elif mixer == 2:
            proj = _proj(h, norm_mix[layer], hgrn_w_in[j].astype(BF16), tm=512)
            mix = (_hgrn(proj, lower_bounds[layer], hgrn_norm[j], bsz, seq), hgrn_w_o[j].astype(BF16))
        else:
            qkv = _proj(h, norm_mix[layer], _qkv_weight(moba_w_qkv[j]), rope, 2 * D_MODEL, tm=512)
            mix = (_moba(qkv, bsz, seq), moba_w_o[j].astype(BF16))
        h = _ffn(h, norm_ffn[layer], wgu, wdn, layer, mix=mix, final_g=final_g)
    return h.reshape(bsz, seq, d).astype(x.dtype)
```

```python
import functools
import math

import jax
import jax.numpy as jnp
from jax import lax
from jax.experimental import pallas as pl
from jax.experimental.pallas import tpu as pltpu

F32 = jnp.float32
BF16 = jnp.bfloat16

D_MODEL = 1024
D_FF = 2816
RMS_EPS = 1e-6
NEG = -1e30
BELOW_NEG = -3e38

HEAD_DIM = 64
ROPE_DIM = HEAD_DIM // 4
ROPE_THETA = 500000.0
LANES = 128
SUBLANES = 8
HEADS_PER_TILE = LANES // HEAD_DIM

S5_GROUP = 16
S5_GROUPS = D_MODEL // S5_GROUP
S5_STATE = 64
S5_CH = S5_GROUPS * S5_STATE
S5_SLABS = D_MODEL // LANES
S5_SLAB_CH = S5_CH // S5_SLABS
S5_SCAN_SHIFTS = (2, 4)

DIL_PATTERNS = ((128, 1), (512, 4), (2048, 16))
DIL_BLOCK = 128

HGRN_HEAD_DIM = 128
HGRN_HEADS = D_MODEL // HGRN_HEAD_DIM
HGRN_CHUNK = 64
HGRN_SUB = 8

MOBA_BLOCK = 256
MOBA_TOPK = 3

VMEM_LIMIT = 56 * 1024 * 1024

NT_DIMS = (((1,), (1,)), ((), ()))
TN_DIMS = (((0,), (0,)), ((), ()))


def _cparams(*sem):
    return pltpu.CompilerParams(dimension_semantics=sem, vmem_limit_bytes=VMEM_LIMIT)


def _rms(x, g):
    return x * lax.rsqrt(jnp.mean(x * x, axis=-1, keepdims=True) + RMS_EPS) * g


def _silu(x):
    return x * jax.nn.sigmoid(x)


def _proj_kernel(*refs, n_out, tn, rope_cols, n_cls):
    if rope_cols:
        x_ref, g_ref, w_ref, c_ref, s1_ref, s2_ref, o_ref = refs
    else:
        x_ref, g_ref, w_ref, o_ref = refs
    tm = x_ref.shape[0]
    per = tm // n_cls
    xn = _rms(x_ref[...], g_ref[...]).astype(BF16)
    if n_cls > 1:
        dst = lax.broadcasted_iota(jnp.int32, (tm, tm), 0)
        src = lax.broadcasted_iota(jnp.int32, (tm, tm), 1)
        perm = (src == (dst % per) * n_cls + dst // per).astype(BF16)
        xn = jnp.dot(perm, xn, preferred_element_type=F32).astype(BF16)
    if rope_cols:
        cos, s1, s2 = c_ref[...], s1_ref[...], s2_ref[...]
    for c in range(n_out // tn):
        y = jnp.dot(xn, w_ref[:, c * tn:(c + 1) * tn], preferred_element_type=F32)
        if c * tn < rope_cols:
            parts = []
            for j in range(tn // LANES):
                yj = y[:, j * LANES:(j + 1) * LANES]
                half = ROPE_DIM // 2
                parts.append(yj * cos + pltpu.roll(yj, LANES - half, 1) * s1 + pltpu.roll(yj, half, 1) * s2)
            y = jnp.concatenate(parts, axis=1) if len(parts) > 1 else parts[0]
        if n_cls == 1:
            o_ref[:, c * tn:(c + 1) * tn] = y
        else:
            for r in range(n_cls):
                o_ref[r, :, c * tn:(c + 1) * tn] = y[r * per:(r + 1) * per]


def _proj(h, g, w, rope=None, rope_cols=0, class_major=None, tm=256, tn=256):
    t, d = h.shape
    n_out = w.shape[1]
    in_specs = [pl.BlockSpec((tm, d), lambda i: (i, 0)),
                pl.BlockSpec((1, d), lambda i: (0, 0)),
                pl.BlockSpec((d, n_out), lambda i: (0, 0))]
    args = [h, g.reshape(1, d), w]
    if rope_cols:
        in_specs += [pl.BlockSpec((tm, LANES), lambda i: (i, 0))] * 3
        args += list(rope)
    if class_major is None:
        n_cls = 1
        out_spec = pl.BlockSpec((tm, n_out), lambda i: (i, 0))
        out_shape = jax.ShapeDtypeStruct((t, n_out), F32)
    else:
        bsz, seq, n_cls = class_major
        tiles = seq // tm
        per = tm // n_cls
        assert seq % tm == 0 and per % SUBLANES == 0
        out_spec = pl.BlockSpec((None, n_cls, per, n_out), lambda i: (i // tiles, 0, i % tiles, 0))
        out_shape = jax.ShapeDtypeStruct((bsz, n_cls, seq // n_cls, n_out), F32)
    return pl.pallas_call(
        functools.partial(_proj_kernel, n_out=n_out, tn=tn, rope_cols=rope_cols, n_cls=n_cls),
        grid=(t // tm,),
        in_specs=in_specs,
        out_specs=out_spec,
        out_shape=out_shape,
        compiler_params=_cparams("parallel"),
        name="proj_rope" if rope_cols else "proj",
    )(*args)


def _ffn_kernel(*refs, has_mix, has_final, fc):
    refs = list(refs)
    h_ref, g_ref, wgu_ref, wd_ref = refs[:4]
    pos = 4
    if has_mix:
        a_ref, wo_ref = refs[pos:pos + 2]
        pos += 2
    if has_final:
        fg_ref = refs[pos]
        pos += 1
    o_ref, acc_ref = refs[pos], refs[pos + 1]

    h = h_ref[...]
    if has_mix:
        h = h + jnp.dot(a_ref[...], wo_ref[...], preferred_element_type=F32)
    xn = _rms(h, g_ref[...]).astype(BF16)

    def gate_up(c):
        gate = jnp.dot(xn, wgu_ref[:, c * fc:(c + 1) * fc], preferred_element_type=F32)
        up = jnp.dot(xn, wgu_ref[:, D_FF + c * fc:D_FF + (c + 1) * fc], preferred_element_type=F32)
        return gate, up

    n_chunks = D_FF // fc
    nxt = gate_up(0)
    for c in range(n_chunks):
        gate, up = nxt
        if c + 1 < n_chunks:
            nxt = gate_up(c + 1)
        act = (_silu(gate) * up).astype(BF16)
        contrib = jnp.dot(act, wd_ref[c * fc:(c + 1) * fc, :], preferred_element_type=F32)
        if c == 0:
            acc_ref[...] = contrib
        else:
            acc_ref[...] += contrib
    out = h + acc_ref[...]
    if has_final:
        out = _rms(out, fg_ref[...])
    o_ref[...] = out


def _ffn(h, g, wgu, wd, layer, mix=None, final_g=None, tm=512, fc=256):
    t, d = h.shape
    const = lambda i: (0, 0)
    in_specs = [pl.BlockSpec((tm, d), lambda i: (i, 0)),
                pl.BlockSpec((1, d), const),
                pl.BlockSpec((None, d, 2 * D_FF), lambda i: (layer, 0, 0)),
                pl.BlockSpec((None, D_FF, d), lambda i: (layer, 0, 0))]
    args = [h, g.reshape(1, d), wgu, wd]
    if mix is not None:
        a, wo = mix
        in_specs += [pl.BlockSpec((tm, d), lambda i: (i, 0)), pl.BlockSpec((d, d), const)]
        args += [a, wo]
    if final_g is not None:
        in_specs += [pl.BlockSpec((1, d), const)]
        args += [final_g.reshape(1, d)]
    return pl.pallas_call(
        functools.partial(_ffn_kernel, has_mix=mix is not None, has_final=final_g is not None, fc=fc),
        grid=(t // tm,),
        in_specs=in_specs,
        out_specs=pl.BlockSpec((tm, d), lambda i: (i, 0)),
        out_shape=jax.ShapeDtypeStruct((t, d), F32),
        scratch_shapes=[pltpu.VMEM((tm, d), F32)],
        compiler_params=_cparams("parallel"),
        name="ffn",
    )(*args)


def _s5_kernel(x_ref, g_ref, wbr_ref, wbi_ref, akr_ref, aki_ref, pr_ref, pi_ref, wcr_ref, wci_ref,
               d_ref, wglu_ref, bglu_ref, o_ref, er_ref, ei_ref, cr_ref, ci_ref, *, tm, tiles_per_seq):
    i = pl.program_id(0)

    @pl.when(i % tiles_per_seq == 0)
    def _():
        cr_ref[...] = jnp.zeros_like(cr_ref)
        ci_ref[...] = jnp.zeros_like(ci_ref)

    x = x_ref[...]
    u = _rms(x, g_ref[...])
    ub = u.astype(BF16)
    first = lax.broadcasted_iota(jnp.int32, (tm, 1), 0) % SUBLANES == 0
    ub_prev = jnp.where(first, 0.0, pltpu.roll(u, 1, 0)).astype(BF16)
    for s in range(S5_SLABS):
        us = jnp.concatenate([ub[:, s * LANES:(s + 1) * LANES], ub_prev[:, s * LANES:(s + 1) * LANES]], axis=1)
        er_ref[:, s * S5_SLAB_CH:(s + 1) * S5_SLAB_CH] = jnp.dot(us, wbr_ref[s], preferred_element_type=F32)
        ei_ref[:, s * S5_SLAB_CH:(s + 1) * S5_SLAB_CH] = jnp.dot(us, wbi_ref[s], preferred_element_type=F32)

    def group(j, carry):
        r0 = pl.multiple_of(j * SUBLANES, SUBLANES)
        xr = er_ref[pl.ds(r0, SUBLANES), :]
        xi = ei_ref[pl.ds(r0, SUBLANES), :]
        for lvl, k in enumerate(S5_SCAN_SHIFTS):
            sr = pltpu.roll(xr, k, 0)
            si = pltpu.roll(xi, k, 0)
            ar, ai = akr_ref[lvl], aki_ref[lvl]
            xr, xi = xr + ar * sr - ai * si, xi + ar * si + ai * sr
        cr, ci = cr_ref[...], ci_ref[...]
        pr, pi_ = pr_ref[...], pi_ref[...]
        hr = xr + pr * cr - pi_ * ci
        hi = xi + pr * ci + pi_ * cr
        er_ref[pl.ds(r0, SUBLANES), :] = hr
        ei_ref[pl.ds(r0, SUBLANES), :] = hi
        cr_ref[...] = jnp.broadcast_to(hr[SUBLANES - 1:SUBLANES, :], (SUBLANES, S5_CH))
        ci_ref[...] = jnp.broadcast_to(hi[SUBLANES - 1:SUBLANES, :], (SUBLANES, S5_CH))
        return carry

    lax.fori_loop(0, tm // SUBLANES, group, 0)

    ys = []
    for s in range(S5_SLABS):
        hr = er_ref[:, s * S5_SLAB_CH:(s + 1) * S5_SLAB_CH].astype(BF16)
        hi = ei_ref[:, s * S5_SLAB_CH:(s + 1) * S5_SLAB_CH].astype(BF16)
        ys.append(jnp.dot(hr, wcr_ref[s], preferred_element_type=F32)
                  - jnp.dot(hi, wci_ref[s], preferred_element_type=F32))
    y = jnp.concatenate(ys, axis=1) + d_ref[...] * u
    z = jax.nn.gelu(y).astype(BF16)
    zz = jnp.dot(z, wglu_ref[...], preferred_element_type=F32) + bglu_ref[...]
    o_ref[...] = x + zz[:, :D_MODEL] * jax.nn.sigmoid(zz[:, D_MODEL:])


def _block_diag_slabs(w):
    g, r, c = w.shape
    per = g // S5_SLABS
    w = w.reshape(S5_SLABS, per, r, c)
    eye = jnp.eye(per, dtype=w.dtype)
    return jnp.einsum('sgrc,gh->sgrhc', w, eye).reshape(S5_SLABS, per * r, per * c)


def _s5_tables(a_re, a_im, log_dt):
    lr, li = a_re.astype(F32), a_im.astype(F32)
    dt = jnp.exp(log_dt.astype(F32))[:, None]
    mag = jnp.exp(lr * dt)
    ab_re, ab_im = mag * jnp.cos(li * dt), mag * jnp.sin(li * dt)
    den = lr * lr + li * li
    m_re = ab_re - 1.0
    f_re = (m_re * lr + ab_im * li) / den
    f_im = (ab_im * lr - m_re * li) / den

    def power(k):
        return ((jnp.exp(lr * dt * k) * jnp.cos(li * dt * k)).reshape(-1),
                (jnp.exp(lr * dt * k) * jnp.sin(li * dt * k)).reshape(-1))

    rows = jnp.arange(SUBLANES)[:, None]
    akr, aki = [], []
    for k in S5_SCAN_SHIFTS:
        pr, pi_ = power(float(k))
        akr.append(jnp.where(rows >= k, pr[None, :], 0.0))
        aki.append(jnp.where(rows >= k, pi_[None, :], 0.0))
    pw = [power(float(k + 1)) for k in range(SUBLANES)]
    p_re = jnp.stack([p[0] for p in pw])
    p_im = jnp.stack([p[1] for p in pw])
    return f_re, f_im, ab_re, ab_im, jnp.stack(akr), jnp.stack(aki), p_re, p_im


def _s5(h, g, a_re, a_im, log_dt, b_re, b_im, c_re, c_im, d_skip, w_glu, b_glu, seq, tm=512):
    t, d = h.shape
    f_re, f_im, ab_re, ab_im, akr, aki, p_re, p_im = _s5_tables(a_re, a_im, log_dt)
    bw_re = f_re[:, :, None] * b_re - f_im[:, :, None] * b_im
    bw_im = f_re[:, :, None] * b_im + f_im[:, :, None] * b_re
    bw1_re = ab_re[:, :, None] * bw_re - ab_im[:, :, None] * bw_im
    bw1_im = ab_re[:, :, None] * bw_im + ab_im[:, :, None] * bw_re
    slabs = lambda w: _block_diag_slabs(jnp.swapaxes(w, 1, 2))
    wbr = jnp.concatenate([slabs(bw_re), slabs(bw1_re)], axis=1).astype(BF16)
    wbi = jnp.concatenate([slabs(bw_im), slabs(bw1_im)], axis=1).astype(BF16)
    wcr = _block_diag_slabs(jnp.swapaxes(c_re.astype(F32), 1, 2)).astype(BF16)
    wci = _block_diag_slabs(jnp.swapaxes(c_im.astype(F32), 1, 2)).astype(BF16)
    c2 = lambda i: (0, 0)
    c3 = lambda i: (0, 0, 0)
    return pl.pallas_call(
        functools.partial(_s5_kernel, tm=tm, tiles_per_seq=seq // tm),
        grid=(t // tm,),
        in_specs=[pl.BlockSpec((tm, d), lambda i: (i, 0)),
                  pl.BlockSpec((1, d), c2),
                  pl.BlockSpec(wbr.shape, c3), pl.BlockSpec(wbi.shape, c3),
                  pl.BlockSpec(akr.shape, c3), pl.BlockSpec(aki.shape, c3),
                  pl.BlockSpec(p_re.shape, c2), pl.BlockSpec(p_im.shape, c2),
                  pl.BlockSpec(wcr.shape, c3), pl.BlockSpec(wci.shape, c3),
                  pl.BlockSpec((1, d), c2),
                  pl.BlockSpec((d, 2 * d), c2),
                  pl.BlockSpec((1, 2 * d), c2)],
        out_specs=pl.BlockSpec((tm, d), lambda i: (i, 0)),
        out_shape=jax.ShapeDtypeStruct((t, d), F32),
        scratch_shapes=[pltpu.VMEM((tm, S5_CH), F32), pltpu.VMEM((tm, S5_CH), F32),
                        pltpu.VMEM((SUBLANES, S5_CH), F32), pltpu.VMEM((SUBLANES, S5_CH), F32)],
        compiler_params=_cparams("arbitrary"),
        name="s5",
    )(h, g.reshape(1, d), wbr, wbi, akr, aki, p_re, p_im, wcr, wci,
      d_skip.astype(F32).reshape(1, d), w_glu.astype(BF16), b_glu.astype(F32).reshape(1, 2 * d))


DIL_CLASSES = 16
DIL_STEPS = 2
DIL_TILE = 256


def _dil_kernel(*refs, seq):
    ncls = DIL_CLASSES
    u_len = seq // ncls
    blk = DIL_BLOCK
    qp_ref, kp_ref, vp_ref, o_ref = refs[:4]
    acc_ref, lsw_ref, m_ref, msw_ref = refs[4:8]
    n_chain = DIL_STEPS * HEADS_PER_TILE
    bufs = refs[8:]
    buf_a, buf_b = bufs[:n_chain], bufs[n_chain:]

    lane = lax.broadcasted_iota(jnp.int32, (1, LANES), 1)
    low = lane < HEAD_DIM
    n_iter = seq // blk // DIL_STEPS

    for pat, (window, dil) in enumerate(DIL_PATTERNS):
        assert window // dil == blk and ncls % dil == 0 and seq % (dil * blk) == 0
        n_run = ncls // dil
        run = SUBLANES * dil
        n_blk = u_len // run
        jq = lax.broadcasted_iota(jnp.int32, (blk, blk), 0)
        jk = lax.broadcasted_iota(jnp.int32, (blk, blk), 1)
        wq = (jq % run) * n_run + jq // run
        wk = (jk % run) * n_run + jk // run
        mask_cur = wk <= wq
        mask_prev = wk >= wq

        def run_rows(step, dil=dil, n_run=n_run, run=run, n_blk=n_blk):
            res = step // n_blk
            bi = step % n_blk
            base = [(res + dil * c) * u_len for c in range(n_run)]
            cur = [pl.multiple_of(b + run * bi, SUBLANES) for b in base]
            prev = [pl.multiple_of(b + run * jnp.maximum(bi - 1, 0), SUBLANES) for b in base]
            return cur, prev, bi > 0

        def gather(ref, offs, run=run):
            parts = [ref[pl.ds(o, run), :] for o in offs]
            return jnp.concatenate(parts, axis=0) if len(parts) > 1 else parts[0]

        def stage_block(bufs, step, t):
            cur, prev, _ = run_rows(step)
            q = gather(qp_ref, cur)
            kcat = jnp.concatenate([gather(kp_ref, prev), gather(kp_ref, cur)], axis=0).astype(BF16)
            for hh in range(HEADS_PER_TILE):
                qh = jnp.where(low if hh == 0 else jnp.logical_not(low), q, 0.0).astype(BF16)
                bufs[t * HEADS_PER_TILE + hh][...] = lax.dot_general(qh, kcat, NT_DIMS,
                                                                     preferred_element_type=F32)

        def phase(it, bufs, nxt_bufs, pat=pat):
            nxt = jnp.minimum(it + 1, n_iter - 1)
            stage_block(nxt_bufs, nxt * DIL_STEPS, 0)
            for t in range(DIL_STEPS):
                step = it * DIL_STEPS + t
                cur, prev, has_prev = run_rows(step)
                vcat = jnp.concatenate([gather(vp_ref, prev), gather(vp_ref, cur)], axis=0)
                mask = jnp.concatenate([jnp.logical_and(mask_prev, has_prev), mask_cur], axis=1)
                ms, os_ = [], []
                for hh in range(HEADS_PER_TILE):
                    c = t * HEADS_PER_TILE + hh
                    s = jnp.where(mask, bufs[c][...], NEG)
                    mb = jnp.max(s, axis=1, keepdims=True)
                    p = jnp.exp2(s - mb).astype(BF16)
                    vh = jnp.where(low if hh == 0 else jnp.logical_not(low), vcat, 1.0).astype(BF16)
                    os_.append(jnp.dot(p, vh, preferred_element_type=F32))
                    ms.append(jnp.broadcast_to(mb, (blk, LANES)))
                if t + 1 < DIL_STEPS:
                    stage_block(nxt_bufs, nxt * DIL_STEPS + t + 1, t + 1)
                m_b = jnp.where(low, ms[0], ms[1])
                msw_b = jnp.where(low, ms[1], ms[0])
                o_b = jnp.where(low, os_[0], os_[1])
                lsw_b = jnp.where(low, os_[1], os_[0])
                for ci, off in enumerate(cur):
                    rows = pl.ds(off, run)
                    piece = slice(ci * run, (ci + 1) * run)
                    if pat == 0:
                        m_ref[rows, :] = m_b[piece]
                        msw_ref[rows, :] = msw_b[piece]
                        acc_ref[rows, :] = o_b[piece]
                        lsw_ref[rows, :] = lsw_b[piece]
                    else:
                        m_o, msw_o = m_ref[rows, :], msw_ref[rows, :]
                        m_n = jnp.maximum(m_o, m_b[piece])
                        msw_n = jnp.maximum(msw_o, msw_b[piece])
                        acc_n = acc_ref[rows, :] * jnp.exp2(m_o - m_n) + o_b[piece] * jnp.exp2(m_b[piece] - m_n)
                        lsw_n = (lsw_ref[rows, :] * jnp.exp2(msw_o - msw_n)
                                 + lsw_b[piece] * jnp.exp2(msw_b[piece] - msw_n))
                        if pat + 1 < len(DIL_PATTERNS):
                            m_ref[rows, :] = m_n
                            msw_ref[rows, :] = msw_n
                            acc_ref[rows, :] = acc_n
                            lsw_ref[rows, :] = lsw_n
                        else:
                            den = pltpu.roll(lsw_n, HEAD_DIM, 1)
                            o_ref[rows, :] = (acc_n / den).astype(o_ref.dtype)

        def two_phases(i, carry, phase=phase):
            phase(2 * i, buf_a, buf_b)
            phase(2 * i + 1, buf_b, buf_a)
            return carry

        for t in range(DIL_STEPS):
            stage_block(buf_a, t, t)
        lax.fori_loop(0, n_iter // 2, two_phases, 0)


def _dilated(qkv, bsz, seq):
    ncls = DIL_CLASSES
    assert seq % (ncls * DIL_BLOCK) == 0 and (seq // DIL_BLOCK) % (2 * DIL_STEPS) == 0 and seq % 256 == 0
    u_len = seq // ncls
    n_tiles = D_MODEL // LANES
    qkv3 = qkv.reshape(bsz, seq, 3 * D_MODEL)
    spec = lambda off: pl.BlockSpec((None, seq, LANES), lambda b, hp, off=off: (b, 0, off + hp))
    in_specs = [spec(off) for off in (0, n_tiles, 2 * n_tiles)]
    out = pl.pallas_call(
        functools.partial(_dil_kernel, seq=seq),
        grid=(bsz, n_tiles),
        in_specs=in_specs,
        out_specs=pl.BlockSpec((None, seq, LANES), lambda b, hp: (b, 0, hp)),
        out_shape=jax.ShapeDtypeStruct((bsz, seq, D_MODEL), BF16),
        scratch_shapes=[pltpu.VMEM((seq, LANES), F32)] * 4
                       + [pltpu.VMEM((DIL_BLOCK, 2 * DIL_BLOCK), F32)] * (2 * DIL_STEPS * HEADS_PER_TILE),
        compiler_params=_cparams("parallel", "parallel"),
        name="dilated_attn",
    )(*([qkv3] * len(in_specs)))
    out = out.reshape(bsz, ncls, u_len, D_MODEL).transpose(0, 2, 1, 3)
    return out.reshape(bsz * seq, D_MODEL)


def _hgrn_kernel(q_ref, f_ref, i_ref, g_ref, lb_ref, ng_ref, o_ref, st_ref, *, tc):
    @pl.when(pl.program_id(2) == 0)
    def _():
        st_ref[...] = jnp.zeros_like(st_ref)

    chunk, sub = HGRN_CHUNK, HGRN_SUB
    n_ch = tc // chunk
    n_sub = chunk // sub
    r = lax.broadcasted_iota(jnp.int32, (chunk, chunk), 0)
    c = lax.broadcasted_iota(jnp.int32, (chunk, chunk), 1)
    tril = (c <= r).astype(F32)
    trow = lax.broadcasted_iota(jnp.int32, (sub, 1), 0)
    lb = lb_ref[...]
    rows = [slice(ch * chunk, (ch + 1) * chunk) for ch in range(n_ch)]

    qs = [_silu(q_ref[rw, :]) for rw in rows]
    fg = [lb + (1.0 - lb) * jax.nn.sigmoid(f_ref[rw, :]) for rw in rows]
    kk = [1.0 - x for x in fg]
    iv = [i_ref[rw, :] for rw in rows]
    ivb = [x.astype(BF16) for x in iv]
    bc = [jnp.dot(tril, jnp.log(x), preferred_element_type=F32, precision=lax.Precision.HIGHEST) for x in fg]
    bl = [x[chunk - 1:chunk] for x in bc]

    upd = [lax.dot_general(ivb[ch], (kk[ch] * jnp.exp(bl[ch] - bc[ch])).astype(BF16), TN_DIMS,
                           preferred_element_type=F32) for ch in range(n_ch)]
    att = []
    for ch in range(n_ch):
        row_att = []
        for blk in range(1, n_sub):
            lo = blk * sub
            r_i = bc[ch][lo - 1:lo]
            qt = (qs[ch][lo:lo + sub] * jnp.exp(bc[ch][lo:lo + sub] - r_i)).astype(BF16)
            kt = (kk[ch][:lo] * jnp.exp(r_i - bc[ch][:lo])).astype(BF16)
            row_att.append(lax.dot_general(qt, kt, NT_DIMS, preferred_element_type=F32).astype(BF16))
        att.append(row_att)

    st_t = st_ref[...]
    states = []
    for ch in range(n_ch):
        states.append(st_t.astype(BF16))
        st_t = st_t * jnp.exp(bl[ch]) + upd[ch]
    st_ref[...] = st_t

    o_inter = [lax.dot_general((qs[ch] * jnp.exp(bc[ch])).astype(BF16), states[ch], NT_DIMS,
                               preferred_element_type=F32) for ch in range(n_ch)]
    o_off = [[jnp.dot(att[ch][blk - 1], ivb[ch][:blk * sub], preferred_element_type=F32)
              for blk in range(1, n_sub)] for ch in range(n_ch)]

    for ch in range(n_ch):
        outs = []
        for blk in range(n_sub):
            lo = blk * sub
            o_i = o_inter[ch][lo:lo + sub]
            if blk > 0:
                o_i = o_i + o_off[ch][blk - 1]
            q_i, k_i, iv_i, fg_i = (x[ch][lo:lo + sub] for x in (qs, kk, iv, fg))
            u = jnp.zeros_like(q_i)
            for s in reversed(range(sub)):
                decayed = u * fg_i[s + 1:s + 2] if s + 1 < sub else u
                u = jnp.where(trow == s, q_i, decayed)
                a_col = jnp.sum(u * k_i[s:s + 1], axis=1, keepdims=True)
                o_i = o_i + a_col * iv_i[s:s + 1]
            outs.append(o_i)
        o = jnp.concatenate(outs, axis=0)
        o = o * lax.rsqrt(jnp.mean(o * o, axis=-1, keepdims=True) + RMS_EPS) * ng_ref[...]
        o_ref[rows[ch], :] = (o * _silu(g_ref[rows[ch], :])).astype(o_ref.dtype)


def _hgrn(proj, lb, norm_g, bsz, seq, tc=1024):
    t = bsz * seq
    tiles = seq // tc
    hd = HGRN_HEAD_DIM
    spec = lambda off: pl.BlockSpec((tc, hd), lambda b, h, j, off=off: (b * tiles + j, off + h))
    return pl.pallas_call(
        functools.partial(_hgrn_kernel, tc=tc),
        grid=(bsz, HGRN_HEADS, tiles),
        in_specs=[spec(0), spec(HGRN_HEADS), spec(2 * HGRN_HEADS), spec(3 * HGRN_HEADS),
                  pl.BlockSpec((1, hd), lambda b, h, j: (0, h)),
                  pl.BlockSpec((1, hd), lambda b, h, j: (0, 0))],
        out_specs=spec(0),
        out_shape=jax.ShapeDtypeStruct((t, D_MODEL), BF16),
        scratch_shapes=[pltpu.VMEM((hd, hd), F32)],
        compiler_params=_cparams("parallel", "parallel", "arbitrary"),
        name="hgrn2",
    )(proj, proj, proj, proj, lb.reshape(1, D_MODEL), norm_g.astype(F32).reshape(1, hd))


MOBA_SUPER = 2
MOBA_Q_TILES = 4
MOBA_VT_ROWS = HEAD_DIM + 16


def _moba_kernel(q_ref, k_ref, v_ref, o_ref, km_ref, kp_ref, vt_ref, qp_ref, acc_ref, *s_refs, n_blk, n_sel):
    blk = MOBA_BLOCK
    sbk = MOBA_SUPER * blk
    lane = lax.broadcasted_iota(jnp.int32, (1, LANES), 1)
    head_of_lane = lane // HEAD_DIM
    hot_lane0 = [(1 - hh) * HEAD_DIM for hh in range(HEADS_PER_TILE)]
    n_rows = HEADS_PER_TILE * n_blk

    ones_row = (lax.broadcasted_iota(jnp.int32, (MOBA_VT_ROWS - HEAD_DIM, 1), 0) == 0).astype(BF16)
    for hh in range(HEADS_PER_TILE):
        vt_ref[hh * MOBA_VT_ROWS + HEAD_DIM:(hh + 1) * MOBA_VT_ROWS, :] = jnp.broadcast_to(
            ones_row, (MOBA_VT_ROWS - HEAD_DIM, n_blk * blk))
    for n in range(n_blk):
        rows = slice(n * blk, (n + 1) * blk)
        kb = k_ref[rows, :]
        v_t = v_ref[rows, :].T
        for hh in range(HEADS_PER_TILE):
            vt_ref[hh * MOBA_VT_ROWS:hh * MOBA_VT_ROWS + HEAD_DIM, rows] = (
                v_t[hh * HEAD_DIM:(hh + 1) * HEAD_DIM].astype(BF16))
        km = jnp.sum(kb, axis=0, keepdims=True) * (1.0 / blk)
        for hh in range(HEADS_PER_TILE):
            mine = head_of_lane == hh
            km_ref[hh * n_blk + n:hh * n_blk + n + 1, :] = jnp.where(mine, km, 0.0)
            hot = (lane == hot_lane0[hh] + n).astype(F32)
            kp_ref[rows, hh * LANES:(hh + 1) * LANES] = jnp.where(mine, kb, hot).astype(BF16)

    prow = lax.broadcasted_iota(jnp.int32, (n_rows, HEADS_PER_TILE * LANES), 0)
    pcol = lax.broadcasted_iota(jnp.int32, (n_rows, HEADS_PER_TILE * LANES), 1)
    target = jnp.zeros_like(prow)
    for hh in range(HEADS_PER_TILE):
        target = jnp.where(prow // n_blk == hh, hh * LANES + hot_lane0[hh] + prow % n_blk, target)
    place = (pcol == target).astype(BF16)
    nid = lax.broadcasted_iota(jnp.int32, (n_blk, blk), 0)
    nid_f = nid.astype(F32)
    km_all = km_ref[...]

    q_rows = [slice(qi * blk, (qi + 1) * blk) for qi in range(n_blk)]
    gates = [lax.dot_general(km_all, q_ref[rw, :], NT_DIMS, preferred_element_type=F32,
                             precision=lax.Precision.HIGHEST) for rw in q_rows]
    biases = []
    for qi in range(n_blk):
        past = nid < qi
        bias_rows = []
        for hh in range(HEADS_PER_TILE):
            g = jnp.where(past, gates[qi][hh * n_blk:(hh + 1) * n_blk], NEG)
            keep = nid == qi
            for _ in range(n_sel):
                mx = jnp.max(g, axis=0, keepdims=True)
                idx = jnp.min(jnp.where(g == mx, nid_f, float(n_blk)), axis=0, keepdims=True)
                pick = nid_f == idx
                keep = jnp.logical_or(keep, jnp.logical_and(pick, past))
                g = jnp.where(pick, BELOW_NEG, g)
            bias_rows.append(jnp.where(keep, 0.0, NEG))
        biases.append(jnp.concatenate(bias_rows, axis=0).astype(BF16))
    bias_qs = [lax.dot_general(b, place, TN_DIMS, preferred_element_type=F32) for b in biases]
    for qi in range(n_blk):
        q = q_ref[q_rows[qi], :]
        for hh in range(HEADS_PER_TILE):
            qp_ref[q_rows[qi], hh * LANES:(hh + 1) * LANES] = jnp.where(
                head_of_lane == hh, q, bias_qs[qi][:, hh * LANES:(hh + 1) * LANES]).astype(BF16)

    krow = lax.broadcasted_iota(jnp.int32, (sbk, blk), 0)
    qcol = lax.broadcasted_iota(jnp.int32, (sbk, blk), 1)
    rel = qcol - krow

    n_qt = MOBA_Q_TILES
    assert n_qt == 2 * MOBA_SUPER
    n_chain = n_qt * HEADS_PER_TILE
    buf_a, buf_b = s_refs[:n_chain], s_refs[n_chain:]
    all_tiles = tuple(range(n_qt))

    def q_operands(g):
        q0 = pl.multiple_of(g * n_qt * blk, n_qt * blk)
        return [[qp_ref[pl.ds(q0 + t * blk, blk), hh * LANES:(hh + 1) * LANES] for hh in range(HEADS_PER_TILE)]
                for t in range(n_qt)]

    def stage(bufs, k0, qps, t, hh):
        kpn = kp_ref[pl.ds(k0, sbk), hh * LANES:(hh + 1) * LANES]
        bufs[t * HEADS_PER_TILE + hh][...] = lax.dot_general(kpn, qps[t][hh], NT_DIMS, preferred_element_type=F32)

    n_groups = n_blk // n_qt

    def q_group(g, carry):
        q0 = pl.multiple_of(g * n_qt * blk, n_qt * blk)
        qps = q_operands(g)
        acc_ref[...] = jnp.zeros_like(acc_ref)

        def step(stats, cur, k0, tiles, causal, nxt=None, nxt_tiles=(), nxt_k0=None, nxt_q=qps):
            chains = [(t, hh) for t in tiles for hh in range(HEADS_PER_TILE)]
            ahead = [(t, hh) for t in nxt_tiles for hh in range(HEADS_PER_TILE)]
            nxt_k0 = k0 + sbk if nxt_k0 is None else nxt_k0
            v_t = [vt_ref[hh * MOBA_VT_ROWS:(hh + 1) * MOBA_VT_ROWS, pl.ds(k0, sbk)] for hh in range(HEADS_PER_TILE)]
            stats = list(stats)
            for t, hh in ahead[:2]:
                stage(nxt, nxt_k0, nxt_q, t, hh)
            for i, (t, hh) in enumerate(chains):
                c = t * HEADS_PER_TILE + hh
                m_o = stats[c]
                s_t = cur[c][...]
                if t in causal:
                    s_t = jnp.where(rel >= causal[t], s_t, NEG)
                m_n = jnp.maximum(m_o, jnp.max(s_t, axis=0, keepdims=True))
                p = jnp.exp2(s_t - m_n).astype(BF16)
                acc_ref[c] = jnp.exp2(m_o - m_n) * acc_ref[c] + jnp.dot(v_t[hh], p, preferred_element_type=F32)
                stats[c] = m_n
                for t2, hh2 in ahead[i + 2:i + 3]:
                    stage(nxt, nxt_k0, nxt_q, t2, hh2)
            for t2, hh2 in ahead[len(chains) + 2:]:
                stage(nxt, nxt_k0, nxt_q, t2, hh2)
            return tuple(stats)

        def pair(i, stats):
            k0 = pl.multiple_of(i * 2 * sbk, 2 * sbk)
            stats = step(stats, buf_a, k0, all_tiles, {}, buf_b, all_tiles)
            return step(stats, buf_b, k0 + sbk, all_tiles, {}, buf_a, all_tiles)

        init = (jnp.full((1, blk), BELOW_NEG, F32),) * n_chain
        stats = lax.fori_loop(0, g, pair, init)
        late = all_tiles[MOBA_SUPER:]
        stats = step(stats, buf_a, q0, all_tiles, {t: -t * blk for t in all_tiles[:MOBA_SUPER]}, buf_b, late)
        step(stats, buf_b, q0 + sbk, late, {t: -(t - MOBA_SUPER) * blk for t in late},
             buf_a, all_tiles, 0, q_operands(jnp.minimum(g + 1, n_groups - 1)))
        for t in all_tiles:
            accs = [acc_ref[t * HEADS_PER_TILE + hh] for hh in range(HEADS_PER_TILE)]
            o_t = jnp.concatenate([a[:HEAD_DIM] / a[HEAD_DIM:HEAD_DIM + 1] for a in accs], axis=0)
            o_ref[pl.ds(q0 + t * blk, blk), :] = o_t.T.astype(o_ref.dtype)
        return carry

    first_q = q_operands(0)
    for t in all_tiles:
        for hh in range(HEADS_PER_TILE):
            stage(buf_a, 0, first_q, t, hh)
    lax.fori_loop(0, n_groups, q_group, 0)


def _moba(qkv, bsz, seq):
    blk = MOBA_BLOCK
    assert seq % (blk * MOBA_Q_TILES) == 0
    n_chain = MOBA_Q_TILES * HEADS_PER_TILE
    n_blk = seq // blk
    n_sel = min(MOBA_TOPK, n_blk - 1)
    n_tiles = D_MODEL // LANES
    qkv3 = qkv.reshape(bsz, seq, 3 * D_MODEL)
    spec = lambda off: pl.BlockSpec((None, seq, LANES), lambda b, hp, off=off: (b, 0, off + hp))
    out = pl.pallas_call(
        functools.partial(_moba_kernel, n_blk=n_blk, n_sel=n_sel),
        grid=(bsz, n_tiles),
        in_specs=[spec(0), spec(n_tiles), spec(2 * n_tiles)],
        out_specs=pl.BlockSpec((None, seq, LANES), lambda b, hp: (b, 0, hp)),
        out_shape=jax.ShapeDtypeStruct((bsz, seq, D_MODEL), BF16),
        scratch_shapes=[pltpu.VMEM((HEADS_PER_TILE * n_blk, LANES), F32),
                        pltpu.VMEM((seq, HEADS_PER_TILE * LANES), BF16),
                        pltpu.VMEM((HEADS_PER_TILE * MOBA_VT_ROWS, seq), BF16),
                        pltpu.VMEM((seq, HEADS_PER_TILE * LANES), BF16),
                        pltpu.VMEM((n_chain, MOBA_VT_ROWS, blk), F32)]
                       + [pltpu.VMEM((MOBA_SUPER * blk, blk), F32)] * (2 * n_chain),
        compiler_params=_cparams("parallel", "parallel"),
        name="moba_attn",
    )(qkv3, qkv3, qkv3)
    return out.reshape(bsz * seq, D_MODEL)


def _rope_tables(positions):
    half = ROPE_DIM // 2
    inv = ROPE_THETA ** (-jnp.arange(0, ROPE_DIM, 2, dtype=F32) / ROPE_DIM)
    ang = positions.astype(F32).reshape(-1)[:, None] * inv
    cos_f, sin_f = jnp.cos(ang), jnp.sin(ang)
    j = jnp.arange(LANES, dtype=jnp.int32) % HEAD_DIM
    pick = jnp.logical_and(j[None, :] % half == jnp.arange(half)[:, None], j[None, :] < ROPE_DIM).astype(F32)
    cos = sum(cos_f[:, f:f + 1] * pick[f][None, :] for f in range(half))
    sin = sum(sin_f[:, f:f + 1] * pick[f][None, :] for f in range(half))
    cos = jnp.where(j < ROPE_DIM, cos, 1.0)
    s1 = jnp.where(j < half, -sin, 0.0)
    s2 = jnp.where(jnp.logical_and(j >= half, j < ROPE_DIM), sin, 0.0)
    return cos, s1, s2


def _qkv_weight(w):
    q_scale = HEAD_DIM ** -0.5 * math.log2(math.e)
    scale = jnp.concatenate([jnp.full((D_MODEL,), q_scale, F32), jnp.ones((2 * D_MODEL,), F32)])
    return (w.astype(F32) * scale[None, :]).astype(BF16)


def kernel(x, positions, norm_mix, norm_ffn, norm_final, s5_a_re, s5_a_im, s5_log_dt, s5_b_re, s5_b_im, s5_c_re, s5_c_im, s5_d, s5_w_glu, s5_b_glu, dil_w_qkv, dil_w_o, hgrn_w_in, hgrn_lower_bound, hgrn_norm, hgrn_w_o, moba_w_qkv, moba_w_o, ffn_w_gate_up, ffn_w_down):
    bsz, seq, d = x.shape
    depth = norm_mix.shape[0]
    n_mixers = 4
    t = bsz * seq
    rope = _rope_tables(positions)
    per = DIL_TILE // DIL_CLASSES
    pos_cm = positions.reshape(bsz, seq // DIL_TILE, per, DIL_CLASSES).transpose(0, 1, 3, 2)
    rope_cm = _rope_tables(pos_cm)
    wgu = ffn_w_gate_up.astype(BF16)
    wdn = ffn_w_down.astype(BF16)
    lb_w = jax.nn.softmax(hgrn_lower_bound.astype(F32), axis=0)
    lower_bounds = jnp.cumsum(lb_w, axis=0) - lb_w[0]
    norm_mix = norm_mix.astype(F32)
    norm_ffn = norm_ffn.astype(F32)

    h = x.reshape(t, d).astype(F32)
    for layer in range(depth):
        mixer, j = layer % n_mixers, layer // n_mixers
        final_g = norm_final.astype(F32) if layer == depth - 1 else None
        mix = None
        if mixer == 0:
            h = _s5(h, norm_mix[layer], s5_a_re[j], s5_a_im[j], s5_log_dt[j], s5_b_re[j], s5_b_im[j],
                    s5_c_re[j], s5_c_im[j], s5_d[j], s5_w_glu[j], s5_b_glu[j], seq)
        elif mixer == 1:
            qkv = _proj(h, norm_mix[layer], _qkv_weight(dil_w_qkv[j]), rope_cm, 2 * D_MODEL,
                        class_major=(bsz, seq, DIL_CLASSES), tm=DIL_TILE)
            mix = (_dilated(qkv, bsz, seq), dil_w_o[j].astype(BF16))
        elif mixer == 2:
            proj = _proj(h, norm_mix[layer], hgrn_w_in[j].astype(BF16), tm=512)
            mix = (_hgrn(proj, lower_bounds[layer], hgrn_norm[j], bsz, seq), hgrn_w_o[j].astype(BF16))
        else:
            qkv = _proj(h, norm_mix[layer], _qkv_weight(moba_w_qkv[j]), rope, 2 * D_MODEL, tm=512)
            mix = (_moba(qkv, bsz, seq), moba_w_o[j].astype(BF16))
        h = _ffn(h, norm_ffn[layer], wgu, wdn, layer, mix=mix, final_g=final_g)
    return h.reshape(bsz, seq, d).astype(x.dtype)
```

```python
import functools
import math

import jax
import jax.numpy as jnp
from jax import lax
from jax.experimental import pallas as pl
from jax.experimental.pallas import tpu as pltpu

F32 = jnp.float32
BF16 = jnp.bfloat16

D_MODEL = 1024
D_FF = 2816
RMS_EPS = 1e-6
NEG = -1e30
BELOW_NEG = -3e38

HEAD_DIM = 64
ROPE_DIM = HEAD_DIM // 4
ROPE_THETA = 500000.0
LANES = 128
SUBLANES = 8
HEADS_PER_TILE = LANES // HEAD_DIM

S5_GROUP = 16
S5_GROUPS = D_MODEL // S5_GROUP
S5_STATE = 64
S5_CH = S5_GROUPS * S5_STATE
S5_SLABS = D_MODEL // LANES
S5_SLAB_CH = S5_CH // S5_SLABS
S5_SCAN_SHIFTS = (2, 4)

DIL_PATTERNS = ((128, 1), (512, 4), (2048, 16))
DIL_BLOCK = 128

HGRN_HEAD_DIM = 128
HGRN_HEADS = D_MODEL // HGRN_HEAD_DIM
HGRN_CHUNK = 64
HGRN_SUB = 8

MOBA_BLOCK = 256
MOBA_TOPK = 3

VMEM_LIMIT = 56 * 1024 * 1024

NT_DIMS = (((1,), (1,)), ((), ()))
TN_DIMS = (((0,), (0,)), ((), ()))


def _cparams(*sem):
    return pltpu.CompilerParams(dimension_semantics=sem, vmem_limit_bytes=VMEM_LIMIT)


def _rms(x, g):
    return x * lax.rsqrt(jnp.mean(x * x, axis=-1, keepdims=True) + RMS_EPS) * g


def _silu(x):
    return x * jax.nn.sigmoid(x)


def _proj_kernel(*refs, n_out, tn, rope_cols, n_cls):
    if rope_cols:
        x_ref, g_ref, w_ref, c_ref, s1_ref, s2_ref, o_ref = refs
    else:
        x_ref, g_ref, w_ref, o_ref = refs
    tm = x_ref.shape[0]
    per = tm // n_cls
    xn = _rms(x_ref[...], g_ref[...]).astype(BF16)
    if n_cls > 1:
        dst = lax.broadcasted_iota(jnp.int32, (tm, tm), 0)
        src = lax.broadcasted_iota(jnp.int32, (tm, tm), 1)
        perm = (src == (dst % per) * n_cls + dst // per).astype(BF16)
        xn = jnp.dot(perm, xn, preferred_element_type=F32).astype(BF16)
    if rope_cols:
        cos, s1, s2 = c_ref[...], s1_ref[...], s2_ref[...]
    for c in range(n_out // tn):
        y = jnp.dot(xn, w_ref[:, c * tn:(c + 1) * tn], preferred_element_type=F32)
        if c * tn < rope_cols:
            parts = []
            for j in range(tn // LANES):
                yj = y[:, j * LANES:(j + 1) * LANES]
                half = ROPE_DIM // 2
                parts.append(yj * cos + pltpu.roll(yj, LANES - half, 1) * s1 + pltpu.roll(yj, half, 1) * s2)
            y = jnp.concatenate(parts, axis=1) if len(parts) > 1 else parts[0]
        if n_cls == 1:
            o_ref[:, c * tn:(c + 1) * tn] = y
        else:
            for r in range(n_cls):
                o_ref[r, :, c * tn:(c + 1) * tn] = y[r * per:(r + 1) * per]


def _proj(h, g, w, rope=None, rope_cols=0, class_major=None, tm=256, tn=256):
    t, d = h.shape
    n_out = w.shape[1]
    in_specs = [pl.BlockSpec((tm, d), lambda i: (i, 0)),
                pl.BlockSpec((1, d), lambda i: (0, 0)),
                pl.BlockSpec((d, n_out), lambda i: (0, 0))]
    args = [h, g.reshape(1, d), w]
    if rope_cols:
        in_specs += [pl.BlockSpec((tm, LANES), lambda i: (i, 0))] * 3
        args += list(rope)
    if class_major is None:
        n_cls = 1
        out_spec = pl.BlockSpec((tm, n_out), lambda i: (i, 0))
        out_shape = jax.ShapeDtypeStruct((t, n_out), F32)
    else:
        bsz, seq, n_cls = class_major
        tiles = seq // tm
        per = tm // n_cls
        assert seq % tm == 0 and per % SUBLANES == 0
        out_spec = pl.BlockSpec((None, n_cls, per, n_out), lambda i: (i // tiles, 0, i % tiles, 0))
        out_shape = jax.ShapeDtypeStruct((bsz, n_cls, seq // n_cls, n_out), F32)
    return pl.pallas_call(
        functools.partial(_proj_kernel, n_out=n_out, tn=tn, rope_cols=rope_cols, n_cls=n_cls),
        grid=(t // tm,),
        in_specs=in_specs,
        out_specs=out_spec,
        out_shape=out_shape,
        compiler_params=_cparams("parallel"),
        name="proj_rope" if rope_cols else "proj",
    )(*args)


def _ffn_kernel(*refs, has_mix, has_final, fc):
    refs = list(refs)
    h_ref, g_ref, wgu_ref, wd_ref = refs[:4]
    pos = 4
    if has_mix:
        a_ref, wo_ref = refs[pos:pos + 2]
        pos += 2
    if has_final:
        fg_ref = refs[pos]
        pos += 1
    o_ref, acc_ref = refs[pos], refs[pos + 1]

    h = h_ref[...]
    if has_mix:
        h = h + jnp.dot(a_ref[...], wo_ref[...], preferred_element_type=F32)
    xn = _rms(h, g_ref[...]).astype(BF16)

    def gate_up(c):
        gate = jnp.dot(xn, wgu_ref[:, c * fc:(c + 1) * fc], preferred_element_type=F32)
        up = jnp.dot(xn, wgu_ref[:, D_FF + c * fc:D_FF + (c + 1) * fc], preferred_element_type=F32)
        return gate, up

    n_chunks = D_FF // fc
    nxt = gate_up(0)
    for c in range(n_chunks):
        gate, up = nxt
        if c + 1 < n_chunks:
            nxt = gate_up(c + 1)
        act = (_silu(gate) * up).astype(BF16)
        contrib = jnp.dot(act, wd_ref[c * fc:(c + 1) * fc, :], preferred_element_type=F32)
        if c == 0:
            acc_ref[...] = contrib
        else:
            acc_ref[...] += contrib
    out = h + acc_ref[...]
    if has_final:
        out = _rms(out, fg_ref[...])
    o_ref[...] = out


def _ffn(h, g, wgu, wd, layer, mix=None, final_g=None, tm=512, fc=256):
    t, d = h.shape
    const = lambda i: (0, 0)
    in_specs = [pl.BlockSpec((tm, d), lambda i: (i, 0)),
                pl.BlockSpec((1, d), const),
                pl.BlockSpec((None, d, 2 * D_FF), lambda i: (layer, 0, 0)),
                pl.BlockSpec((None, D_FF, d), lambda i: (layer, 0, 0))]
    args = [h, g.reshape(1, d), wgu, wd]
    if mix is not None:
        a, wo = mix
        in_specs += [pl.BlockSpec((tm, d), lambda i: (i, 0)), pl.BlockSpec((d, d), const)]
        args += [a, wo]
    if final_g is not None:
        in_specs += [pl.BlockSpec((1, d), const)]
        args += [final_g.reshape(1, d)]
    return pl.pallas_call(
        functools.partial(_ffn_kernel, has_mix=mix is not None, has_final=final_g is not None, fc=fc),
        grid=(t // tm,),
        in_specs=in_specs,
        out_specs=pl.BlockSpec((tm, d), lambda i: (i, 0)),
        out_shape=jax.ShapeDtypeStruct((t, d), F32),
        scratch_shapes=[pltpu.VMEM((tm, d), F32)],
        compiler_params=_cparams("parallel"),
        name="ffn",
    )(*args)


def _s5_kernel(x_ref, g_ref, wbr_ref, wbi_ref, akr_ref, aki_ref, pr_ref, pi_ref, wcr_ref, wci_ref,
               d_ref, wglu_ref, bglu_ref, o_ref, er_ref, ei_ref, cr_ref, ci_ref, *, tm, tiles_per_seq):
    i = pl.program_id(0)

    @pl.when(i % tiles_per_seq == 0)
    def _():
        cr_ref[...] = jnp.zeros_like(cr_ref)
        ci_ref[...] = jnp.zeros_like(ci_ref)

    x = x_ref[...]
    u = _rms(x, g_ref[...])
    ub = u.astype(BF16)
    first = lax.broadcasted_iota(jnp.int32, (tm, 1), 0) % SUBLANES == 0
    ub_prev = jnp.where(first, 0.0, pltpu.roll(u, 1, 0)).astype(BF16)
    for s in range(S5_SLABS):
        us = jnp.concatenate([ub[:, s * LANES:(s + 1) * LANES], ub_prev[:, s * LANES:(s + 1) * LANES]], axis=1)
        er_ref[:, s * S5_SLAB_CH:(s + 1) * S5_SLAB_CH] = jnp.dot(us, wbr_ref[s], preferred_element_type=F32)
        ei_ref[:, s * S5_SLAB_CH:(s + 1) * S5_SLAB_CH] = jnp.dot(us, wbi_ref[s], preferred_element_type=F32)

    def group(j, carry):
        r0 = pl.multiple_of(j * SUBLANES, SUBLANES)
        xr = er_ref[pl.ds(r0, SUBLANES), :]
        xi = ei_ref[pl.ds(r0, SUBLANES), :]
        for lvl, k in enumerate(S5_SCAN_SHIFTS):
            sr = pltpu.roll(xr, k, 0)
            si = pltpu.roll(xi, k, 0)
            ar, ai = akr_ref[lvl], aki_ref[lvl]
            xr, xi = xr + ar * sr - ai * si, xi + ar * si + ai * sr
        cr, ci = cr_ref[...], ci_ref[...]
        pr, pi_ = pr_ref[...], pi_ref[...]
        hr = xr + pr * cr - pi_ * ci
        hi = xi + pr * ci + pi_ * cr
        er_ref[pl.ds(r0, SUBLANES), :] = hr
        ei_ref[pl.ds(r0, SUBLANES), :] = hi
        cr_ref[...] = jnp.broadcast_to(hr[SUBLANES - 1:SUBLANES, :], (SUBLANES, S5_CH))
        ci_ref[...] = jnp.broadcast_to(hi[SUBLANES - 1:SUBLANES, :], (SUBLANES, S5_CH))
        return carry

    lax.fori_loop(0, tm // SUBLANES, group, 0)

    ys = []
    for s in range(S5_SLABS):
        hr = er_ref[:, s * S5_SLAB_CH:(s + 1) * S5_SLAB_CH].astype(BF16)
        hi = ei_ref[:, s * S5_SLAB_CH:(s + 1) * S5_SLAB_CH].astype(BF16)
        ys.append(jnp.dot(hr, wcr_ref[s], preferred_element_type=F32)
                  - jnp.dot(hi, wci_ref[s], preferred_element_type=F32))
    y = jnp.concatenate(ys, axis=1) + d_ref[...] * u
    z = jax.nn.gelu(y).astype(BF16)
    zz = jnp.dot(z, wglu_ref[...], preferred_element_type=F32) + bglu_ref[...]
    o_ref[...] = x + zz[:, :D_MODEL] * jax.nn.sigmoid(zz[:, D_MODEL:])


def _block_diag_slabs(w):
    g, r, c = w.shape
    per = g // S5_SLABS
    w = w.reshape(S5_SLABS, per, r, c)
    eye = jnp.eye(per, dtype=w.dtype)
    return jnp.einsum('sgrc,gh->sgrhc', w, eye).reshape(S5_SLABS, per * r, per * c)


def _s5_tables(a_re, a_im, log_dt):
    lr, li = a_re.astype(F32), a_im.astype(F32)
    dt = jnp.exp(log_dt.astype(F32))[:, None]
    mag = jnp.exp(lr * dt)
    ab_re, ab_im = mag * jnp.cos(li * dt), mag * jnp.sin(li * dt)
    den = lr * lr + li * li
    m_re = ab_re - 1.0
    f_re = (m_re * lr + ab_im * li) / den
    f_im = (ab_im * lr - m_re * li) / den

    def power(k):
        return ((jnp.exp(lr * dt * k) * jnp.cos(li * dt * k)).reshape(-1),
                (jnp.exp(lr * dt * k) * jnp.sin(li * dt * k)).reshape(-1))

    rows = jnp.arange(SUBLANES)[:, None]
    akr, aki = [], []
    for k in S5_SCAN_SHIFTS:
        pr, pi_ = power(float(k))
        akr.append(jnp.where(rows >= k, pr[None, :], 0.0))
        aki.append(jnp.where(rows >= k, pi_[None, :], 0.0))
    pw = [power(float(k + 1)) for k in range(SUBLANES)]
    p_re = jnp.stack([p[0] for p in pw])
    p_im = jnp.stack([p[1] for p in pw])
    return f_re, f_im, ab_re, ab_im, jnp.stack(akr), jnp.stack(aki), p_re, p_im


def _s5(h, g, a_re, a_im, log_dt, b_re, b_im, c_re, c_im, d_skip, w_glu, b_glu, seq, tm=512):
    t, d = h.shape
    f_re, f_im, ab_re, ab_im, akr, aki, p_re, p_im = _s5_tables(a_re, a_im, log_dt)
    bw_re = f_re[:, :, None] * b_re - f_im[:, :, None] * b_im
    bw_im = f_re[:, :, None] * b_im + f_im[:, :, None] * b_re
    bw1_re = ab_re[:, :, None] * bw_re - ab_im[:, :, None] * bw_im
    bw1_im = ab_re[:, :, None] * bw_im + ab_im[:, :, None] * bw_re
    slabs = lambda w: _block_diag_slabs(jnp.swapaxes(w, 1, 2))
    wbr = jnp.concatenate([slabs(bw_re), slabs(bw1_re)], axis=1).astype(BF16)
    wbi = jnp.concatenate([slabs(bw_im), slabs(bw1_im)], axis=1).astype(BF16)
    wcr = _block_diag_slabs(jnp.swapaxes(c_re.astype(F32), 1, 2)).astype(BF16)
    wci = _block_diag_slabs(jnp.swapaxes(c_im.astype(F32), 1, 2)).astype(BF16)
    c2 = lambda i: (0, 0)
    c3 = lambda i: (0, 0, 0)
    return pl.pallas_call(
        functools.partial(_s5_kernel, tm=tm, tiles_per_seq=seq // tm),
        grid=(t // tm,),
        in_specs=[pl.BlockSpec((tm, d), lambda i: (i, 0)),
                  pl.BlockSpec((1, d), c2),
                  pl.BlockSpec(wbr.shape, c3), pl.BlockSpec(wbi.shape, c3),
                  pl.BlockSpec(akr.shape, c3), pl.BlockSpec(aki.shape, c3),
                  pl.BlockSpec(p_re.shape, c2), pl.BlockSpec(p_im.shape, c2),
                  pl.BlockSpec(wcr.shape, c3), pl.BlockSpec(wci.shape, c3),
                  pl.BlockSpec((1, d), c2),
                  pl.BlockSpec((d, 2 * d), c2),
                  pl.BlockSpec((1, 2 * d), c2)],
        out_specs=pl.BlockSpec((tm, d), lambda i: (i, 0)),
        out_shape=jax.ShapeDtypeStruct((t, d), F32),
        scratch_shapes=[pltpu.VMEM((tm, S5_CH), F32), pltpu.VMEM((tm, S5_CH), F32),
                        pltpu.VMEM((SUBLANES, S5_CH), F32), pltpu.VMEM((SUBLANES, S5_CH), F32)],
        compiler_params=_cparams("arbitrary"),
        name="s5",
    )(h, g.reshape(1, d), wbr, wbi, akr, aki, p_re, p_im, wcr, wci,
      d_skip.astype(F32).reshape(1, d), w_glu.astype(BF16), b_glu.astype(F32).reshape(1, 2 * d))


DIL_CLASSES = 16
DIL_STEPS = 2
DIL_TILE = 256


def _dil_kernel(*refs, seq):
    ncls = DIL_CLASSES
    u_len = seq // ncls
    blk = DIL_BLOCK
    qp_ref, kp_ref, vp_ref, o_ref = refs[:4]
    acc_ref, lsw_ref, m_ref, msw_ref = refs[4:8]
    n_chain = DIL_STEPS * HEADS_PER_TILE
    bufs = refs[8:]
    buf_a, buf_b = bufs[:n_chain], bufs[n_chain:]

    lane = lax.broadcasted_iota(jnp.int32, (1, LANES), 1)
    low = lane < HEAD_DIM
    n_iter = seq // blk // DIL_STEPS

    for pat, (window, dil) in enumerate(DIL_PATTERNS):
        assert window // dil == blk and ncls % dil == 0 and seq % (dil * blk) == 0
        n_run = ncls // dil
        run = SUBLANES * dil
        n_blk = u_len // run
        jq = lax.broadcasted_iota(jnp.int32, (blk, blk), 0)
        jk = lax.broadcasted_iota(jnp.int32, (blk, blk), 1)
        wq = (jq % run) * n_run + jq // run
        wk = (jk % run) * n_run + jk // run
        mask_cur = wk <= wq
        mask_prev = wk >= wq

        def run_rows(step, dil=dil, n_run=n_run, run=run, n_blk=n_blk):
            res = step // n_blk
            bi = step % n_blk
            base = [(res + dil * c) * u_len for c in range(n_run)]
            cur = [pl.multiple_of(b + run * bi, SUBLANES) for b in base]
            prev = [pl.multiple_of(b + run * jnp.maximum(bi - 1, 0), SUBLANES) for b in base]
            return cur, prev, bi > 0

        def gather(ref, offs, run=run):
            parts = [ref[pl.ds(o, run), :] for o in offs]
            return jnp.concatenate(parts, axis=0) if len(parts) > 1 else parts[0]

        def stage_block(bufs, step, t):
            cur, prev, _ = run_rows(step)
            q = gather(qp_ref, cur)
            kcat = jnp.concatenate([gather(kp_ref, prev), gather(kp_ref, cur)], axis=0).astype(BF16)
            for hh in range(HEADS_PER_TILE):
                qh = jnp.where(low if hh == 0 else jnp.logical_not(low), q, 0.0).astype(BF16)
                bufs[t * HEADS_PER_TILE + hh][...] = lax.dot_general(qh, kcat, NT_DIMS,
                                                                     preferred_element_type=F32)

        def phase(it, bufs, nxt_bufs, pat=pat):
            nxt = jnp.minimum(it + 1, n_iter - 1)
            stage_block(nxt_bufs, nxt * DIL_STEPS, 0)
            for t in range(DIL_STEPS):
                step = it * DIL_STEPS + t
                cur, prev, has_prev = run_rows(step)
                vcat = jnp.concatenate([gather(vp_ref, prev), gather(vp_ref, cur)], axis=0)
                mask = jnp.concatenate([jnp.logical_and(mask_prev, has_prev), mask_cur], axis=1)
                ms, os_ = [], []
                for hh in range(HEADS_PER_TILE):
                    c = t * HEADS_PER_TILE + hh
                    s = jnp.where(mask, bufs[c][...], NEG)
                    mb = jnp.max(s, axis=1, keepdims=True)
                    p = jnp.exp2(s - mb).astype(BF16)
                    vh = jnp.where(low if hh == 0 else jnp.logical_not(low), vcat, 1.0).astype(BF16)
                    os_.append(jnp.dot(p, vh, preferred_element_type=F32))
                    ms.append(jnp.broadcast_to(mb, (blk, LANES)))
                if t + 1 < DIL_STEPS:
                    stage_block(nxt_bufs, nxt * DIL_STEPS + t + 1, t + 1)
                m_b = jnp.where(low, ms[0], ms[1])
                msw_b = jnp.where(low, ms[1], ms[0])
                o_b = jnp.where(low, os_[0], os_[1])
                lsw_b = jnp.where(low, os_[1], os_[0])
                for ci, off in enumerate(cur):
                    rows = pl.ds(off, run)
                    piece = slice(ci * run, (ci + 1) * run)
                    if pat == 0:
                        m_ref[rows, :] = m_b[piece]
                        msw_ref[rows, :] = msw_b[piece]
                        acc_ref[rows, :] = o_b[piece]
                        lsw_ref[rows, :] = lsw_b[piece]
                    else:
                        m_o, msw_o = m_ref[rows, :], msw_ref[rows, :]
                        m_n = jnp.maximum(m_o, m_b[piece])
                        msw_n = jnp.maximum(msw_o, msw_b[piece])
                        acc_n = acc_ref[rows, :] * jnp.exp2(m_o - m_n) + o_b[piece] * jnp.exp2(m_b[piece] - m_n)
                        lsw_n = (lsw_ref[rows, :] * jnp.exp2(msw_o - msw_n)
                                 + lsw_b[piece] * jnp.exp2(msw_b[piece] - msw_n))
                        if pat + 1 < len(DIL_PATTERNS):
                            m_ref[rows, :] = m_n
                            msw_ref[rows, :] = msw_n
                            acc_ref[rows, :] = acc_n
                            lsw_ref[rows, :] = lsw_n
                        else:
                            den = pltpu.roll(lsw_n, HEAD_DIM, 1)
                            o_ref[rows, :] = (acc_n / den).astype(o_ref.dtype)

        def two_phases(i, carry, phase=phase):
            phase(2 * i, buf_a, buf_b)
            phase(2 * i + 1, buf_b, buf_a)
            return carry

        for t in range(DIL_STEPS):
            stage_block(buf_a, t, t)
        lax.fori_loop(0, n_iter // 2, two_phases, 0)


def _dilated(qkv, bsz, seq):
    ncls = DIL_CLASSES
    assert seq % (ncls * DIL_BLOCK) == 0 and (seq // DIL_BLOCK) % (2 * DIL_STEPS) == 0 and seq % 256 == 0
    u_len = seq // ncls
    n_tiles = D_MODEL // LANES
    qkv3 = qkv.reshape(bsz, seq, 3 * D_MODEL)
    spec = lambda off: pl.BlockSpec((None, seq, LANES), lambda b, hp, off=off: (b, 0, off + hp))
    in_specs = [spec(off) for off in (0, n_tiles, 2 * n_tiles)]
    out = pl.pallas_call(
        functools.partial(_dil_kernel, seq=seq),
        grid=(bsz, n_tiles),
        in_specs=in_specs,
        out_specs=pl.BlockSpec((None, seq, LANES), lambda b, hp: (b, 0, hp)),
        out_shape=jax.ShapeDtypeStruct((bsz, seq, D_MODEL), BF16),
        scratch_shapes=[pltpu.VMEM((seq, LANES), F32)] * 4
                       + [pltpu.VMEM((DIL_BLOCK, 2 * DIL_BLOCK), F32)] * (2 * DIL_STEPS * HEADS_PER_TILE),
        compiler_params=_cparams("parallel", "parallel"),
        name="dilated_attn",
    )(*([qkv3] * len(in_specs)))
    out = out.reshape(bsz, ncls, u_len, D_MODEL).transpose(0, 2, 1, 3)
    return out.reshape(bsz * seq, D_MODEL)


def _hgrn_kernel(q_ref, f_ref, i_ref, g_ref, lb_ref, ng_ref, o_ref, st_ref, *, tc):
    @pl.when(pl.program_id(2) == 0)
    def _():
        st_ref[...] = jnp.zeros_like(st_ref)

    chunk, sub = HGRN_CHUNK, HGRN_SUB
    n_ch = tc // chunk
    n_sub = chunk // sub
    r = lax.broadcasted_iota(jnp.int32, (chunk, chunk), 0)
    c = lax.broadcasted_iota(jnp.int32, (chunk, chunk), 1)
    tril = (c <= r).astype(F32)
    trow = lax.broadcasted_iota(jnp.int32, (sub, 1), 0)
    lb = lb_ref[...]
    rows = [slice(ch * chunk, (ch + 1) * chunk) for ch in range(n_ch)]

    qs = [_silu(q_ref[rw, :]) for rw in rows]
    fg = [lb + (1.0 - lb) * jax.nn.sigmoid(f_ref[rw, :]) for rw in rows]
    kk = [1.0 - x for x in fg]
    iv = [i_ref[rw, :] for rw in rows]
    ivb = [x.astype(BF16) for x in iv]
    bc = [jnp.dot(tril, jnp.log(x), preferred_element_type=F32, precision=lax.Precision.HIGHEST) for x in fg]
    bl = [x[chunk - 1:chunk] for x in bc]

    upd = [lax.dot_general(ivb[ch], (kk[ch] * jnp.exp(bl[ch] - bc[ch])).astype(BF16), TN_DIMS,
                           preferred_element_type=F32) for ch in range(n_ch)]
    att = []
    for ch in range(n_ch):
        row_att = []
        for blk in range(1, n_sub):
            lo = blk * sub
            r_i = bc[ch][lo - 1:lo]
            qt = (qs[ch][lo:lo + sub] * jnp.exp(bc[ch][lo:lo + sub] - r_i)).astype(BF16)
            kt = (kk[ch][:lo] * jnp.exp(r_i - bc[ch][:lo])).astype(BF16)
            row_att.append(lax.dot_general(qt, kt, NT_DIMS, preferred_element_type=F32).astype(BF16))
        att.append(row_att)

    st_t = st_ref[...]
    states = []
    for ch in range(n_ch):
        states.append(st_t.astype(BF16))
        st_t = st_t * jnp.exp(bl[ch]) + upd[ch]
    st_ref[...] = st_t

    o_inter = [lax.dot_general((qs[ch] * jnp.exp(bc[ch])).astype(BF16), states[ch], NT_DIMS,
                               preferred_element_type=F32) for ch in range(n_ch)]
    o_off = [[jnp.dot(att[ch][blk - 1], ivb[ch][:blk * sub], preferred_element_type=F32)
              for blk in range(1, n_sub)] for ch in range(n_ch)]

    for ch in range(n_ch):
        outs = []
        for blk in range(n_sub):
            lo = blk * sub
            o_i = o_inter[ch][lo:lo + sub]
            if blk > 0:
                o_i = o_i + o_off[ch][blk - 1]
            q_i, k_i, iv_i, fg_i = (x[ch][lo:lo + sub] for x in (qs, kk, iv, fg))
            u = jnp.zeros_like(q_i)
            for s in reversed(range(sub)):
                decayed = u * fg_i[s + 1:s + 2] if s + 1 < sub else u
                u = jnp.where(trow == s, q_i, decayed)
                a_col = jnp.sum(u * k_i[s:s + 1], axis=1, keepdims=True)
                o_i = o_i + a_col * iv_i[s:s + 1]
            outs.append(o_i)
        o = jnp.concatenate(outs, axis=0)
        o = o * lax.rsqrt(jnp.mean(o * o, axis=-1, keepdims=True) + RMS_EPS) * ng_ref[...]
        o_ref[rows[ch], :] = (o * _silu(g_ref[rows[ch], :])).astype(o_ref.dtype)


def _hgrn(proj, lb, norm_g, bsz, seq, tc=2048):
    t = bsz * seq
    tiles = seq // tc
    hd = HGRN_HEAD_DIM
    spec = lambda off: pl.BlockSpec((tc, hd), lambda b, h, j, off=off: (b * tiles + j, off + h))
    return pl.pallas_call(
        functools.partial(_hgrn_kernel, tc=tc),
        grid=(bsz, HGRN_HEADS, tiles),
        in_specs=[spec(0), spec(HGRN_HEADS), spec(2 * HGRN_HEADS), spec(3 * HGRN_HEADS),
                  pl.BlockSpec((1, hd), lambda b, h, j: (0, h)),
                  pl.BlockSpec((1, hd), lambda b, h, j: (0, 0))],
        out_specs=spec(0),
        out_shape=jax.ShapeDtypeStruct((t, D_MODEL), BF16),
        scratch_shapes=[pltpu.VMEM((hd, hd), F32)],
        compiler_params=_cparams("parallel", "parallel", "arbitrary"),
        name="hgrn2",
    )(proj, proj, proj, proj, lb.reshape(1, D_MODEL), norm_g.astype(F32).reshape(1, hd))


MOBA_SUPER = 2
MOBA_Q_TILES = 4
MOBA_VT_ROWS = HEAD_DIM + 16


def _moba_kernel(q_ref, k_ref, v_ref, o_ref, km_ref, kp_ref, vt_ref, qp_ref, acc_ref, *s_refs, n_blk, n_sel):
    blk = MOBA_BLOCK
    sbk = MOBA_SUPER * blk
    lane = lax.broadcasted_iota(jnp.int32, (1, LANES), 1)
    head_of_lane = lane // HEAD_DIM
    hot_lane0 = [(1 - hh) * HEAD_DIM for hh in range(HEADS_PER_TILE)]
    n_rows = HEADS_PER_TILE * n_blk

    ones_row = (lax.broadcasted_iota(jnp.int32, (MOBA_VT_ROWS - HEAD_DIM, 1), 0) == 0).astype(BF16)
    for hh in range(HEADS_PER_TILE):
        vt_ref[hh * MOBA_VT_ROWS + HEAD_DIM:(hh + 1) * MOBA_VT_ROWS, :] = jnp.broadcast_to(
            ones_row, (MOBA_VT_ROWS - HEAD_DIM, n_blk * blk))
    for n in range(n_blk):
        rows = slice(n * blk, (n + 1) * blk)
        kb = k_ref[rows, :]
        v_t = v_ref[rows, :].T
        for hh in range(HEADS_PER_TILE):
            vt_ref[hh * MOBA_VT_ROWS:hh * MOBA_VT_ROWS + HEAD_DIM, rows] = (
                v_t[hh * HEAD_DIM:(hh + 1) * HEAD_DIM].astype(BF16))
        km = jnp.sum(kb, axis=0, keepdims=True) * (1.0 / blk)
        for hh in range(HEADS_PER_TILE):
            mine = head_of_lane == hh
            km_ref[hh * n_blk + n:hh * n_blk + n + 1, :] = jnp.where(mine, km, 0.0)
            hot = (lane == hot_lane0[hh] + n).astype(F32)
            kp_ref[rows, hh * LANES:(hh + 1) * LANES] = jnp.where(mine, kb, hot).astype(BF16)

    prow = lax.broadcasted_iota(jnp.int32, (n_rows, HEADS_PER_TILE * LANES), 0)
    pcol = lax.broadcasted_iota(jnp.int32, (n_rows, HEADS_PER_TILE * LANES), 1)
    target = jnp.zeros_like(prow)
    for hh in range(HEADS_PER_TILE):
        target = jnp.where(prow // n_blk == hh, hh * LANES + hot_lane0[hh] + prow % n_blk, target)
    place = (pcol == target).astype(BF16)
    nid = lax.broadcasted_iota(jnp.int32, (n_blk, blk), 0)
    nid_f = nid.astype(F32)
    km_all = km_ref[...]

    q_rows = [slice(qi * blk, (qi + 1) * blk) for qi in range(n_blk)]
    gates = [lax.dot_general(km_all, q_ref[rw, :], NT_DIMS, preferred_element_type=F32,
                             precision=lax.Precision.HIGHEST) for rw in q_rows]
    biases = []
    for qi in range(n_blk):
        past = nid < qi
        bias_rows = []
        for hh in range(HEADS_PER_TILE):
            g = jnp.where(past, gates[qi][hh * n_blk:(hh + 1) * n_blk], NEG)
            keep = nid == qi
            for _ in range(n_sel):
                mx = jnp.max(g, axis=0, keepdims=True)
                idx = jnp.min(jnp.where(g == mx, nid_f, float(n_blk)), axis=0, keepdims=True)
                pick = nid_f == idx
                keep = jnp.logical_or(keep, jnp.logical_and(pick, past))
                g = jnp.where(pick, BELOW_NEG, g)
            bias_rows.append(jnp.where(keep, 0.0, NEG))
        biases.append(jnp.concatenate(bias_rows, axis=0).astype(BF16))
    bias_qs = [lax.dot_general(b, place, TN_DIMS, preferred_element_type=F32) for b in biases]
    for qi in range(n_blk):
        q = q_ref[q_rows[qi], :]
        for hh in range(HEADS_PER_TILE):
            qp_ref[q_rows[qi], hh * LANES:(hh + 1) * LANES] = jnp.where(
                head_of_lane == hh, q, bias_qs[qi][:, hh * LANES:(hh + 1) * LANES]).astype(BF16)

    krow = lax.broadcasted_iota(jnp.int32, (sbk, blk), 0)
    qcol = lax.broadcasted_iota(jnp.int32, (sbk, blk), 1)
    rel = qcol - krow

    n_qt = MOBA_Q_TILES
    assert n_qt == 2 * MOBA_SUPER
    n_chain = n_qt * HEADS_PER_TILE
    buf_a, buf_b = s_refs[:n_chain], s_refs[n_chain:]
    all_tiles = tuple(range(n_qt))

    def q_operands(g):
        q0 = pl.multiple_of(g * n_qt * blk, n_qt * blk)
        return [[qp_ref[pl.ds(q0 + t * blk, blk), hh * LANES:(hh + 1) * LANES] for hh in range(HEADS_PER_TILE)]
                for t in range(n_qt)]

    def stage(bufs, k0, qps, t, hh):
        kpn = kp_ref[pl.ds(k0, sbk), hh * LANES:(hh + 1) * LANES]
        bufs[t * HEADS_PER_TILE + hh][...] = lax.dot_general(kpn, qps[t][hh], NT_DIMS, preferred_element_type=F32)

    n_groups = n_blk // n_qt

    def q_group(g, carry):
        q0 = pl.multiple_of(g * n_qt * blk, n_qt * blk)
        qps = q_operands(g)
        acc_ref[...] = jnp.zeros_like(acc_ref)

        def step(stats, cur, k0, tiles, causal, nxt=None, nxt_tiles=(), nxt_k0=None, nxt_q=qps):
            chains = [(t, hh) for t in tiles for hh in range(HEADS_PER_TILE)]
            ahead = [(t, hh) for t in nxt_tiles for hh in range(HEADS_PER_TILE)]
            nxt_k0 = k0 + sbk if nxt_k0 is None else nxt_k0
            v_t = [vt_ref[hh * MOBA_VT_ROWS:(hh + 1) * MOBA_VT_ROWS, pl.ds(k0, sbk)] for hh in range(HEADS_PER_TILE)]
            stats = list(stats)
            for t, hh in ahead[:2]:
                stage(nxt, nxt_k0, nxt_q, t, hh)
            for i, (t, hh) in enumerate(chains):
                c = t * HEADS_PER_TILE + hh
                m_o = stats[c]
                s_t = cur[c][...]
                if t in causal:
                    s_t = jnp.where(rel >= causal[t], s_t, NEG)
                m_n = jnp.maximum(m_o, jnp.max(s_t, axis=0, keepdims=True))
                p = jnp.exp2(s_t - m_n).astype(BF16)
                acc_ref[c] = jnp.exp2(m_o - m_n) * acc_ref[c] + jnp.dot(v_t[hh], p, preferred_element_type=F32)
                stats[c] = m_n
                for t2, hh2 in ahead[i + 2:i + 3]:
                    stage(nxt, nxt_k0, nxt_q, t2, hh2)
            for t2, hh2 in ahead[len(chains) + 2:]:
                stage(nxt, nxt_k0, nxt_q, t2, hh2)
            return tuple(stats)

        def pair(i, stats):
            k0 = pl.multiple_of(i * 2 * sbk, 2 * sbk)
            stats = step(stats, buf_a, k0, all_tiles, {}, buf_b, all_tiles)
            return step(stats, buf_b, k0 + sbk, all_tiles, {}, buf_a, all_tiles)

        init = (jnp.full((1, blk), BELOW_NEG, F32),) * n_chain
        stats = lax.fori_loop(0, g, pair, init)
        late = all_tiles[MOBA_SUPER:]
        stats = step(stats, buf_a, q0, all_tiles, {t: -t * blk for t in all_tiles[:MOBA_SUPER]}, buf_b, late)
        step(stats, buf_b, q0 + sbk, late, {t: -(t - MOBA_SUPER) * blk for t in late},
             buf_a, all_tiles, 0, q_operands(jnp.minimum(g + 1, n_groups - 1)))
        for t in all_tiles:
            accs = [acc_ref[t * HEADS_PER_TILE + hh] for hh in range(HEADS_PER_TILE)]
            o_t = jnp.concatenate([a[:HEAD_DIM] / a[HEAD_DIM:HEAD_DIM + 1] for a in accs], axis=0)
            o_ref[pl.ds(q0 + t * blk, blk), :] = o_t.T.astype(o_ref.dtype)
        return carry

    first_q = q_operands(0)
    for t in all_tiles:
        for hh in range(HEADS_PER_TILE):
            stage(buf_a, 0, first_q, t, hh)
    lax.fori_loop(0, n_groups, q_group, 0)


def _moba(qkv, bsz, seq):
    blk = MOBA_BLOCK
    assert seq % (blk * MOBA_Q_TILES) == 0
    n_chain = MOBA_Q_TILES * HEADS_PER_TILE
    n_blk = seq // blk
    n_sel = min(MOBA_TOPK, n_blk - 1)
    n_tiles = D_MODEL // LANES
    qkv3 = qkv.reshape(bsz, seq, 3 * D_MODEL)
    spec = lambda off: pl.BlockSpec((None, seq, LANES), lambda b, hp, off=off: (b, 0, off + hp))
    out = pl.pallas_call(
        functools.partial(_moba_kernel, n_blk=n_blk, n_sel=n_sel),
        grid=(bsz, n_tiles),
        in_specs=[spec(0), spec(n_tiles), spec(2 * n_tiles)],
        out_specs=pl.BlockSpec((None, seq, LANES), lambda b, hp: (b, 0, hp)),
        out_shape=jax.ShapeDtypeStruct((bsz, seq, D_MODEL), BF16),
        scratch_shapes=[pltpu.VMEM((HEADS_PER_TILE * n_blk, LANES), F32),
                        pltpu.VMEM((seq, HEADS_PER_TILE * LANES), BF16),
                        pltpu.VMEM((HEADS_PER_TILE * MOBA_VT_ROWS, seq), BF16),
                        pltpu.VMEM((seq, HEADS_PER_TILE * LANES), BF16),
                        pltpu.VMEM((n_chain, MOBA_VT_ROWS, blk), F32)]
                       + [pltpu.VMEM((MOBA_SUPER * blk, blk), F32)] * (2 * n_chain),
        compiler_params=_cparams("parallel", "parallel"),
        name="moba_attn",
    )(qkv3, qkv3, qkv3)
    return out.reshape(bsz * seq, D_MODEL)


def _rope_tables(positions):
    half = ROPE_DIM // 2
    inv = ROPE_THETA ** (-jnp.arange(0, ROPE_DIM, 2, dtype=F32) / ROPE_DIM)
    ang = positions.astype(F32).reshape(-1)[:, None] * inv
    cos_f, sin_f = jnp.cos(ang), jnp.sin(ang)
    j = jnp.arange(LANES, dtype=jnp.int32) % HEAD_DIM
    pick = jnp.logical_and(j[None, :] % half == jnp.arange(half)[:, None], j[None, :] < ROPE_DIM).astype(F32)
    cos = sum(cos_f[:, f:f + 1] * pick[f][None, :] for f in range(half))
    sin = sum(sin_f[:, f:f + 1] * pick[f][None, :] for f in range(half))
    cos = jnp.where(j < ROPE_DIM, cos, 1.0)
    s1 = jnp.where(j < half, -sin, 0.0)
    s2 = jnp.where(jnp.logical_and(j >= half, j < ROPE_DIM), sin, 0.0)
    return cos, s1, s2


def _qkv_weight(w):
    q_scale = HEAD_DIM ** -0.5 * math.log2(math.e)
    scale = jnp.concatenate([jnp.full((D_MODEL,), q_scale, F32), jnp.ones((2 * D_MODEL,), F32)])
    return (w.astype(F32) * scale[None, :]).astype(BF16)


def kernel(x, positions, norm_mix, norm_ffn, norm_final, s5_a_re, s5_a_im, s5_log_dt, s5_b_re, s5_b_im, s5_c_re, s5_c_im, s5_d, s5_w_glu, s5_b_glu, dil_w_qkv, dil_w_o, hgrn_w_in, hgrn_lower_bound, hgrn_norm, hgrn_w_o, moba_w_qkv, moba_w_o, ffn_w_gate_up, ffn_w_down):
    bsz, seq, d = x.shape
    depth = norm_mix.shape[0]
    n_mixers = 4
    t = bsz * seq
    rope = _rope_tables(positions)
    per = DIL_TILE // DIL_CLASSES
    pos_cm = positions.reshape(bsz, seq // DIL_TILE, per, DIL_CLASSES).transpose(0, 1, 3, 2)
    rope_cm = _rope_tables(pos_cm)
    wgu = ffn_w_gate_up.astype(BF16)
    wdn = ffn_w_down.astype(BF16)
    lb_w = jax.nn.softmax(hgrn_lower_bound.astype(F32), axis=0)
    lower_bounds = jnp.cumsum(lb_w, axis=0) - lb_w[0]
    norm_mix = norm_mix.astype(F32)
    norm_ffn = norm_ffn.astype(F32)

    h = x.reshape(t, d).astype(F32)
    for layer in range(depth):
        mixer, j = layer % n_mixers, layer // n_mixers
        final_g = norm_final.astype(F32) if layer == depth - 1 else None
        mix = None
        if mixer == 0:
            h = _s5(h, norm_mix[layer], s5_a_re[j], s5_a_im[j], s5_log_dt[j], s5_b_re[j], s5_b_im[j],
                    s5_c_re[j], s5_c_im[j], s5_d[j], s5_w_glu[j], s5_b_glu[j], seq)
        elif mixer == 1:
            qkv = _proj(h, norm_mix[layer], _qkv_weight(dil_w_qkv[j]), rope_cm, 2 * D_MODEL,
                        class_major=(bsz, seq, DIL_CLASSES), tm=DIL_TILE)
            mix = (_dilated(qkv, bsz, seq), dil_w_o[j].astype(BF16))
        elif mixer == 2:
            proj = _proj(h, norm_mix[layer], hgrn_w_in[j].astype(BF16), tm=512)
            mix = (_hgrn(proj, lower_bounds[layer], hgrn_norm[j], bsz, seq), hgrn_w_o[j].astype(BF16))
        else:
            qkv = _proj(h, norm_mix[layer], _qkv_weight(moba_w_qkv[j]), rope, 2 * D_MODEL, tm=512)
            mix = (_moba(qkv, bsz, seq), moba_w_o[j].astype(BF16))
        h = _ffn(h, norm_ffn[layer], wgu, wdn, layer, mix=mix, final_g=final_g)
    return h.reshape(bsz, seq, d).astype(x.dtype)
```

```python
import functools
import math

import jax
import jax.numpy as jnp
from jax import lax
from jax.experimental import pallas as pl
from jax.experimental.pallas import tpu as pltpu

F32 = jnp.float32
BF16 = jnp.bfloat16

D_MODEL = 1024
D_FF = 2816
RMS_EPS = 1e-6
NEG = -1e30
BELOW_NEG = -3e38

HEAD_DIM = 64
ROPE_DIM = HEAD_DIM // 4
ROPE_THETA = 500000.0
LANES = 128
SUBLANES = 8
HEADS_PER_TILE = LANES // HEAD_DIM

S5_GROUP = 16
S5_GROUPS = D_MODEL // S5_GROUP
S5_STATE = 64
S5_CH = S5_GROUPS * S5_STATE
S5_SLABS = D_MODEL // LANES
S5_SLAB_CH = S5_CH // S5_SLABS
S5_SCAN_SHIFTS = (2, 4)

DIL_PATTERNS = ((128, 1), (512, 4), (2048, 16))
DIL_BLOCK = 128

HGRN_HEAD_DIM = 128
HGRN_HEADS = D_MODEL // HGRN_HEAD_DIM
HGRN_CHUNK = 64
HGRN_SUB = 8

MOBA_BLOCK = 256
MOBA_TOPK = 3

VMEM_LIMIT = 56 * 1024 * 1024

NT_DIMS = (((1,), (1,)), ((), ()))
TN_DIMS = (((0,), (0,)), ((), ()))


def _cparams(*sem):
    return pltpu.CompilerParams(dimension_semantics=sem, vmem_limit_bytes=VMEM_LIMIT)


def _rms(x, g):
    return x * lax.rsqrt(jnp.mean(x * x, axis=-1, keepdims=True) + RMS_EPS) * g


def _silu(x):
    return x * jax.nn.sigmoid(x)


def _proj_kernel(*refs, n_out, tn, rope_cols, n_cls):
    if rope_cols:
        x_ref, g_ref, w_ref, c_ref, s1_ref, s2_ref, o_ref = refs
    else:
        x_ref, g_ref, w_ref, o_ref = refs
    tm = x_ref.shape[0]
    per = tm // n_cls
    xn = _rms(x_ref[...], g_ref[...]).astype(BF16)
    if n_cls > 1:
        dst = lax.broadcasted_iota(jnp.int32, (tm, tm), 0)
        src = lax.broadcasted_iota(jnp.int32, (tm, tm), 1)
        perm = (src == (dst % per) * n_cls + dst // per).astype(BF16)
        xn = jnp.dot(perm, xn, preferred_element_type=F32).astype(BF16)
    if rope_cols:
        cos, s1, s2 = c_ref[...], s1_ref[...], s2_ref[...]
    for c in range(n_out // tn):
        y = jnp.dot(xn, w_ref[:, c * tn:(c + 1) * tn], preferred_element_type=F32)
        if c * tn < rope_cols:
            parts = []
            for j in range(tn // LANES):
                yj = y[:, j * LANES:(j + 1) * LANES]
                half = ROPE_DIM // 2
                parts.append(yj * cos + pltpu.roll(yj, LANES - half, 1) * s1 + pltpu.roll(yj, half, 1) * s2)
            y = jnp.concatenate(parts, axis=1) if len(parts) > 1 else parts[0]
        if n_cls == 1:
            o_ref[:, c * tn:(c + 1) * tn] = y
        else:
            for r in range(n_cls):
                o_ref[r, :, c * tn:(c + 1) * tn] = y[r * per:(r + 1) * per]


def _proj(h, g, w, rope=None, rope_cols=0, class_major=None, tm=256, tn=256):
    t, d = h.shape
    n_out = w.shape[1]
    in_specs = [pl.BlockSpec((tm, d), lambda i: (i, 0)),
                pl.BlockSpec((1, d), lambda i: (0, 0)),
                pl.BlockSpec((d, n_out), lambda i: (0, 0))]
    args = [h, g.reshape(1, d), w]
    if rope_cols:
        in_specs += [pl.BlockSpec((tm, LANES), lambda i: (i, 0))] * 3
        args += list(rope)
    if class_major is None:
        n_cls = 1
        out_spec = pl.BlockSpec((tm, n_out), lambda i: (i, 0))
        out_shape = jax.ShapeDtypeStruct((t, n_out), F32)
    else:
        bsz, seq, n_cls = class_major
        tiles = seq // tm
        per = tm // n_cls
        assert seq % tm == 0 and per % SUBLANES == 0
        out_spec = pl.BlockSpec((None, n_cls, per, n_out), lambda i: (i // tiles, 0, i % tiles, 0))
        out_shape = jax.ShapeDtypeStruct((bsz, n_cls, seq // n_cls, n_out), F32)
    return pl.pallas_call(
        functools.partial(_proj_kernel, n_out=n_out, tn=tn, rope_cols=rope_cols, n_cls=n_cls),
        grid=(t // tm,),
        in_specs=in_specs,
        out_specs=out_spec,
        out_shape=out_shape,
        compiler_params=_cparams("parallel"),
        name="proj_rope" if rope_cols else "proj",
    )(*args)


def _ffn_kernel(*refs, has_mix, has_final, fc):
    refs = list(refs)
    h_ref, g_ref, wgu_ref, wd_ref = refs[:4]
    pos = 4
    if has_mix:
        a_ref, wo_ref = refs[pos:pos + 2]
        pos += 2
    if has_final:
        fg_ref = refs[pos]
        pos += 1
    o_ref, acc_ref = refs[pos], refs[pos + 1]

    h = h_ref[...]
    if has_mix:
        h = h + jnp.dot(a_ref[...], wo_ref[...], preferred_element_type=F32)
    xn = _rms(h, g_ref[...]).astype(BF16)

    def gate_up(c):
        gate = jnp.dot(xn, wgu_ref[:, c * fc:(c + 1) * fc], preferred_element_type=F32)
        up = jnp.dot(xn, wgu_ref[:, D_FF + c * fc:D_FF + (c + 1) * fc], preferred_element_type=F32)
        return gate, up

    n_chunks = D_FF // fc
    nxt = gate_up(0)
    for c in range(n_chunks):
        gate, up = nxt
        if c + 1 < n_chunks:
            nxt = gate_up(c + 1)
        act = (_silu(gate) * up).astype(BF16)
        contrib = jnp.dot(act, wd_ref[c * fc:(c + 1) * fc, :], preferred_element_type=F32)
        if c == 0:
            acc_ref[...] = contrib
        else:
            acc_ref[...] += contrib
    out = h + acc_ref[...]
    if has_final:
        out = _rms(out, fg_ref[...])
    o_ref[...] = out


def _ffn(h, g, wgu, wd, layer, mix=None, final_g=None, tm=512, fc=256):
    t, d = h.shape
    const = lambda i: (0, 0)
    in_specs = [pl.BlockSpec((tm, d), lambda i: (i, 0)),
                pl.BlockSpec((1, d), const),
                pl.BlockSpec((None, d, 2 * D_FF), lambda i: (layer, 0, 0)),
                pl.BlockSpec((None, D_FF, d), lambda i: (layer, 0, 0))]
    args = [h, g.reshape(1, d), wgu, wd]
    if mix is not None:
        a, wo = mix
        in_specs += [pl.BlockSpec((tm, d), lambda i: (i, 0)), pl.BlockSpec((d, d), const)]
        args += [a, wo]
    if final_g is not None:
        in_specs += [pl.BlockSpec((1, d), const)]
        args += [final_g.reshape(1, d)]
    return pl.pallas_call(
        functools.partial(_ffn_kernel, has_mix=mix is not None, has_final=final_g is not None, fc=fc),
        grid=(t // tm,),
        in_specs=in_specs,
        out_specs=pl.BlockSpec((tm, d), lambda i: (i, 0)),
        out_shape=jax.ShapeDtypeStruct((t, d), F32),
        scratch_shapes=[pltpu.VMEM((tm, d), F32)],
        compiler_params=_cparams("parallel"),
        name="ffn",
    )(*args)


def _s5_kernel(x_ref, g_ref, wbr_ref, wbi_ref, akr_ref, aki_ref, pr_ref, pi_ref, wcr_ref, wci_ref,
               d_ref, wglu_ref, bglu_ref, o_ref, er_ref, ei_ref, cr_ref, ci_ref, *, tm, tiles_per_seq):
    i = pl.program_id(0)

    @pl.when(i % tiles_per_seq == 0)
    def _():
        cr_ref[...] = jnp.zeros_like(cr_ref)
        ci_ref[...] = jnp.zeros_like(ci_ref)

    x = x_ref[...]
    u = _rms(x, g_ref[...])
    ub = u.astype(BF16)
    first = lax.broadcasted_iota(jnp.int32, (tm, 1), 0) % SUBLANES == 0
    ub_prev = jnp.where(first, 0.0, pltpu.roll(u, 1, 0)).astype(BF16)
    for s in range(S5_SLABS):
        us = jnp.concatenate([ub[:, s * LANES:(s + 1) * LANES], ub_prev[:, s * LANES:(s + 1) * LANES]], axis=1)
        er_ref[:, s * S5_SLAB_CH:(s + 1) * S5_SLAB_CH] = jnp.dot(us, wbr_ref[s], preferred_element_type=F32)
        ei_ref[:, s * S5_SLAB_CH:(s + 1) * S5_SLAB_CH] = jnp.dot(us, wbi_ref[s], preferred_element_type=F32)

    def group(j, carry):
        r0 = pl.multiple_of(j * SUBLANES, SUBLANES)
        xr = er_ref[pl.ds(r0, SUBLANES), :]
        xi = ei_ref[pl.ds(r0, SUBLANES), :]
        for lvl, k in enumerate(S5_SCAN_SHIFTS):
            sr = pltpu.roll(xr, k, 0)
            si = pltpu.roll(xi, k, 0)
            ar, ai = akr_ref[lvl], aki_ref[lvl]
            xr, xi = xr + ar * sr - ai * si, xi + ar * si + ai * sr
        cr, ci = cr_ref[...], ci_ref[...]
        pr, pi_ = pr_ref[...], pi_ref[...]
        hr = xr + pr * cr - pi_ * ci
        hi = xi + pr * ci + pi_ * cr
        er_ref[pl.ds(r0, SUBLANES), :] = hr
        ei_ref[pl.ds(r0, SUBLANES), :] = hi
        cr_ref[...] = jnp.broadcast_to(hr[SUBLANES - 1:SUBLANES, :], (SUBLANES, S5_CH))
        ci_ref[...] = jnp.broadcast_to(hi[SUBLANES - 1:SUBLANES, :], (SUBLANES, S5_CH))
        return carry

    lax.fori_loop(0, tm // SUBLANES, group, 0)

    ys = []
    for s in range(S5_SLABS):
        hr = er_ref[:, s * S5_SLAB_CH:(s + 1) * S5_SLAB_CH].astype(BF16)
        hi = ei_ref[:, s * S5_SLAB_CH:(s + 1) * S5_SLAB_CH].astype(BF16)
        ys.append(jnp.dot(hr, wcr_ref[s], preferred_element_type=F32)
                  - jnp.dot(hi, wci_ref[s], preferred_element_type=F32))
    y = jnp.concatenate(ys, axis=1) + d_ref[...] * u
    z = jax.nn.gelu(y).astype(BF16)
    zz = jnp.dot(z, wglu_ref[...], preferred_element_type=F32) + bglu_ref[...]
    o_ref[...] = x + zz[:, :D_MODEL] * jax.nn.sigmoid(zz[:, D_MODEL:])


def _block_diag_slabs(w):
    g, r, c = w.shape
    per = g // S5_SLABS
    w = w.reshape(S5_SLABS, per, r, c)
    eye = jnp.eye(per, dtype=w.dtype)
    return jnp.einsum('sgrc,gh->sgrhc', w, eye).reshape(S5_SLABS, per * r, per * c)


def _s5_tables(a_re, a_im, log_dt):
    lr, li = a_re.astype(F32), a_im.astype(F32)
    dt = jnp.exp(log_dt.astype(F32))[:, None]
    mag = jnp.exp(lr * dt)
    ab_re, ab_im = mag * jnp.cos(li * dt), mag * jnp.sin(li * dt)
    den = lr * lr + li * li
    m_re = ab_re - 1.0
    f_re = (m_re * lr + ab_im * li) / den
    f_im = (ab_im * lr - m_re * li) / den

    def power(k):
        return ((jnp.exp(lr * dt * k) * jnp.cos(li * dt * k)).reshape(-1),
                (jnp.exp(lr * dt * k) * jnp.sin(li * dt * k)).reshape(-1))

    rows = jnp.arange(SUBLANES)[:, None]
    akr, aki = [], []
    for k in S5_SCAN_SHIFTS:
        pr, pi_ = power(float(k))
        akr.append(jnp.where(rows >= k, pr[None, :], 0.0))
        aki.append(jnp.where(rows >= k, pi_[None, :], 0.0))
    pw = [power(float(k + 1)) for k in range(SUBLANES)]
    p_re = jnp.stack([p[0] for p in pw])
    p_im = jnp.stack([p[1] for p in pw])
    return f_re, f_im, ab_re, ab_im, jnp.stack(akr), jnp.stack(aki), p_re, p_im


def _s5(h, g, a_re, a_im, log_dt, b_re, b_im, c_re, c_im, d_skip, w_glu, b_glu, seq, tm=512):
    t, d = h.shape
    f_re, f_im, ab_re, ab_im, akr, aki, p_re, p_im = _s5_tables(a_re, a_im, log_dt)
    bw_re = f_re[:, :, None] * b_re - f_im[:, :, None] * b_im
    bw_im = f_re[:, :, None] * b_im + f_im[:, :, None] * b_re
    bw1_re = ab_re[:, :, None] * bw_re - ab_im[:, :, None] * bw_im
    bw1_im = ab_re[:, :, None] * bw_im + ab_im[:, :, None] * bw_re
    slabs = lambda w: _block_diag_slabs(jnp.swapaxes(w, 1, 2))
    wbr = jnp.concatenate([slabs(bw_re), slabs(bw1_re)], axis=1).astype(BF16)
    wbi = jnp.concatenate([slabs(bw_im), slabs(bw1_im)], axis=1).astype(BF16)
    wcr = _block_diag_slabs(jnp.swapaxes(c_re.astype(F32), 1, 2)).astype(BF16)
    wci = _block_diag_slabs(jnp.swapaxes(c_im.astype(F32), 1, 2)).astype(BF16)
    c2 = lambda i: (0, 0)
    c3 = lambda i: (0, 0, 0)
    return pl.pallas_call(
        functools.partial(_s5_kernel, tm=tm, tiles_per_seq=seq // tm),
        grid=(t // tm,),
        in_specs=[pl.BlockSpec((tm, d), lambda i: (i, 0)),
                  pl.BlockSpec((1, d), c2),
                  pl.BlockSpec(wbr.shape, c3), pl.BlockSpec(wbi.shape, c3),
                  pl.BlockSpec(akr.shape, c3), pl.BlockSpec(aki.shape, c3),
                  pl.BlockSpec(p_re.shape, c2), pl.BlockSpec(p_im.shape, c2),
                  pl.BlockSpec(wcr.shape, c3), pl.BlockSpec(wci.shape, c3),
                  pl.BlockSpec((1, d), c2),
                  pl.BlockSpec((d, 2 * d), c2),
                  pl.BlockSpec((1, 2 * d), c2)],
        out_specs=pl.BlockSpec((tm, d), lambda i: (i, 0)),
        out_shape=jax.ShapeDtypeStruct((t, d), F32),
        scratch_shapes=[pltpu.VMEM((tm, S5_CH), F32), pltpu.VMEM((tm, S5_CH), F32),
                        pltpu.VMEM((SUBLANES, S5_CH), F32), pltpu.VMEM((SUBLANES, S5_CH), F32)],
        compiler_params=_cparams("arbitrary"),
        name="s5",
    )(h, g.reshape(1, d), wbr, wbi, akr, aki, p_re, p_im, wcr, wci,
      d_skip.astype(F32).reshape(1, d), w_glu.astype(BF16), b_glu.astype(F32).reshape(1, 2 * d))


DIL_CLASSES = 16
DIL_STEPS = 2
DIL_TILE = 256


def _dil_kernel(*refs, seq):
    ncls = DIL_CLASSES
    u_len = seq // ncls
    blk = DIL_BLOCK
    qp_ref, kp_ref, vp_ref, o_ref = refs[:4]
    acc_ref, lsw_ref, m_ref, msw_ref = refs[4:8]
    n_chain = DIL_STEPS * HEADS_PER_TILE
    bufs = refs[8:]
    buf_a, buf_b = bufs[:n_chain], bufs[n_chain:]

    lane = lax.broadcasted_iota(jnp.int32, (1, LANES), 1)
    low = lane < HEAD_DIM
    n_iter = seq // blk // DIL_STEPS

    for pat, (window, dil) in enumerate(DIL_PATTERNS):
        assert window // dil == blk and ncls % dil == 0 and seq % (dil * blk) == 0
        n_run = ncls // dil
        run = SUBLANES * dil
        n_blk = u_len // run
        jq = lax.broadcasted_iota(jnp.int32, (blk, blk), 0)
        jk = lax.broadcasted_iota(jnp.int32, (blk, blk), 1)
        wq = (jq % run) * n_run + jq // run
        wk = (jk % run) * n_run + jk // run
        mask_cur = wk <= wq
        mask_prev = wk >= wq

        def run_rows(step, dil=dil, n_run=n_run, run=run, n_blk=n_blk):
            res = step // n_blk
            bi = step % n_blk
            base = [(res + dil * c) * u_len for c in range(n_run)]
            cur = [pl.multiple_of(b + run * bi, SUBLANES) for b in base]
            prev = [pl.multiple_of(b + run * jnp.maximum(bi - 1, 0), SUBLANES) for b in base]
            return cur, prev, bi > 0

        def gather(ref, offs, run=run):
            parts = [ref[pl.ds(o, run), :] for o in offs]
            return jnp.concatenate(parts, axis=0) if len(parts) > 1 else parts[0]

        def stage_block(bufs, step, t):
            cur, prev, _ = run_rows(step)
            q = gather(qp_ref, cur)
            kcat = jnp.concatenate([gather(kp_ref, prev), gather(kp_ref, cur)], axis=0).astype(BF16)
            for hh in range(HEADS_PER_TILE):
                qh = jnp.where(low if hh == 0 else jnp.logical_not(low), q, 0.0).astype(BF16)
                bufs[t * HEADS_PER_TILE + hh][...] = lax.dot_general(qh, kcat, NT_DIMS,
                                                                     preferred_element_type=F32)

        def phase(it, bufs, nxt_bufs, pat=pat):
            nxt = jnp.minimum(it + 1, n_iter - 1)
            stage_block(nxt_bufs, nxt * DIL_STEPS, 0)
            for t in range(DIL_STEPS):
                step = it * DIL_STEPS + t
                cur, prev, has_prev = run_rows(step)
                vcat = jnp.concatenate([gather(vp_ref, prev), gather(vp_ref, cur)], axis=0)
                mask = jnp.concatenate([jnp.logical_and(mask_prev, has_prev), mask_cur], axis=1)
                ms, os_ = [], []
                for hh in range(HEADS_PER_TILE):
                    c = t * HEADS_PER_TILE + hh
                    s = jnp.where(mask, bufs[c][...], NEG)
                    mb = jnp.max(s, axis=1, keepdims=True)
                    p = jnp.exp2(s - mb).astype(BF16)
                    vh = jnp.where(low if hh == 0 else jnp.logical_not(low), vcat, 1.0).astype(BF16)
                    os_.append(jnp.dot(p, vh, preferred_element_type=F32))
                    ms.append(jnp.broadcast_to(mb, (blk, LANES)))
                if t + 1 < DIL_STEPS:
                    stage_block(nxt_bufs, nxt * DIL_STEPS + t + 1, t + 1)
                m_b = jnp.where(low, ms[0], ms[1])
                msw_b = jnp.where(low, ms[1], ms[0])
                o_b = jnp.where(low, os_[0], os_[1])
                lsw_b = jnp.where(low, os_[1], os_[0])
                for ci, off in enumerate(cur):
                    rows = pl.ds(off, run)
                    piece = slice(ci * run, (ci + 1) * run)
                    if pat == 0:
                        m_ref[rows, :] = m_b[piece]
                        msw_ref[rows, :] = msw_b[piece]
                        acc_ref[rows, :] = o_b[piece]
                        lsw_ref[rows, :] = lsw_b[piece]
                    else:
                        m_o, msw_o = m_ref[rows, :], msw_ref[rows, :]
                        m_n = jnp.maximum(m_o, m_b[piece])
                        msw_n = jnp.maximum(msw_o, msw_b[piece])
                        acc_n = acc_ref[rows, :] * jnp.exp2(m_o - m_n) + o_b[piece] * jnp.exp2(m_b[piece] - m_n)
                        lsw_n = (lsw_ref[rows, :] * jnp.exp2(msw_o - msw_n)
                                 + lsw_b[piece] * jnp.exp2(msw_b[piece] - msw_n))
                        if pat + 1 < len(DIL_PATTERNS):
                            m_ref[rows, :] = m_n
                            msw_ref[rows, :] = msw_n
                            acc_ref[rows, :] = acc_n
                            lsw_ref[rows, :] = lsw_n
                        else:
                            den = pltpu.roll(lsw_n, HEAD_DIM, 1)
                            o_ref[rows, :] = (acc_n / den).astype(o_ref.dtype)

        def two_phases(i, carry, phase=phase):
            phase(2 * i, buf_a, buf_b)
            phase(2 * i + 1, buf_b, buf_a)
            return carry

        for t in range(DIL_STEPS):
            stage_block(buf_a, t, t)
        lax.fori_loop(0, n_iter // 2, two_phases, 0)


def _dilated(qkv, bsz, seq):
    ncls = DIL_CLASSES
    assert seq % (ncls * DIL_BLOCK) == 0 and (seq // DIL_BLOCK) % (2 * DIL_STEPS) == 0
    u_len = seq // ncls
    n_tiles = D_MODEL // LANES
    qkv3 = qkv.reshape(bsz, seq, 3 * D_MODEL)
    spec = lambda off: pl.BlockSpec((None, seq, LANES), lambda b, hp, off=off: (b, 0, off + hp))
    in_specs = [spec(off) for off in (0, n_tiles, 2 * n_tiles)]
    out = pl.pallas_call(
        functools.partial(_dil_kernel, seq=seq),
        grid=(bsz, n_tiles),
        in_specs=in_specs,
        out_specs=pl.BlockSpec((None, seq, LANES), lambda b, hp: (b, 0, hp)),
        out_shape=jax.ShapeDtypeStruct((bsz, seq, D_MODEL), BF16),
        scratch_shapes=[pltpu.VMEM((seq, LANES), F32)] * 4
                       + [pltpu.VMEM((DIL_BLOCK, 2 * DIL_BLOCK), F32)] * (2 * DIL_STEPS * HEADS_PER_TILE),
        compiler_params=_cparams("parallel", "parallel"),
        name="dilated_attn",
    )(*([qkv3] * len(in_specs)))
    out = out.reshape(bsz, ncls, u_len, D_MODEL).transpose(0, 2, 1, 3)
    return out.reshape(bsz * seq, D_MODEL)


def _hgrn_kernel(q_ref, f_ref, i_ref, g_ref, lb_ref, ng_ref, o_ref, st_ref, *, tc):
    @pl.when(pl.program_id(2) == 0)
    def _():
        st_ref[...] = jnp.zeros_like(st_ref)

    chunk, sub = HGRN_CHUNK, HGRN_SUB
    n_ch = tc // chunk
    n_sub = chunk // sub
    r = lax.broadcasted_iota(jnp.int32, (chunk, chunk), 0)
    c = lax.broadcasted_iota(jnp.int32, (chunk, chunk), 1)
    tril = (c <= r).astype(F32)
    trow = lax.broadcasted_iota(jnp.int32, (sub, 1), 0)
    lb = lb_ref[...]
    rows = [slice(ch * chunk, (ch + 1) * chunk) for ch in range(n_ch)]

    qs = [_silu(q_ref[rw, :]) for rw in rows]
    fg = [lb + (1.0 - lb) * jax.nn.sigmoid(f_ref[rw, :]) for rw in rows]
    kk = [1.0 - x for x in fg]
    iv = [i_ref[rw, :] for rw in rows]
    ivb = [x.astype(BF16) for x in iv]
    bc = [jnp.dot(tril, jnp.log(x), preferred_element_type=F32, precision=lax.Precision.HIGHEST) for x in fg]
    bl = [x[chunk - 1:chunk] for x in bc]

    upd = [lax.dot_general(ivb[ch], (kk[ch] * jnp.exp(bl[ch] - bc[ch])).astype(BF16), TN_DIMS,
                           preferred_element_type=F32) for ch in range(n_ch)]
    att = []
    for ch in range(n_ch):
        row_att = []
        for blk in range(1, n_sub):
            lo = blk * sub
            r_i = bc[ch][lo - 1:lo]
            qt = (qs[ch][lo:lo + sub] * jnp.exp(bc[ch][lo:lo + sub] - r_i)).astype(BF16)
            kt = (kk[ch][:lo] * jnp.exp(r_i - bc[ch][:lo])).astype(BF16)
            row_att.append(lax.dot_general(qt, kt, NT_DIMS, preferred_element_type=F32).astype(BF16))
        att.append(row_att)

    st_t = st_ref[...]
    states = []
    for ch in range(n_ch):
        states.append(st_t.astype(BF16))
        st_t = st_t * jnp.exp(bl[ch]) + upd[ch]
    st_ref[...] = st_t

    o_inter = [lax.dot_general((qs[ch] * jnp.exp(bc[ch])).astype(BF16), states[ch], NT_DIMS,
                               preferred_element_type=F32) for ch in range(n_ch)]
    o_off = [[jnp.dot(att[ch][blk - 1], ivb[ch][:blk * sub], preferred_element_type=F32)
              for blk in range(1, n_sub)] for ch in range(n_ch)]

    for ch in range(n_ch):
        outs = []
        for blk in range(n_sub):
            lo = blk * sub
            o_i = o_inter[ch][lo:lo + sub]
            if blk > 0:
                o_i = o_i + o_off[ch][blk - 1]
            q_i, k_i, iv_i, fg_i = (x[ch][lo:lo + sub] for x in (qs, kk, iv, fg))
            u = jnp.zeros_like(q_i)
            for s in reversed(range(sub)):
                decayed = u * fg_i[s + 1:s + 2] if s + 1 < sub else u
                u = jnp.where(trow == s, q_i, decayed)
                a_col = jnp.sum(u * k_i[s:s + 1], axis=1, keepdims=True)
                o_i = o_i + a_col * iv_i[s:s + 1]
            outs.append(o_i)
        o = jnp.concatenate(outs, axis=0)
        o = o * lax.rsqrt(jnp.mean(o * o, axis=-1, keepdims=True) + RMS_EPS) * ng_ref[...]
        o_ref[rows[ch], :] = (o * _silu(g_ref[rows[ch], :])).astype(o_ref.dtype)


def _hgrn(proj, lb, norm_g, bsz, seq, tc=2048):
    t = bsz * seq
    tiles = seq // tc
    hd = HGRN_HEAD_DIM
    spec = lambda off: pl.BlockSpec((tc, hd), lambda b, h, j, off=off: (b * tiles + j, off + h))
    return pl.pallas_call(
        functools.partial(_hgrn_kernel, tc=tc),
        grid=(bsz, HGRN_HEADS, tiles),
        in_specs=[spec(0), spec(HGRN_HEADS), spec(2 * HGRN_HEADS), spec(3 * HGRN_HEADS),
                  pl.BlockSpec((1, hd), lambda b, h, j: (0, h)),
                  pl.BlockSpec((1, hd), lambda b, h, j: (0, 0))],
        out_specs=spec(0),
        out_shape=jax.ShapeDtypeStruct((t, D_MODEL), BF16),
        scratch_shapes=[pltpu.VMEM((hd, hd), F32)],
        compiler_params=_cparams("parallel", "parallel", "arbitrary"),
        name="hgrn2",
    )(proj, proj, proj, proj, lb.reshape(1, D_MODEL), norm_g.astype(F32).reshape(1, hd))


MOBA_SUPER = 2
MOBA_Q_TILES = 4
MOBA_AHEAD = 2
MOBA_VT_ROWS = HEAD_DIM + 16


def _moba_kernel(q_ref, k_ref, v_ref, o_ref, km_ref, kp_ref, vt_ref, qp_ref, acc_ref, *s_refs, n_blk, n_sel):
    blk = MOBA_BLOCK
    sbk = MOBA_SUPER * blk
    lane = lax.broadcasted_iota(jnp.int32, (1, LANES), 1)
    head_of_lane = lane // HEAD_DIM
    hot_lane0 = [(1 - hh) * HEAD_DIM for hh in range(HEADS_PER_TILE)]
    n_rows = HEADS_PER_TILE * n_blk

    ones_row = (lax.broadcasted_iota(jnp.int32, (MOBA_VT_ROWS - HEAD_DIM, 1), 0) == 0).astype(BF16)
    for hh in range(HEADS_PER_TILE):
        vt_ref[hh * MOBA_VT_ROWS + HEAD_DIM:(hh + 1) * MOBA_VT_ROWS, :] = jnp.broadcast_to(
            ones_row, (MOBA_VT_ROWS - HEAD_DIM, n_blk * blk))
    for n in range(n_blk):
        rows = slice(n * blk, (n + 1) * blk)
        kb = k_ref[rows, :]
        v_t = v_ref[rows, :].T
        for hh in range(HEADS_PER_TILE):
            vt_ref[hh * MOBA_VT_ROWS:hh * MOBA_VT_ROWS + HEAD_DIM, rows] = (
                v_t[hh * HEAD_DIM:(hh + 1) * HEAD_DIM].astype(BF16))
        km = jnp.sum(kb, axis=0, keepdims=True) * (1.0 / blk)
        for hh in range(HEADS_PER_TILE):
            mine = head_of_lane == hh
            km_ref[hh * n_blk + n:hh * n_blk + n + 1, :] = jnp.where(mine, km, 0.0)
            hot = (lane == hot_lane0[hh] + n).astype(F32)
            kp_ref[rows, hh * LANES:(hh + 1) * LANES] = jnp.where(mine, kb, hot).astype(BF16)

    prow = lax.broadcasted_iota(jnp.int32, (n_rows, HEADS_PER_TILE * LANES), 0)
    pcol = lax.broadcasted_iota(jnp.int32, (n_rows, HEADS_PER_TILE * LANES), 1)
    target = jnp.zeros_like(prow)
    for hh in range(HEADS_PER_TILE):
        target = jnp.where(prow // n_blk == hh, hh * LANES + hot_lane0[hh] + prow % n_blk, target)
    place = (pcol == target).astype(BF16)
    nid = lax.broadcasted_iota(jnp.int32, (n_blk, blk), 0)
    nid_f = nid.astype(F32)
    km_all = km_ref[...]

    q_rows = [slice(qi * blk, (qi + 1) * blk) for qi in range(n_blk)]
    gates = [lax.dot_general(km_all, q_ref[rw, :], NT_DIMS, preferred_element_type=F32,
                             precision=lax.Precision.HIGHEST) for rw in q_rows]
    biases = []
    for qi in range(n_blk):
        past = nid < qi
        bias_rows = []
        for hh in range(HEADS_PER_TILE):
            g = jnp.where(past, gates[qi][hh * n_blk:(hh + 1) * n_blk], NEG)
            keep = nid == qi
            for _ in range(n_sel):
                mx = jnp.max(g, axis=0, keepdims=True)
                idx = jnp.min(jnp.where(g == mx, nid_f, float(n_blk)), axis=0, keepdims=True)
                pick = nid_f == idx
                keep = jnp.logical_or(keep, jnp.logical_and(pick, past))
                g = jnp.where(pick, BELOW_NEG, g)
            bias_rows.append(jnp.where(keep, 0.0, NEG))
        biases.append(jnp.concatenate(bias_rows, axis=0).astype(BF16))
    bias_qs = [lax.dot_general(b, place, TN_DIMS, preferred_element_type=F32) for b in biases]
    for qi in range(n_blk):
        q = q_ref[q_rows[qi], :]
        for hh in range(HEADS_PER_TILE):
            qp_ref[q_rows[qi], hh * LANES:(hh + 1) * LANES] = jnp.where(
                head_of_lane == hh, q, bias_qs[qi][:, hh * LANES:(hh + 1) * LANES]).astype(BF16)

    krow = lax.broadcasted_iota(jnp.int32, (sbk, blk), 0)
    qcol = lax.broadcasted_iota(jnp.int32, (sbk, blk), 1)
    rel = qcol - krow

    n_qt = MOBA_Q_TILES
    assert n_qt == 2 * MOBA_SUPER
    n_chain = n_qt * HEADS_PER_TILE
    buf_a, buf_b = s_refs[:n_chain], s_refs[n_chain:]
    all_tiles = tuple(range(n_qt))

    def q_operands(g):
        q0 = pl.multiple_of(g * n_qt * blk, n_qt * blk)
        return [[qp_ref[pl.ds(q0 + t * blk, blk), hh * LANES:(hh + 1) * LANES] for hh in range(HEADS_PER_TILE)]
                for t in range(n_qt)]

    def stage(bufs, k0, qps, t, hh):
        kpn = kp_ref[pl.ds(k0, sbk), hh * LANES:(hh + 1) * LANES]
        bufs[t * HEADS_PER_TILE + hh][...] = lax.dot_general(kpn, qps[t][hh], NT_DIMS, preferred_element_type=F32)

    n_groups = n_blk // n_qt

    def q_group(g, carry):
        q0 = pl.multiple_of(g * n_qt * blk, n_qt * blk)
        qps = q_operands(g)
        acc_ref[...] = jnp.zeros_like(acc_ref)

        def step(stats, cur, k0, tiles, causal, nxt=None, nxt_tiles=(), nxt_k0=None, nxt_q=qps):
            chains = [(t, hh) for t in tiles for hh in range(HEADS_PER_TILE)]
            ahead = [(t, hh) for t in nxt_tiles for hh in range(HEADS_PER_TILE)]
            nxt_k0 = k0 + sbk if nxt_k0 is None else nxt_k0
            v_t = [vt_ref[hh * MOBA_VT_ROWS:(hh + 1) * MOBA_VT_ROWS, pl.ds(k0, sbk)] for hh in range(HEADS_PER_TILE)]
            stats = list(stats)
            for t, hh in ahead[:MOBA_AHEAD]:
                stage(nxt, nxt_k0, nxt_q, t, hh)
            for i, (t, hh) in enumerate(chains):
                c = t * HEADS_PER_TILE + hh
                m_o = stats[c]
                s_t = cur[c][...]
                if t in causal:
                    s_t = jnp.where(rel >= causal[t], s_t, NEG)
                m_n = jnp.maximum(m_o, jnp.max(s_t, axis=0, keepdims=True))
                p = jnp.exp2(s_t - m_n).astype(BF16)
                acc_ref[c] = jnp.exp2(m_o - m_n) * acc_ref[c] + jnp.dot(v_t[hh], p, preferred_element_type=F32)
                stats[c] = m_n
                for t2, hh2 in ahead[i + MOBA_AHEAD:i + MOBA_AHEAD + 1]:
                    stage(nxt, nxt_k0, nxt_q, t2, hh2)
            for t2, hh2 in ahead[len(chains) + MOBA_AHEAD:]:
                stage(nxt, nxt_k0, nxt_q, t2, hh2)
            return tuple(stats)

        def pair(i, stats):
            k0 = pl.multiple_of(i * 2 * sbk, 2 * sbk)
            stats = step(stats, buf_a, k0, all_tiles, {}, buf_b, all_tiles)
            return step(stats, buf_b, k0 + sbk, all_tiles, {}, buf_a, all_tiles)

        init = (jnp.full((1, blk), BELOW_NEG, F32),) * n_chain
        stats = lax.fori_loop(0, g, pair, init)
        late = all_tiles[MOBA_SUPER:]
        stats = step(stats, buf_a, q0, all_tiles, {t: -t * blk for t in all_tiles[:MOBA_SUPER]}, buf_b, late)
        step(stats, buf_b, q0 + sbk, late, {t: -(t - MOBA_SUPER) * blk for t in late},
             buf_a, all_tiles, 0, q_operands(jnp.minimum(g + 1, n_groups - 1)))
        for t in all_tiles:
            accs = [acc_ref[t * HEADS_PER_TILE + hh] for hh in range(HEADS_PER_TILE)]
            o_t = jnp.concatenate([a[:HEAD_DIM] / a[HEAD_DIM:HEAD_DIM + 1] for a in accs], axis=0)
            o_ref[pl.ds(q0 + t * blk, blk), :] = o_t.T.astype(o_ref.dtype)
        return carry

    first_q = q_operands(0)
    for t in all_tiles:
        for hh in range(HEADS_PER_TILE):
            stage(buf_a, 0, first_q, t, hh)
    lax.fori_loop(0, n_groups, q_group, 0)


def _moba(qkv, bsz, seq):
    blk = MOBA_BLOCK
    assert seq % (blk * MOBA_Q_TILES) == 0
    n_chain = MOBA_Q_TILES * HEADS_PER_TILE
    n_blk = seq // blk
    n_sel = min(MOBA_TOPK, n_blk - 1)
    n_tiles = D_MODEL // LANES
    qkv3 = qkv.reshape(bsz, seq, 3 * D_MODEL)
    spec = lambda off: pl.BlockSpec((None, seq, LANES), lambda b, hp, off=off: (b, 0, off + hp))
    out = pl.pallas_call(
        functools.partial(_moba_kernel, n_blk=n_blk, n_sel=n_sel),
        grid=(bsz, n_tiles),
        in_specs=[spec(0), spec(n_tiles), spec(2 * n_tiles)],
        out_specs=pl.BlockSpec((None, seq, LANES), lambda b, hp: (b, 0, hp)),
        out_shape=jax.ShapeDtypeStruct((bsz, seq, D_MODEL), BF16),
        scratch_shapes=[pltpu.VMEM((HEADS_PER_TILE * n_blk, LANES), F32),
                        pltpu.VMEM((seq, HEADS_PER_TILE * LANES), BF16),
                        pltpu.VMEM((HEADS_PER_TILE * MOBA_VT_ROWS, seq), BF16),
                        pltpu.VMEM((seq, HEADS_PER_TILE * LANES), BF16),
                        pltpu.VMEM((n_chain, MOBA_VT_ROWS, blk), F32)]
                       + [pltpu.VMEM((MOBA_SUPER * blk, blk), F32)] * (2 * n_chain),
        compiler_params=_cparams("parallel", "parallel"),
        name="moba_attn",
    )(qkv3, qkv3, qkv3)
    return out.reshape(bsz * seq, D_MODEL)


def _rope_tables(positions):
    half = ROPE_DIM // 2
    inv = ROPE_THETA ** (-jnp.arange(0, ROPE_DIM, 2, dtype=F32) / ROPE_DIM)
    ang = positions.astype(F32).reshape(-1)[:, None] * inv
    cos_f, sin_f = jnp.cos(ang), jnp.sin(ang)
    j = jnp.arange(LANES, dtype=jnp.int32) % HEAD_DIM
    pick = jnp.logical_and(j[None, :] % half == jnp.arange(half)[:, None], j[None, :] < ROPE_DIM).astype(F32)
    cos = sum(cos_f[:, f:f + 1] * pick[f][None, :] for f in range(half))
    sin = sum(sin_f[:, f:f + 1] * pick[f][None, :] for f in range(half))
    cos = jnp.where(j < ROPE_DIM, cos, 1.0)
    s1 = jnp.where(j < half, -sin, 0.0)
    s2 = jnp.where(jnp.logical_and(j >= half, j < ROPE_DIM), sin, 0.0)
    return cos, s1, s2


def _qkv_weight(w):
    q_scale = HEAD_DIM ** -0.5 * math.log2(math.e)
    scale = jnp.concatenate([jnp.full((D_MODEL,), q_scale, F32), jnp.ones((2 * D_MODEL,), F32)])
    return (w.astype(F32) * scale[None, :]).astype(BF16)


def kernel(x, positions, norm_mix, norm_ffn, norm_final, s5_a_re, s5_a_im, s5_log_dt, s5_b_re, s5_b_im, s5_c_re, s5_c_im, s5_d, s5_w_glu, s5_b_glu, dil_w_qkv, dil_w_o, hgrn_w_in, hgrn_lower_bound, hgrn_norm, hgrn_w_o, moba_w_qkv, moba_w_o, ffn_w_gate_up, ffn_w_down):
    bsz, seq, d = x.shape
    depth = norm_mix.shape[0]
    n_mixers = 4
    t = bsz * seq
    rope = _rope_tables(positions)
    per = DIL_TILE // DIL_CLASSES
    pos_cm = positions.reshape(bsz, seq // DIL_TILE, per, DIL_CLASSES).transpose(0, 1, 3, 2)
    rope_cm = _rope_tables(pos_cm)
    wgu = ffn_w_gate_up.astype(BF16)
    wdn = ffn_w_down.astype(BF16)
    lb_w = jax.nn.softmax(hgrn_lower_bound.astype(F32), axis=0)
    lower_bounds = jnp.cumsum(lb_w, axis=0) - lb_w[0]
    norm_mix = norm_mix.astype(F32)
    norm_ffn = norm_ffn.astype(F32)

    h = x.reshape(t, d).astype(F32)
    for layer in range(depth):
        mixer, j = layer % n_mixers, layer // n_mixers
        final_g = norm_final.astype(F32) if layer == depth - 1 else None
        mix = None
        if mixer == 0:
            h = _s5(h, norm_mix[layer], s5_a_re[j], s5_a_im[j], s5_log_dt[j], s5_b_re[j], s5_b_im[j],
                    s5_c_re[j], s5_c_im[j], s5_d[j], s5_w_glu[j], s5_b_glu[j], seq)
        elif mixer == 1:
            qkv = _proj(h, norm_mix[layer], _qkv_weight(dil_w_qkv[j]), rope_cm, 2 * D_MODEL,
                        class_major=(bsz, seq, DIL_CLASSES), tm=DIL_TILE)
            mix = (_dilated(qkv, bsz, seq), dil_w_o[j].astype(BF16))
        elif mixer == 2:
            proj = _proj(h, norm_mix[layer], hgrn_w_in[j].astype(BF16), tm=512)
            mix = (_hgrn(proj, lower_bounds[layer], hgrn_norm[j], bsz, seq), hgrn_w_o[j].astype(BF16))
        else:
            qkv = _proj(h, norm_mix[layer], _qkv_weight(moba_w_qkv[j]), rope, 2 * D_MODEL, tm=512)
            mix = (_moba(qkv, bsz, seq), moba_w_o[j].astype(BF16))
        h = _ffn(h, norm_ffn[layer], wgu, wdn, layer, mix=mix, final_g=final_g)
    return h.reshape(bsz, seq, d).astype(x.dtype)
```
